```python
import jax, jax.numpy as jnp
from jax import lax
import numpy as np

D_MODEL = 1024
BATCH = 2
SEQ = 16384
DEPTH = 1
DEC_BATCH = 1
DEC_SEQ = 16384
PAST_LEN = 128

N_MEM = 256
RET_HEADS = 4
RET_DK = D_MODEL // RET_HEADS
RET_DV = D_MODEL // RET_HEADS
RET_WIDTH = RET_HEADS * RET_DK
RET_CHUNK = 128
ROPE_THETA = 10000.0
FOUR_GROUPS = 4
FOUR_WIDTH = D_MODEL
FOUR_GROUP_DIM = FOUR_WIDTH // FOUR_GROUPS
CA_HEADS = 4
CA_DIM = D_MODEL // CA_HEADS
D_FF = 4 * D_MODEL
EPS = 1e-6
GN_EPS = 1e-5
IN_WIDTH = 4 * RET_WIDTH + FOUR_WIDTH + 2 * D_MODEL
IN_SPLITS = [RET_WIDTH, 2 * RET_WIDTH, 3 * RET_WIDTH, 4 * RET_WIDTH,
             4 * RET_WIDTH + FOUR_WIDTH, 4 * RET_WIDTH + FOUR_WIDTH + D_MODEL]

kernel_name = "hybrid_retention_fnet_encoder"


def rms_norm(x, w):
    x32 = x.astype(jnp.float32)
    y = x32 * lax.rsqrt(jnp.mean(x32 * x32, axis=-1, keepdims=True) + EPS)
    return (y * w.astype(jnp.float32)).astype(x.dtype)


def rotary(x):
    s, d = x.shape[2], x.shape[3]
    inv = ROPE_THETA ** (-jnp.arange(0, d, 2, dtype=jnp.float32) / d)
    ang = jnp.arange(s, dtype=jnp.float32)[:, None] * inv[None, :]
    cos, sin = jnp.cos(ang), jnp.sin(ang)
    x1, x2 = x[..., : d // 2], x[..., d // 2:]
    return jnp.concatenate([x1 * cos - x2 * sin, x2 * cos + x1 * sin], axis=-1)


def retention_scan(q, k, v, log_gamma, include_diag):
    b, h, s, dk = q.shape
    dv = v.shape[-1]
    c = RET_CHUNK
    n_chunks = s // c
    pos = jnp.arange(c, dtype=jnp.float32)
    diff = pos[:, None] - pos[None, :]
    mask = (diff >= 0) if include_diag else (diff > 0)
    lg = log_gamma[:, None, None]
    decay_in = jnp.where(mask[None], jnp.exp(lg * jnp.where(mask, diff, 0.0)[None]), 0.0)
    xi = jnp.exp(log_gamma[:, None] * (pos + 1.0)[None, :])
    zeta = jnp.exp(log_gamma[:, None] * (c - 1.0 - pos)[None, :])
    g_chunk = jnp.exp(log_gamma * c)

    def to_chunks(t):
        return t.reshape(b, h, n_chunks, c, t.shape[-1]).transpose(2, 0, 1, 3, 4)

    def step(state, inp):
        qc, kc, vc = inp
        scores = jnp.einsum('bhid,bhjd->bhij', qc, kc) * decay_in[None]
        inner = jnp.einsum('bhij,bhjv->bhiv', scores, vc)
        cross = jnp.einsum('bhid,bhdv->bhiv', qc, state) * xi[None, :, :, None]
        new_state = state * g_chunk[None, :, None, None] + jnp.einsum(
            'bhjd,bhjv->bhdv', kc * zeta[None, :, :, None], vc)
        return new_state, inner + cross

    state0 = jnp.zeros((b, h, dk, dv), jnp.float32)
    _, ys = lax.scan(step, state0, (to_chunks(q), to_chunks(k), to_chunks(v)))
    return ys.transpose(1, 2, 0, 3, 4).reshape(b, h, s, dv)


def bidirectional_retention(q, k, v, decay_fwd, decay_bwd):
    lg_f = jax.nn.log_sigmoid(decay_fwd.astype(jnp.float32))
    lg_b = jax.nn.log_sigmoid(decay_bwd.astype(jnp.float32))
    y_f = retention_scan(q, k, v, lg_f, True)
    flip = lambda t: jnp.flip(t, axis=2)
    y_b = flip(retention_scan(flip(q), flip(k), flip(v), lg_b, False))
    return y_f + y_b


def encoder_layer(x, mem, norm_mix_w, w_in, ret_decay_fwd, ret_decay_bwd, ret_gn_w,
                  w_ret_out, w_four_out, w_mix_out, norm_ca_w, norm_mem_w,
                  w_cq, w_ck, w_cv, w_co, norm_mlp_w, w_up, w_down):
    b, s, _ = x.shape
    xn = rms_norm(x, norm_mix_w)
    proj = xn @ w_in
    q, k, v, g, u, gate_r, gate_f = jnp.split(proj, IN_SPLITS, axis=-1)

    def heads(t, d):
        return t.reshape(b, s, RET_HEADS, d).transpose(0, 2, 1, 3).astype(jnp.float32)

    q = rotary(heads(q, RET_DK)) * (RET_DK ** -0.5)
    k = rotary(heads(k, RET_DK))
    v = heads(v, RET_DV)
    r = bidirectional_retention(q, k, v, ret_decay_fwd, ret_decay_bwd)
    mu = jnp.mean(r, axis=-1, keepdims=True)
    var = jnp.mean(jnp.square(r - mu), axis=-1, keepdims=True)
    r = (r - mu) * lax.rsqrt(var + GN_EPS)
    r = r.transpose(0, 2, 1, 3).reshape(b, s, RET_WIDTH) * ret_gn_w.astype(jnp.float32)
    r = (jax.nn.silu(g.astype(jnp.float32)) * r).astype(x.dtype)
    ret_branch = r @ w_ret_out

    uf = u.reshape(b, s, FOUR_GROUPS, FOUR_GROUP_DIM).astype(jnp.float32)
    uf = jnp.fft.fft2(uf, axes=(1, 3), norm="ortho").real
    four_branch = uf.reshape(b, s, FOUR_WIDTH).astype(x.dtype) @ w_four_out

    merged = jax.nn.sigmoid(gate_r) * ret_branch + jax.nn.sigmoid(gate_f) * four_branch
    x = x + merged @ w_mix_out

    hq = rms_norm(x, norm_ca_w) @ w_cq
    mn = rms_norm(mem, norm_mem_w)
    cq = hq.reshape(b, s, CA_HEADS, CA_DIM)
    ck = (mn @ w_ck).reshape(b, mem.shape[1], CA_HEADS, CA_DIM)
    cv = (mn @ w_cv).reshape(b, mem.shape[1], CA_HEADS, CA_DIM)
    logits = jnp.einsum('bshd,bmhd->bhsm', cq, ck).astype(jnp.float32) * (CA_DIM ** -0.5)
    probs = jax.nn.softmax(logits, axis=-1).astype(x.dtype)
    att = jnp.einsum('bhsm,bmhd->bshd', probs, cv).reshape(b, s, D_MODEL)
    x = x + att @ w_co

    hid = jnp.square(jax.nn.relu(rms_norm(x, norm_mlp_w) @ w_up))
    x = x + hid @ w_down
    return x


def trunk(x, mem, norm_mix_w, w_in, ret_decay_fwd, ret_decay_bwd, ret_gn_w,
          w_ret_out, w_four_out, w_mix_out, norm_ca_w, norm_mem_w,
          w_cq, w_ck, w_cv, w_co, norm_mlp_w, w_up, w_down, norm_final_w):
    for l in range(DEPTH):
        x = encoder_layer(x, mem, norm_mix_w[l], w_in[l], ret_decay_fwd[l], ret_decay_bwd[l],
                          ret_gn_w[l], w_ret_out[l], w_four_out[l], w_mix_out[l],
                          norm_ca_w[l], norm_mem_w[l], w_cq[l], w_ck[l], w_cv[l], w_co[l],
                          norm_mlp_w[l], w_up[l], w_down[l])
    return rms_norm(x, norm_final_w)


def setup_inputs(seed: int = 0) -> dict:
    key = jax.random.key(seed)
    ks = jax.random.split(key, 24)
    f32 = jnp.float32

    def w(k, shape, fan_in):
        return jax.random.normal(k, shape, f32) * (fan_in ** -0.5)

    def gain(k, shape):
        return jnp.ones(shape, f32) + 0.01 * jax.random.normal(k, shape, f32)

    base_decay = jnp.log(2.0 ** (5.0 + jnp.arange(RET_HEADS, dtype=f32)) - 1.0)
    return {
        "x_prompt": jax.random.normal(ks[0], (BATCH, SEQ, D_MODEL), f32),
        "x_sample": jax.random.normal(ks[1], (DEC_BATCH, DEC_SEQ, D_MODEL), f32),
        "mem_prompt": jax.random.normal(ks[2], (BATCH, N_MEM, D_MODEL), f32),
        "mem_sample": jax.random.normal(ks[3], (DEC_BATCH, N_MEM, D_MODEL), f32),
        "norm_mix_w": gain(ks[4], (DEPTH, D_MODEL)),
        "w_in": w(ks[5], (DEPTH, D_MODEL, IN_WIDTH), D_MODEL),
        "ret_decay_fwd": base_decay[None, :] + 0.1 * jax.random.normal(ks[6], (DEPTH, RET_HEADS), f32),
        "ret_decay_bwd": base_decay[None, :] + 0.1 * jax.random.normal(ks[7], (DEPTH, RET_HEADS), f32),
        "ret_gn_w": gain(ks[8], (DEPTH, RET_WIDTH)),
        "w_ret_out": w(ks[9], (DEPTH, RET_WIDTH, D_MODEL), RET_WIDTH),
        "w_four_out": w(ks[10], (DEPTH, FOUR_WIDTH, D_MODEL), FOUR_WIDTH),
        "w_mix_out": w(ks[11], (DEPTH, D_MODEL, D_MODEL), D_MODEL),
        "norm_ca_w": gain(ks[12], (DEPTH, D_MODEL)),
        "norm_mem_w": gain(ks[13], (DEPTH, D_MODEL)),
        "w_cq": w(ks[14], (DEPTH, D_MODEL, D_MODEL), D_MODEL),
        "w_ck": w(ks[15], (DEPTH, D_MODEL, D_MODEL), D_MODEL),
        "w_cv": w(ks[16], (DEPTH, D_MODEL, D_MODEL), D_MODEL),
        "w_co": w(ks[17], (DEPTH, D_MODEL, D_MODEL), D_MODEL),
        "norm_mlp_w": gain(ks[18], (DEPTH, D_MODEL)),
        "w_up": w(ks[19], (DEPTH, D_MODEL, D_FF), D_MODEL),
        "w_down": w(ks[20], (DEPTH, D_FF, D_MODEL), D_FF),
        "norm_final_w": gain(ks[21], (D_MODEL,)),
    }


def reference(x_prompt, x_sample, mem_prompt, mem_sample, norm_mix_w, w_in, ret_decay_fwd,
              ret_decay_bwd, ret_gn_w, w_ret_out, w_four_out, w_mix_out, norm_ca_w, norm_mem_w,
              w_cq, w_ck, w_cv, w_co, norm_mlp_w, w_up, w_down, norm_final_w):
    y_prompt = trunk(x_prompt, mem_prompt, norm_mix_w, w_in, ret_decay_fwd, ret_decay_bwd,
                     ret_gn_w, w_ret_out, w_four_out, w_mix_out, norm_ca_w, norm_mem_w,
                     w_cq, w_ck, w_cv, w_co, norm_mlp_w, w_up, w_down, norm_final_w)
    y_sample = trunk(x_sample, mem_sample, norm_mix_w, w_in, ret_decay_fwd, ret_decay_bwd,
                     ret_gn_w, w_ret_out, w_four_out, w_mix_out, norm_ca_w, norm_mem_w,
                     w_cq, w_ck, w_cv, w_co, norm_mlp_w, w_up, w_down, norm_final_w)
    return (y_prompt, y_sample)
```

```python
import functools

import numpy as np
import jax
import jax.numpy as jnp
from jax import lax
from jax.experimental import pallas as pl
from jax.experimental.pallas import tpu as pltpu

RET_HEADS = 4
FOUR_GROUPS = 4
CA_HEADS = 4
ROPE_THETA = 10000.0
EPS = 1e-6
GN_EPS = 1e-5

V7X_VMEM_BYTES = 64 * 1024 * 1024
V7X_MXU_DIM = 256

RET_CHUNK = V7X_MXU_DIM
FFT_N2 = 128

BF16 = jnp.bfloat16
F32 = jnp.float32


def _vmem_limit(nbytes):
    return int(min(nbytes + 8 * 1024 * 1024, V7X_VMEM_BYTES - 6 * 1024 * 1024))


def _rms(x, w):
    return x * lax.rsqrt(jnp.mean(x * x, axis=-1, keepdims=True) + EPS) * w


def _dot(a, b):
    return jnp.dot(a, b, preferred_element_type=F32)


def _resident(shape):
    return pl.BlockSpec(shape, lambda *_: (0,) * len(shape), pipeline_mode=pl.Buffered(1))


@functools.lru_cache(maxsize=None)
def _rope_tables(s, dk):
    inv = ROPE_THETA ** (-np.arange(0, dk, 2, dtype=np.float64) / dk)
    ang = np.arange(s, dtype=np.float64)[:, None] * inv[None, :]
    return np.cos(ang).astype(np.float32), np.sin(ang).astype(np.float32)


@functools.lru_cache(maxsize=None)
def _chan_dft(n):
    ang = 2.0 * np.pi * np.outer(np.arange(n), np.arange(n)) / n
    return np.concatenate([np.cos(ang), -np.sin(ang)], axis=1).astype(np.float32)


@functools.lru_cache(maxsize=None)
def _fft_a_table(n1, n2):
    n = n1 * n2
    k1 = np.arange(n1, dtype=np.float64)[None, :, None]
    tok = (n2 * np.arange(n1, dtype=np.float64)[None, None, :]
           + np.arange(n2, dtype=np.float64)[:, None, None])
    ang = 2.0 * np.pi * np.mod(k1 * tok, n) / n
    c, s = np.cos(ang), np.sin(ang)
    top = np.concatenate([c, s], axis=2)
    bot = np.concatenate([-s, c], axis=2)
    return np.concatenate([top, bot], axis=1).astype(np.float32)


@functools.lru_cache(maxsize=None)
def _fft_b_table(n2, scale):
    ang = 2.0 * np.pi * np.outer(np.arange(n2), np.arange(n2)) / n2
    return (scale * np.concatenate([np.cos(ang), np.sin(ang)], axis=1)).astype(np.float32)


def _mem_kv_kernel(mem_ref, nw_ref, wk_ref, wv_ref, ck_ref, cv_ref):
    mn = _rms(mem_ref[0], nw_ref[...]).astype(BF16)
    ck_ref[0] = _dot(mn, wk_ref[...]).astype(BF16)
    cv_ref[0] = _dot(mn, wv_ref[...]).astype(BF16)


def _mem_kv(mem, norm_w, w_ck, w_cv):
    b, m, d = mem.shape
    out = jax.ShapeDtypeStruct((b, m, d), BF16)
    blk = pl.BlockSpec((1, m, d), lambda i: (i, 0, 0))
    return pl.pallas_call(
        _mem_kv_kernel,
        out_shape=(out, out),
        grid=(b,),
        in_specs=[blk, _resident((1, d)), _resident((d, d)), _resident((d, d))],
        out_specs=(blk, blk),
        compiler_params=pltpu.CompilerParams(
            dimension_semantics=("arbitrary",),
            vmem_limit_bytes=_vmem_limit(2 * d * d * 2 + 4 * m * d * 4)),
        name="mem_kv",
    )(mem, norm_w, w_ck, w_cv)


def _in_proj_kernel(x_ref, nw_ref, w_ref, cos_ref, sin_ref, dft_ref,
                    q_ref, k_ref, v_ref, g_ref, z_ref, gr_ref, gf_ref):
    d = x_ref.shape[-1]
    dk = d // RET_HEADS
    half = dk // 2
    gd = d // FOUR_GROUPS
    xb = _rms(x_ref[0], nw_ref[...]).astype(BF16)
    cos = cos_ref[...]
    sin = sin_ref[...]

    def proj(j):
        return _dot(xb, w_ref[:, j * d:(j + 1) * d])

    def rope(p, out_ref, scale):
        for h in range(RET_HEADS):
            x1 = p[:, h * dk:h * dk + half]
            x2 = p[:, h * dk + half:(h + 1) * dk]
            o1 = x1 * cos - x2 * sin
            o2 = x2 * cos + x1 * sin
            if scale is not None:
                o1 = o1 * scale
                o2 = o2 * scale
            out_ref[0, :, h * dk:h * dk + half] = o1.astype(BF16)
            out_ref[0, :, h * dk + half:(h + 1) * dk] = o2.astype(BF16)

    rope(proj(0), q_ref, dk ** -0.5)
    rope(proj(1), k_ref, None)
    v_ref[0] = proj(2).astype(BF16)
    g_ref[0] = proj(3).astype(BF16)
    u = proj(4).astype(BF16)
    for gi in range(FOUR_GROUPS):
        ab = _dot(u[:, gi * gd:(gi + 1) * gd], dft_ref[...])
        z_ref[0, 0, :, gi * gd:(gi + 1) * gd] = ab[:, :gd].astype(BF16)
        z_ref[0, 1, :, gi * gd:(gi + 1) * gd] = ab[:, gd:].astype(BF16)
    gr_ref[0] = proj(5).astype(BF16)
    gf_ref[0] = proj(6).astype(BF16)


def _in_proj(x, norm_w, w_in, cos, sin, dft, tm):
    b, s, d = x.shape
    half = d // RET_HEADS // 2
    gd = d // FOUR_GROUPS
    act = jax.ShapeDtypeStruct((b, s, d), BF16)
    zsh = jax.ShapeDtypeStruct((b, 2, s, d), BF16)
    row = pl.BlockSpec((1, tm, d), lambda i, j: (i, j, 0))
    zrow = pl.BlockSpec((1, 2, tm, d), lambda i, j: (i, 0, j, 0))
    tab = pl.BlockSpec((tm, half), lambda i, j: (j, 0))
    vmem = (w_in.size * 2 + 2 * tm * d * 4 + 2 * 8 * tm * d * 2 + 4 * tm * half * 4
            + dft.size * 2 + 4 * tm * d * 4)
    return pl.pallas_call(
        _in_proj_kernel,
        out_shape=(act, act, act, act, zsh, act, act),
        grid=(b, s // tm),
        in_specs=[row, _resident((1, d)), _resident(w_in.shape), tab, tab,
                  _resident((gd, 2 * gd))],
        out_specs=(row, row, row, row, zrow, row, row),
        compiler_params=pltpu.CompilerParams(
            dimension_semantics=("parallel", "parallel"),
            vmem_limit_bytes=_vmem_limit(vmem)),
        name="in_proj",
    )(x, norm_w, w_in, cos, sin, dft)


def _log_sigmoid(x):
    return jnp.minimum(x, 0.0) - jnp.log1p(jnp.exp(-jnp.abs(x)))


def _ret_bwd_kernel(dec_ref, q_ref, k_ref, v_ref, yb_ref, state_ref, xi_ref, zt_ref, gc_ref):
    c = RET_CHUNK
    t_len, d = q_ref.shape[1], q_ref.shape[2]
    dk = d // RET_HEADS

    @pl.when(pl.program_id(1) == 0)
    def _init():
        state_ref[...] = jnp.zeros_like(state_ref)
        for h in range(RET_HEADS):
            lg = _log_sigmoid(jnp.full((c, dk), dec_ref[h], F32))
            row = lax.broadcasted_iota(jnp.int32, (c, dk), 0).astype(F32)
            col = lax.broadcasted_iota(jnp.int32, (dk, c), 1).astype(F32)
            xi_ref[h] = jnp.exp(lg * (c - row))
            zt_ref[h] = jnp.exp(lg * col)
            gc_ref[h] = jnp.exp(_log_sigmoid(jnp.full((1, dk), dec_ref[h], F32)) * c)

    for h in range(RET_HEADS):
        cols = slice(h * dk, (h + 1) * dk)
        state = state_ref[h]
        for ci in reversed(range(t_len // c)):
            rows = slice(ci * c, (ci + 1) * c)
            q = q_ref[0, rows, cols]
            k = k_ref[0, rows, cols]
            v = v_ref[0, rows, cols]
            cross = _dot(q, state.astype(BF16)) * xi_ref[h]
            yb_ref[0, rows, cols] = cross.astype(BF16)
            kt = (k.astype(F32).T * zt_ref[h]).astype(BF16)
            state = state * gc_ref[h] + _dot(kt, v)
        state_ref[h] = state


def _ret_fwd_kernel(decf_ref, decb_ref, q_ref, k_ref, v_ref, g_ref, yb_ref, gnw_ref, r_ref,
                    state_ref, dmat_ref, xi_ref, zt_ref, gc_ref):
    c = RET_CHUNK
    t_len, d = q_ref.shape[1], q_ref.shape[2]
    dk = d // RET_HEADS

    @pl.when(pl.program_id(1) == 0)
    def _init():
        state_ref[...] = jnp.zeros_like(state_ref)
        for h in range(RET_HEADS):
            lgf = _log_sigmoid(jnp.full((c, dk), decf_ref[h], F32))
            lgb = _log_sigmoid(jnp.full((c, c), decb_ref[h], F32))
            row = lax.broadcasted_iota(jnp.int32, (c, dk), 0).astype(F32)
            col = lax.broadcasted_iota(jnp.int32, (dk, c), 1).astype(F32)
            xi_ref[h] = jnp.exp(lgf * (row + 1.0))
            zt_ref[h] = jnp.exp(lgf * (c - 1.0 - col))
            gc_ref[h] = jnp.exp(_log_sigmoid(jnp.full((1, dk), decf_ref[h], F32)) * c)
            i = lax.broadcasted_iota(jnp.int32, (c, c), 0)
            j = lax.broadcasted_iota(jnp.int32, (c, c), 1)
            diff = (i - j).astype(F32)
            lgf_cc = _log_sigmoid(jnp.full((c, c), decf_ref[h], F32))
            dmat_ref[h] = jnp.where(i >= j, jnp.exp(lgf_cc * jnp.maximum(diff, 0.0)),
                                    jnp.exp(lgb * jnp.maximum(-diff, 0.0)))

    for h in range(RET_HEADS):
        cols = slice(h * dk, (h + 1) * dk)
        state = state_ref[h]
        gnw = gnw_ref[:, cols]
        for ci in range(t_len // c):
            rows = slice(ci * c, (ci + 1) * c)
            q = q_ref[0, rows, cols]
            k = k_ref[0, rows, cols]
            v = v_ref[0, rows, cols]
            scores = lax.dot_general(q, k, (((1,), (1,)), ((), ())), preferred_element_type=F32)
            inner = _dot((scores * dmat_ref[h]).astype(BF16), v)
            cross = _dot(q, state.astype(BF16)) * xi_ref[h]
            y = inner + cross + yb_ref[0, rows, cols].astype(F32)
            kt = (k.astype(F32).T * zt_ref[h]).astype(BF16)
            state = state * gc_ref[h] + _dot(kt, v)
            mu = jnp.mean(y, axis=-1, keepdims=True)
            yc = y - mu
            var = jnp.mean(yc * yc, axis=-1, keepdims=True)
            yn = yc * lax.rsqrt(var + GN_EPS) * gnw
            g = g_ref[0, rows, cols].astype(F32)
            r_ref[0, rows, cols] = (g * jax.nn.sigmoid(g) * yn).astype(BF16)
        state_ref[h] = state


def _retention(q, k, v, g, decay_fwd, decay_bwd, gn_w, t_len):
    b, s, d = q.shape
    dk = d // RET_HEADS
    c = RET_CHUNK
    nb = s // t_len
    act = jax.ShapeDtypeStruct((b, s, d), BF16)
    smem = pl.BlockSpec(memory_space=pltpu.SMEM)
    rev = pl.BlockSpec((1, t_len, d), lambda i, t: (i, nb - 1 - t, 0))
    fwd = pl.BlockSpec((1, t_len, d), lambda i, t: (i, t, 0))
    tables = [pltpu.VMEM((RET_HEADS, c, dk), F32),
              pltpu.VMEM((RET_HEADS, dk, c), F32),
              pltpu.VMEM((RET_HEADS, 1, dk), F32)]
    state = pltpu.VMEM((RET_HEADS, dk, dk), F32)
    table_bytes = RET_HEADS * (3 * c * dk + dk * dk + dk) * 4
    yb = pl.pallas_call(
        _ret_bwd_kernel,
        out_shape=act,
        grid=(b, nb),
        in_specs=[smem, rev, rev, rev],
        out_specs=rev,
        scratch_shapes=[state] + tables,
        compiler_params=pltpu.CompilerParams(
            dimension_semantics=("arbitrary", "arbitrary"),
            vmem_limit_bytes=_vmem_limit(2 * 4 * t_len * d * 2 + table_bytes)),
        name="ret_bwd",
    )(decay_bwd, q, k, v)
    return pl.pallas_call(
        _ret_fwd_kernel,
        out_shape=act,
        grid=(b, nb),
        in_specs=[smem, smem, fwd, fwd, fwd, fwd, fwd, _resident((1, d))],
        out_specs=fwd,
        scratch_shapes=[state, pltpu.VMEM((RET_HEADS, c, c), F32)] + tables,
        compiler_params=pltpu.CompilerParams(
            dimension_semantics=("arbitrary", "arbitrary"),
            vmem_limit_bytes=_vmem_limit(2 * 6 * t_len * d * 2 + table_bytes)),
        name="ret_fwd",
    )(decay_fwd, decay_bwd, q, k, v, g, yb, gn_w)


def _fft_a_kernel(z_ref, tab_ref, o_ref):
    nb, n1, d = o_ref.shape[2], o_ref.shape[3], o_ref.shape[4]
    for j in range(nb):
        zz = z_ref[0, :, :, j * d:(j + 1) * d].reshape(2 * n1, d)
        out = _dot(tab_ref[j], zz)
        o_ref[0, 0, j] = out[:n1].astype(BF16)
        o_ref[0, 1, j] = out[n1:].astype(BF16)


def _fft_b_kernel(zt_ref, tab_ref, o_ref):
    n2, w = o_ref.shape[1], o_ref.shape[2]
    zz = zt_ref[0].reshape(2 * n2, w)
    o_ref[0] = _dot(tab_ref[...], zz).astype(BF16)


def _seq_dft_real(z, tab_a, tab_b, nb, kb):
    b, _, s, d = z.shape
    n2 = tab_b.shape[0]
    n1 = s // n2
    zt = pl.pallas_call(
        _fft_a_kernel,
        out_shape=jax.ShapeDtypeStruct((b, 2, n2, n1, d), BF16),
        grid=(b, n2 // nb),
        in_specs=[pl.BlockSpec((1, 2, n1, nb * d), lambda i, j: (i, 0, 0, j)),
                  pl.BlockSpec((nb, 2 * n1, 2 * n1), lambda i, j: (j, 0, 0))],
        out_specs=pl.BlockSpec((1, 2, nb, n1, d), lambda i, j: (i, 0, j, 0, 0)),
        compiler_params=pltpu.CompilerParams(
            dimension_semantics=("parallel", "parallel"),
            vmem_limit_bytes=_vmem_limit(2 * (2 * 2 * n1 * nb * d * 2 + nb * 4 * n1 * n1 * 2)
                                         + 2 * n1 * d * 4 * 2)),
        name="fft_a",
    )(z.reshape(b, 2, n1, n2 * d), tab_a)
    out = pl.pallas_call(
        _fft_b_kernel,
        out_shape=jax.ShapeDtypeStruct((b, n2, n1 * d), BF16),
        grid=(b, n1 // kb),
        in_specs=[pl.BlockSpec((1, 2, n2, kb * d), lambda i, j: (i, 0, 0, j)),
                  _resident((n2, 2 * n2))],
        out_specs=pl.BlockSpec((1, n2, kb * d), lambda i, j: (i, 0, j)),
        compiler_params=pltpu.CompilerParams(
            dimension_semantics=("parallel", "parallel"),
            vmem_limit_bytes=_vmem_limit(2 * 3 * n2 * kb * d * 2 + n2 * kb * d * 4 * 2)),
        name="fft_b",
    )(zt.reshape(b, 2, n2, n1 * d), tab_b)
    return out.reshape(b, s, d)


def _tail_kernel(x_ref, r_ref, f_ref, gr_ref, gf_ref, ck_ref, cv_ref,
                 w_ro_ref, w_fo_ref, w_mo_ref, n_ca_ref, w_cq_ref, w_co_ref,
                 n_mlp_ref, w_up_ref, w_dn_ref, n_fin_ref, o_ref, att_ref):
    d = x_ref.shape[-1]
    dh = d // CA_HEADS
    ff = w_up_ref.shape[1]
    x = x_ref[0]
    ret = _dot(r_ref[0], w_ro_ref[...])
    fou = _dot(f_ref[0], w_fo_ref[...])
    merged = (jax.nn.sigmoid(gr_ref[0].astype(F32)) * ret
              + jax.nn.sigmoid(gf_ref[0].astype(F32)) * fou)
    x = x + _dot(merged.astype(BF16), w_mo_ref[...])

    hq = _dot(_rms(x, n_ca_ref[...]).astype(BF16), w_cq_ref[...]).astype(BF16)
    for h in range(CA_HEADS):
        cols = slice(h * dh, (h + 1) * dh)
        logits = lax.dot_general(hq[:, cols], ck_ref[0, :, cols], (((1,), (1,)), ((), ())),
                                 preferred_element_type=F32) * (dh ** -0.5)
        p = jnp.exp(logits - jnp.max(logits, axis=-1, keepdims=True))
        probs = p / jnp.sum(p, axis=-1, keepdims=True)
        att_ref[:, cols] = _dot(probs.astype(BF16), cv_ref[0, :, cols]).astype(BF16)
    x = x + _dot(att_ref[...], w_co_ref[...])

    xb = _rms(x, n_mlp_ref[...]).astype(BF16)
    acc = x
    for j in range(ff // d):
        cols = slice(j * d, (j + 1) * d)
        hid = jnp.square(jnp.maximum(_dot(xb, w_up_ref[:, cols]), 0.0)).astype(BF16)
        acc = acc + _dot(hid, w_dn_ref[cols, :])
    o_ref[0] = _rms(acc, n_fin_ref[...])


def _tail(x, r, four, gate_r, gate_f, ck, cv, w_ro, w_fo, w_mo, n_ca, w_cq, w_co,
          n_mlp, w_up, w_dn, n_fin, tm):
    b, s, d = x.shape
    m = ck.shape[1]
    ff = w_up.shape[1]
    row = pl.BlockSpec((1, tm, d), lambda i, j: (i, j, 0))
    mem = pl.BlockSpec((1, m, d), lambda i, j: (i, 0, 0))
    vec = _resident((1, d))
    sq = _resident((d, d))
    vmem = ((5 * d * d + 2 * d * ff) * 2 + 4 * m * d * 2 + 2 * tm * d * (4 + 4 + 4 * 2)
            + tm * d * 4 * 8)
    return pl.pallas_call(
        _tail_kernel,
        out_shape=jax.ShapeDtypeStruct((b, s, d), F32),
        grid=(b, s // tm),
        in_specs=[row, row, row, row, row, mem, mem, sq, sq, sq, vec, sq, sq, vec,
                  _resident((d, ff)), _resident((ff, d)), vec],
        out_specs=row,
        scratch_shapes=[pltpu.VMEM((tm, d), BF16)],
        compiler_params=pltpu.CompilerParams(
            dimension_semantics=("parallel", "parallel"),
            vmem_limit_bytes=_vmem_limit(vmem)),
        name="tail",
    )(x, r, four, gate_r, gate_f, ck, cv, w_ro, w_fo, w_mo, n_ca, w_cq, w_co,
      n_mlp, w_up, w_dn, n_fin)


def _tiles(s):
    tm_in = min(512, s)
    t_ret = min(512, s)
    tm_tail = min(256, s)
    n1 = s // FFT_N2
    nb = min(4, FFT_N2)
    kb = min(4, n1)
    return tm_in, t_ret, tm_tail, nb, kb


def _trunk(x, mem, p):
    b, s, d = x.shape
    dk = d // RET_HEADS
    gd = d // FOUR_GROUPS
    n1 = s // FFT_N2
    assert s % FFT_N2 == 0 and s % RET_CHUNK == 0
    tm_in, t_ret, tm_tail, nb, kb = _tiles(s)
    cos, sin = (jnp.asarray(t) for t in _rope_tables(s, dk))
    dft = jnp.asarray(_chan_dft(gd)).astype(BF16)
    tab_a = jnp.asarray(_fft_a_table(n1, FFT_N2)).astype(BF16)
    tab_b = jnp.asarray(_fft_b_table(FFT_N2, float(s * gd) ** -0.5)).astype(BF16)

    for l in range(p["w_in"].shape[0]):
        ck, cv = _mem_kv(mem, p["norm_mem_w"][l], p["w_ck"][l], p["w_cv"][l])
        q, k, v, g, z, gate_r, gate_f = _in_proj(x, p["norm_mix_w"][l], p["w_in"][l],
                                                 cos, sin, dft, tm_in)
        r = _retention(q, k, v, g, p["ret_decay_fwd"][l], p["ret_decay_bwd"][l],
                       p["ret_gn_w"][l], t_ret)
        four = _seq_dft_real(z, tab_a, tab_b, nb, kb)
        last = l == p["w_in"].shape[0] - 1
        n_fin = p["norm_final_w"] if last else None
        assert last, "a single layer is fused with the final norm"
        x = _tail(x, r, four, gate_r, gate_f, ck, cv, p["w_ret_out"][l], p["w_four_out"][l],
                  p["w_mix_out"][l], p["norm_ca_w"][l], p["w_cq"][l], p["w_co"][l],
                  p["norm_mlp_w"][l], p["w_up"][l], p["w_down"][l], n_fin, tm_tail)
    return x


def kernel(x_prompt, x_sample, mem_prompt, mem_sample, norm_mix_w, w_in, ret_decay_fwd,
           ret_decay_bwd, ret_gn_w, w_ret_out, w_four_out, w_mix_out, norm_ca_w, norm_mem_w,
           w_cq, w_ck, w_cv, w_co, norm_mlp_w, w_up, w_down, norm_final_w):
    depth, d = norm_mix_w.shape
    vec = lambda w: w.reshape(depth, 1, d)
    mat = lambda w: w.astype(BF16)
    p = dict(
        norm_mix_w=vec(norm_mix_w), w_in=mat(w_in), ret_decay_fwd=ret_decay_fwd,
        ret_decay_bwd=ret_decay_bwd, ret_gn_w=vec(ret_gn_w), w_ret_out=mat(w_ret_out),
        w_four_out=mat(w_four_out), w_mix_out=mat(w_mix_out), norm_ca_w=vec(norm_ca_w),
        norm_mem_w=vec(norm_mem_w), w_cq=mat(w_cq), w_ck=mat(w_ck), w_cv=mat(w_cv),
        w_co=mat(w_co), norm_mlp_w=vec(norm_mlp_w), w_up=mat(w_up), w_down=mat(w_down),
        norm_final_w=norm_final_w.reshape(1, d))
    return _trunk(x_prompt, mem_prompt, p), _trunk(x_sample, mem_sample, p)
```

```python
import functools

import numpy as np
import jax
import jax.numpy as jnp
from jax import lax
from jax.experimental import pallas as pl
from jax.experimental.pallas import tpu as pltpu

RET_HEADS = 4
FOUR_GROUPS = 4
CA_HEADS = 4
ROPE_THETA = 10000.0
EPS = 1e-6
GN_EPS = 1e-5

V7X_VMEM_BYTES = 64 * 1024 * 1024
V7X_MXU_DIM = 256
V7X_LANES = 128

RET_CHUNK = V7X_MXU_DIM
FFT_N2 = 128

BF16 = jnp.bfloat16
F32 = jnp.float32


def _vmem_limit(nbytes):
    return int(min(nbytes + 8 * 1024 * 1024, V7X_VMEM_BYTES - 6 * 1024 * 1024))


def _rms(x, w):
    return x * lax.rsqrt(jnp.mean(x * x, axis=-1, keepdims=True) + EPS) * w


def _dot(a, b):
    return jnp.dot(a, b, preferred_element_type=F32)


def _words(x):
    return pltpu.bitcast(x.astype(BF16), jnp.int32)


def _rows(w):
    return pltpu.bitcast(w, BF16)


def _resident(shape):
    return pl.BlockSpec(shape, lambda *_: (0,) * len(shape), pipeline_mode=pl.Buffered(1))


@functools.lru_cache(maxsize=None)
def _rope_tables(s, dk):
    inv = ROPE_THETA ** (-np.arange(0, dk, 2, dtype=np.float64) / dk)
    ang = np.arange(s, dtype=np.float64)[:, None] * inv[None, :]
    return np.cos(ang).astype(np.float32), np.sin(ang).astype(np.float32)


@functools.lru_cache(maxsize=None)
def _chan_dft(n):
    ang = 2.0 * np.pi * np.outer(np.arange(n), np.arange(n)) / n
    return np.concatenate([np.cos(ang), -np.sin(ang)], axis=1).astype(np.float32)


@functools.lru_cache(maxsize=None)
def _fft_a_table(n1, n2):
    n = n1 * n2
    k1 = np.arange(n1, dtype=np.float64)[None, :, None]
    tok = (n2 * np.arange(n1, dtype=np.float64)[None, None, :]
           + np.arange(n2, dtype=np.float64)[:, None, None])
    ang = 2.0 * np.pi * np.mod(k1 * tok, n) / n
    c, s = np.cos(ang), np.sin(ang)
    re_rows = np.concatenate([c, s], axis=2)
    im_rows = np.concatenate([-s, c], axis=2)
    return np.stack([re_rows, im_rows], axis=2).reshape(n2, 2 * n1, 2 * n1).astype(np.float32)


@functools.lru_cache(maxsize=None)
def _fft_b_table(n2, scale):
    ang = 2.0 * np.pi * np.outer(np.arange(n2), np.arange(n2)) / n2
    return (scale * np.stack([np.cos(ang), np.sin(ang)], axis=-1).reshape(n2, 2 * n2)).astype(np.float32)


def _mem_kv_kernel(mem_ref, nw_ref, wk_ref, wv_ref, ck_ref, cv_ref):
    mn = _rms(mem_ref[0], nw_ref[...]).astype(BF16)
    ck_ref[0] = _dot(mn, wk_ref[...]).astype(BF16)
    cv_ref[0] = _dot(mn, wv_ref[...]).astype(BF16)


def _mem_kv(mem, norm_w, w_ck, w_cv):
    b, m, d = mem.shape
    out = jax.ShapeDtypeStruct((b, m, d), BF16)
    blk = pl.BlockSpec((1, m, d), lambda i: (i, 0, 0))
    return pl.pallas_call(
        _mem_kv_kernel,
        out_shape=(out, out),
        grid=(b,),
        in_specs=[blk, _resident((1, d)), _resident((d, d)), _resident((d, d))],
        out_specs=(blk, blk),
        compiler_params=pltpu.CompilerParams(
            dimension_semantics=("arbitrary",),
            vmem_limit_bytes=_vmem_limit(2 * d * d * 2 + 4 * m * d * 4)),
        name="mem_kv",
    )(mem, norm_w, w_ck, w_cv)


def _in_proj_kernel(x_ref, nw_ref, w_ref, cos_ref, sin_ref, dft_ref,
                    q_ref, k_ref, v_ref, g_ref, z_ref, gr_ref, gf_ref, perm_ref):
    d = x_ref.shape[-1]
    dk = d // RET_HEADS
    half = dk // 2
    gd = d // FOUR_GROUPS
    xb = _rms(x_ref[0], nw_ref[...]).astype(BF16)
    cos = cos_ref[...]
    sin = sin_ref[...]

    def proj(j):
        return _dot(xb, w_ref[:, j * d:(j + 1) * d])

    def rope(p, out_ref, scale):
        for h in range(RET_HEADS):
            x1 = p[:, h * dk:h * dk + half]
            x2 = p[:, h * dk + half:(h + 1) * dk]
            o1 = x1 * cos - x2 * sin
            o2 = x2 * cos + x1 * sin
            if scale is not None:
                o1 = o1 * scale
                o2 = o2 * scale
            out_ref[0, :, h * dk:h * dk + half] = o1.astype(BF16)
            out_ref[0, :, h * dk + half:(h + 1) * dk] = o2.astype(BF16)

    rope(proj(0), q_ref, dk ** -0.5)
    rope(proj(1), k_ref, None)
    v_ref[0] = proj(2).astype(BF16)
    g_ref[0] = proj(3).astype(BF16)
    u = proj(4)
    for lc in range(d // V7X_LANES):
        for blk in range(u.shape[0] // FFT_N2):
            a, par = divmod(blk, 2)
            perm_ref[lc, pl.ds(2 * FFT_N2 * a + par, FFT_N2, stride=2), :] = (
                u[blk * FFT_N2:(blk + 1) * FFT_N2, lc * V7X_LANES:(lc + 1) * V7X_LANES])
    for gi in range(FOUR_GROUPS):
        ug = jnp.concatenate([perm_ref[lc] for lc in range(gi * gd // V7X_LANES,
                                                           (gi + 1) * gd // V7X_LANES)], axis=1)
        ab = _dot(ug.astype(BF16), dft_ref[...])
        z_ref[0, 0, :, gi * gd:(gi + 1) * gd] = _words(ab[:, :gd])
        z_ref[0, 1, :, gi * gd:(gi + 1) * gd] = _words(ab[:, gd:])
    gr_ref[0] = proj(5).astype(BF16)
    gf_ref[0] = proj(6).astype(BF16)


def _in_proj(x, norm_w, w_in, cos, sin, dft, tm):
    b, s, d = x.shape
    half = d // RET_HEADS // 2
    gd = d // FOUR_GROUPS
    act = jax.ShapeDtypeStruct((b, s, d), BF16)
    assert tm % (2 * FFT_N2) == 0
    zsh = jax.ShapeDtypeStruct((b, 2, s // 2, d), jnp.int32)
    row = pl.BlockSpec((1, tm, d), lambda i, j: (i, j, 0))
    zrow = pl.BlockSpec((1, 2, tm // 2, d), lambda i, j: (i, 0, j, 0))
    tab = pl.BlockSpec((tm, half), lambda i, j: (j, 0))
    vmem = (w_in.size * 2 + 2 * tm * d * 4 + 2 * 8 * tm * d * 2 + 4 * tm * half * 4
            + dft.size * 2 + 5 * tm * d * 4)
    return pl.pallas_call(
        _in_proj_kernel,
        out_shape=(act, act, act, act, zsh, act, act),
        grid=(b, s // tm),
        in_specs=[row, _resident((1, d)), _resident(w_in.shape), tab, tab,
                  _resident((gd, 2 * gd))],
        out_specs=(row, row, row, row, zrow, row, row),
        scratch_shapes=[pltpu.VMEM((d // V7X_LANES, tm, V7X_LANES), F32)],
        compiler_params=pltpu.CompilerParams(
            dimension_semantics=("parallel", "parallel"),
            vmem_limit_bytes=_vmem_limit(vmem)),
        name="in_proj",
    )(x, norm_w, w_in, cos, sin, dft)


def _log_sigmoid(x):
    return jnp.minimum(x, 0.0) - jnp.log1p(jnp.exp(-jnp.abs(x)))


def _ret_bwd_kernel(dec_ref, q_ref, k_ref, v_ref, yb_ref, state_ref, xi_ref, zt_ref, gc_ref):
    c = RET_CHUNK
    t_len, d = q_ref.shape[1], q_ref.shape[2]
    dk = d // RET_HEADS

    @pl.when(pl.program_id(1) == 0)
    def _init():
        state_ref[...] = jnp.zeros_like(state_ref)
        for h in range(RET_HEADS):
            lg = _log_sigmoid(jnp.full((c, dk), dec_ref[h], F32))
            row = lax.broadcasted_iota(jnp.int32, (c, dk), 0).astype(F32)
            col = lax.broadcasted_iota(jnp.int32, (dk, c), 1).astype(F32)
            xi_ref[h] = jnp.exp(lg * (c - row))
            zt_ref[h] = jnp.exp(lg * col)
            gc_ref[h] = jnp.exp(_log_sigmoid(jnp.full((1, dk), dec_ref[h], F32)) * c)

    for h in range(RET_HEADS):
        cols = slice(h * dk, (h + 1) * dk)
        state = state_ref[h]
        for ci in reversed(range(t_len // c)):
            rows = slice(ci * c, (ci + 1) * c)
            q = q_ref[0, rows, cols]
            k = k_ref[0, rows, cols]
            v = v_ref[0, rows, cols]
            cross = _dot(q, state.astype(BF16)) * xi_ref[h]
            yb_ref[0, rows, cols] = cross.astype(BF16)
            kt = (k.astype(F32).T * zt_ref[h]).astype(BF16)
            state = state * gc_ref[h] + _dot(kt, v)
        state_ref[h] = state


def _ret_fwd_kernel(decf_ref, decb_ref, q_ref, k_ref, v_ref, g_ref, yb_ref, gnw_ref, r_ref,
                    state_ref, dmat_ref, xi_ref, zt_ref, gc_ref):
    c = RET_CHUNK
    t_len, d = q_ref.shape[1], q_ref.shape[2]
    dk = d // RET_HEADS

    @pl.when(pl.program_id(1) == 0)
    def _init():
        state_ref[...] = jnp.zeros_like(state_ref)
        for h in range(RET_HEADS):
            lgf = _log_sigmoid(jnp.full((c, dk), decf_ref[h], F32))
            lgb = _log_sigmoid(jnp.full((c, c), decb_ref[h], F32))
            row = lax.broadcasted_iota(jnp.int32, (c, dk), 0).astype(F32)
            col = lax.broadcasted_iota(jnp.int32, (dk, c), 1).astype(F32)
            xi_ref[h] = jnp.exp(lgf * (row + 1.0))
            zt_ref[h] = jnp.exp(lgf * (c - 1.0 - col))
            gc_ref[h] = jnp.exp(_log_sigmoid(jnp.full((1, dk), decf_ref[h], F32)) * c)
            i = lax.broadcasted_iota(jnp.int32, (c, c), 0)
            j = lax.broadcasted_iota(jnp.int32, (c, c), 1)
            diff = (i - j).astype(F32)
            lgf_cc = _log_sigmoid(jnp.full((c, c), decf_ref[h], F32))
            dmat_ref[h] = jnp.where(i >= j, jnp.exp(lgf_cc * jnp.maximum(diff, 0.0)),
                                    jnp.exp(lgb * jnp.maximum(-diff, 0.0)))

    for h in range(RET_HEADS):
        cols = slice(h * dk, (h + 1) * dk)
        state = state_ref[h]
        gnw = gnw_ref[:, cols]
        for ci in range(t_len // c):
            rows = slice(ci * c, (ci + 1) * c)
            q = q_ref[0, rows, cols]
            k = k_ref[0, rows, cols]
            v = v_ref[0, rows, cols]
            scores = lax.dot_general(q, k, (((1,), (1,)), ((), ())), preferred_element_type=F32)
            inner = _dot((scores * dmat_ref[h]).astype(BF16), v)
            cross = _dot(q, state.astype(BF16)) * xi_ref[h]
            y = inner + cross + yb_ref[0, rows, cols].astype(F32)
            kt = (k.astype(F32).T * zt_ref[h]).astype(BF16)
            state = state * gc_ref[h] + _dot(kt, v)
            mu = jnp.mean(y, axis=-1, keepdims=True)
            yc = y - mu
            var = jnp.mean(yc * yc, axis=-1, keepdims=True)
            yn = yc * lax.rsqrt(var + GN_EPS) * gnw
            g = g_ref[0, rows, cols].astype(F32)
            r_ref[0, rows, cols] = (g * jax.nn.sigmoid(g) * yn).astype(BF16)
        state_ref[h] = state


def _retention(q, k, v, g, decay_fwd, decay_bwd, gn_w, t_len):
    b, s, d = q.shape
    dk = d // RET_HEADS
    c = RET_CHUNK
    nb = s // t_len
    act = jax.ShapeDtypeStruct((b, s, d), BF16)
    smem = pl.BlockSpec(memory_space=pltpu.SMEM)
    rev = pl.BlockSpec((1, t_len, d), lambda i, t: (i, nb - 1 - t, 0))
    fwd = pl.BlockSpec((1, t_len, d), lambda i, t: (i, t, 0))
    tables = [pltpu.VMEM((RET_HEADS, c, dk), F32),
              pltpu.VMEM((RET_HEADS, dk, c), F32),
              pltpu.VMEM((RET_HEADS, 1, dk), F32)]
    state = pltpu.VMEM((RET_HEADS, dk, dk), F32)
    table_bytes = RET_HEADS * (3 * c * dk + dk * dk + dk) * 4
    yb = pl.pallas_call(
        _ret_bwd_kernel,
        out_shape=act,
        grid=(b, nb),
        in_specs=[smem, rev, rev, rev],
        out_specs=rev,
        scratch_shapes=[state] + tables,
        compiler_params=pltpu.CompilerParams(
            dimension_semantics=("arbitrary", "arbitrary"),
            vmem_limit_bytes=_vmem_limit(2 * 4 * t_len * d * 2 + table_bytes)),
        name="ret_bwd",
    )(decay_bwd, q, k, v)
    return pl.pallas_call(
        _ret_fwd_kernel,
        out_shape=act,
        grid=(b, nb),
        in_specs=[smem, smem, fwd, fwd, fwd, fwd, fwd, _resident((1, d))],
        out_specs=fwd,
        scratch_shapes=[state, pltpu.VMEM((RET_HEADS, c, c), F32)] + tables,
        compiler_params=pltpu.CompilerParams(
            dimension_semantics=("arbitrary", "arbitrary"),
            vmem_limit_bytes=_vmem_limit(2 * 6 * t_len * d * 2 + table_bytes)),
        name="ret_fwd",
    )(decay_fwd, decay_bwd, q, k, v, g, yb, gn_w)


def _fft_a_kernel(z_ref, tab_ref, o_ref):
    for r in range(z_ref.shape[3]):
        zz = jnp.concatenate([_rows(z_ref[0, 0, :, r, :]), _rows(z_ref[0, 1, :, r, :])], axis=0)
        o_ref[0, :, r, :] = _words(_dot(tab_ref[r], zz))


def _fft_b_kernel(zt_ref, tab_ref, o_ref):
    for i in range(zt_ref.shape[1]):
        o_ref[0, :, i, :] = _words(_dot(tab_ref[...], _rows(zt_ref[0, i])))


def _seq_dft_real(z, tab_a, tab_b, rn, kb):
    b, _, s2, d = z.shape
    n2 = tab_b.shape[0]
    n1 = 2 * s2 // n2
    zt = pl.pallas_call(
        _fft_a_kernel,
        out_shape=jax.ShapeDtypeStruct((b, n1, n2, d), jnp.int32),
        grid=(b, n2 // rn),
        in_specs=[pl.BlockSpec((1, 2, n1 // 2, rn, d), lambda i, j: (i, 0, 0, j, 0)),
                  pl.BlockSpec((rn, 2 * n1, 2 * n1), lambda i, j: (j, 0, 0))],
        out_specs=pl.BlockSpec((1, n1, rn, d), lambda i, j: (i, 0, j, 0)),
        compiler_params=pltpu.CompilerParams(
            dimension_semantics=("parallel", "parallel"),
            vmem_limit_bytes=_vmem_limit(2 * (2 * n1 * rn * d * 4 + rn * 4 * n1 * n1 * 2)
                                         + 6 * n1 * d * 4)),
        name="fft_a",
    )(z.reshape(b, 2, n1 // 2, n2, d), tab_a)
    return pl.pallas_call(
        _fft_b_kernel,
        out_shape=jax.ShapeDtypeStruct((b, n2 // 2, n1, d), jnp.int32),
        grid=(b, n1 // kb),
        in_specs=[pl.BlockSpec((1, kb, n2, d), lambda i, j: (i, j, 0, 0)),
                  _resident((n2, 2 * n2))],
        out_specs=pl.BlockSpec((1, n2 // 2, kb, d), lambda i, j: (i, 0, j, 0)),
        compiler_params=pltpu.CompilerParams(
            dimension_semantics=("parallel", "parallel"),
            vmem_limit_bytes=_vmem_limit(2 * (kb * n2 * d * 4 + (n2 // 2) * kb * d * 4)
                                         + 4 * n2 * d * 4)),
        name="fft_b",
    )(zt, tab_b)


def _tail_kernel(x_ref, r_ref, f_ref, gr_ref, gf_ref, ck_ref, cv_ref,
                 w_ro_ref, w_fo_ref, w_mo_ref, n_ca_ref, w_cq_ref, w_co_ref,
                 n_mlp_ref, w_up_ref, w_dn_ref, n_fin_ref, o_ref, att_ref, perm_ref):
    d = x_ref.shape[-1]
    dh = d // CA_HEADS
    ff = w_up_ref.shape[1]
    x = x_ref[0]
    ret = _dot(r_ref[0], w_ro_ref[...])
    gq, n1 = f_ref.shape[1], f_ref.shape[2]
    fperm = _dot(_rows(f_ref[0].reshape(gq * n1, d)), w_fo_ref[...])
    for lc in range(d // V7X_LANES):
        perm_ref[lc] = fperm[:, lc * V7X_LANES:(lc + 1) * V7X_LANES]
    fou = jnp.concatenate(
        [jnp.concatenate([perm_ref[lc, pl.ds(blk // 2 * 2 * n1 + blk % 2, n1, stride=2), :]
                          for lc in range(d // V7X_LANES)], axis=1)
         for blk in range(2 * gq)], axis=0)
    merged = (jax.nn.sigmoid(gr_ref[0].astype(F32)) * ret
              + jax.nn.sigmoid(gf_ref[0].astype(F32)) * fou)
    x = x + _dot(merged.astype(BF16), w_mo_ref[...])

    hq = _dot(_rms(x, n_ca_ref[...]).astype(BF16), w_cq_ref[...]).astype(BF16)
    for h in range(CA_HEADS):
        cols = slice(h * dh, (h + 1) * dh)
        logits = lax.dot_general(hq[:, cols], ck_ref[0, :, cols], (((1,), (1,)), ((), ())),
                                 preferred_element_type=F32) * (dh ** -0.5)
        p = jnp.exp(logits - jnp.max(logits, axis=-1, keepdims=True))
        probs = p / jnp.sum(p, axis=-1, keepdims=True)
        att_ref[:, cols] = _dot(probs.astype(BF16), cv_ref[0, :, cols]).astype(BF16)
    x = x + _dot(att_ref[...], w_co_ref[...])

    xb = _rms(x, n_mlp_ref[...]).astype(BF16)
    acc = x
    for j in range(ff // d):
        cols = slice(j * d, (j + 1) * d)
        hid = jnp.square(jnp.maximum(_dot(xb, w_up_ref[:, cols]), 0.0)).astype(BF16)
        acc = acc + _dot(hid, w_dn_ref[cols, :])
    o_ref[0] = _rms(acc, n_fin_ref[...])


def _tail(x, r, four, gate_r, gate_f, ck, cv, w_ro, w_fo, w_mo, n_ca, w_cq, w_co,
          n_mlp, w_up, w_dn, n_fin, tm):
    b, s, d = x.shape
    m = ck.shape[1]
    ff = w_up.shape[1]
    row = pl.BlockSpec((1, tm, d), lambda i, j: (i, j, 0))
    n1 = four.shape[2]
    assert tm % (2 * n1) == 0
    half = pl.BlockSpec((1, tm // (2 * n1), n1, d), lambda i, j: (i, j, 0, 0))
    mem = pl.BlockSpec((1, m, d), lambda i, j: (i, 0, 0))
    vec = _resident((1, d))
    sq = _resident((d, d))
    vmem = ((5 * d * d + 2 * d * ff) * 2 + 4 * m * d * 2 + 2 * tm * d * (4 + 4 + 4 * 2)
            + tm * d * 4 * 8)
    return pl.pallas_call(
        _tail_kernel,
        out_shape=jax.ShapeDtypeStruct((b, s, d), F32),
        grid=(b, s // tm),
        in_specs=[row, row, half, row, row, mem, mem, sq, sq, sq, vec, sq, sq, vec,
                  _resident((d, ff)), _resident((ff, d)), vec],
        out_specs=row,
        scratch_shapes=[pltpu.VMEM((tm, d), BF16),
                        pltpu.VMEM((d // V7X_LANES, tm, V7X_LANES), F32)],
        compiler_params=pltpu.CompilerParams(
            dimension_semantics=("parallel", "parallel"),
            vmem_limit_bytes=_vmem_limit(vmem)),
        name="tail",
    )(x, r, four, gate_r, gate_f, ck, cv, w_ro, w_fo, w_mo, n_ca, w_cq, w_co,
      n_mlp, w_up, w_dn, n_fin)


def _tiles(s):
    tm_in = min(512, s)
    t_ret = min(512, s)
    tm_tail = min(256, s)
    n1 = s // FFT_N2
    rn = 8
    kb = min(8, n1)
    return tm_in, t_ret, tm_tail, rn, kb


def _trunk(x, mem, p):
    b, s, d = x.shape
    dk = d // RET_HEADS
    gd = d // FOUR_GROUPS
    n1 = s // FFT_N2
    assert s % FFT_N2 == 0 and s % RET_CHUNK == 0
    tm_in, t_ret, tm_tail, rn, kb = _tiles(s)
    cos, sin = (jnp.asarray(t) for t in _rope_tables(s, dk))
    dft = jnp.asarray(_chan_dft(gd)).astype(BF16)
    tab_a = jnp.asarray(_fft_a_table(n1, FFT_N2)).astype(BF16)
    tab_b = jnp.asarray(_fft_b_table(FFT_N2, float(s * gd) ** -0.5)).astype(BF16)

    for l in range(p["w_in"].shape[0]):
        ck, cv = _mem_kv(mem, p["norm_mem_w"][l], p["w_ck"][l], p["w_cv"][l])
        q, k, v, g, z, gate_r, gate_f = _in_proj(x, p["norm_mix_w"][l], p["w_in"][l],
                                                 cos, sin, dft, tm_in)
        r = _retention(q, k, v, g, p["ret_decay_fwd"][l], p["ret_decay_bwd"][l],
                       p["ret_gn_w"][l], t_ret)
        four = _seq_dft_real(z, tab_a, tab_b, rn, kb)
        last = l == p["w_in"].shape[0] - 1
        n_fin = p["norm_final_w"] if last else None
        assert last, "a single layer is fused with the final norm"
        x = _tail(x, r, four, gate_r, gate_f, ck, cv, p["w_ret_out"][l], p["w_four_out"][l],
                  p["w_mix_out"][l], p["norm_ca_w"][l], p["w_cq"][l], p["w_co"][l],
                  p["norm_mlp_w"][l], p["w_up"][l], p["w_down"][l], n_fin, tm_tail)
    return x


def kernel(x_prompt, x_sample, mem_prompt, mem_sample, norm_mix_w, w_in, ret_decay_fwd,
           ret_decay_bwd, ret_gn_w, w_ret_out, w_four_out, w_mix_out, norm_ca_w, norm_mem_w,
           w_cq, w_ck, w_cv, w_co, norm_mlp_w, w_up, w_down, norm_final_w):
    depth, d = norm_mix_w.shape
    vec = lambda w: w.reshape(depth, 1, d)
    mat = lambda w: w.astype(BF16)
    p = dict(
        norm_mix_w=vec(norm_mix_w), w_in=mat(w_in), ret_decay_fwd=ret_decay_fwd,
        ret_decay_bwd=ret_decay_bwd, ret_gn_w=vec(ret_gn_w), w_ret_out=mat(w_ret_out),
        w_four_out=mat(w_four_out), w_mix_out=mat(w_mix_out), norm_ca_w=vec(norm_ca_w),
        norm_mem_w=vec(norm_mem_w), w_cq=mat(w_cq), w_ck=mat(w_ck), w_cv=mat(w_cv),
        w_co=mat(w_co), norm_mlp_w=vec(norm_mlp_w), w_up=mat(w_up), w_down=mat(w_down),
        norm_final_w=norm_final_w.reshape(1, d))
    return _trunk(x_prompt, mem_prompt, p), _trunk(x_sample, mem_sample, p)
```

```python
import functools

import numpy as np
import jax
import jax.numpy as jnp
from jax import lax
from jax.experimental import pallas as pl
from jax.experimental.pallas import tpu as pltpu

RET_HEADS = 4
FOUR_GROUPS = 4
CA_HEADS = 4
ROPE_THETA = 10000.0
EPS = 1e-6
GN_EPS = 1e-5

V7X_VMEM_BYTES = 64 * 1024 * 1024
V7X_MXU_DIM = 256
V7X_LANES = 128
SUBLANES = 8

RET_CHUNK = V7X_MXU_DIM
FFT_N2 = 128

BF16 = jnp.bfloat16
F32 = jnp.float32


def _vmem_limit(nbytes):
    return int(min(nbytes + 8 * 1024 * 1024, V7X_VMEM_BYTES - 6 * 1024 * 1024))


def _rms(x, w):
    return x * lax.rsqrt(jnp.mean(x * x, axis=-1, keepdims=True) + EPS) * w


def _dot(a, b):
    return jnp.dot(a, b, preferred_element_type=F32)


def _words(x):
    return pltpu.bitcast(x.astype(BF16), jnp.int32)


def _rows(w):
    return pltpu.bitcast(w, BF16)


def _resident(shape):
    return pl.BlockSpec(shape, lambda *_: (0,) * len(shape), pipeline_mode=pl.Buffered(1))


@functools.lru_cache(maxsize=None)
def _rope_tables(s, dk):
    inv = ROPE_THETA ** (-np.arange(0, dk, 2, dtype=np.float64) / dk)
    ang = np.arange(s, dtype=np.float64)[:, None] * inv[None, :]
    return np.cos(ang).astype(np.float32), np.sin(ang).astype(np.float32)


@functools.lru_cache(maxsize=None)
def _chan_dft(n):
    ang = 2.0 * np.pi * np.outer(np.arange(n), np.arange(n)) / n
    return np.concatenate([np.cos(ang), -np.sin(ang)], axis=1).astype(np.float32)


@functools.lru_cache(maxsize=None)
def _fft_a_table(n1, n2):
    n = n1 * n2
    k1 = np.arange(n1, dtype=np.float64)[None, :, None]
    tok = (n2 * np.arange(n1, dtype=np.float64)[None, None, :]
           + np.arange(n2, dtype=np.float64)[:, None, None])
    ang = 2.0 * np.pi * np.mod(k1 * tok, n) / n
    c, s = np.cos(ang), np.sin(ang)
    re_rows = np.concatenate([c, s], axis=2)
    im_rows = np.concatenate([-s, c], axis=2)
    return np.stack([re_rows, im_rows], axis=2).reshape(n2, 2 * n1, 2 * n1).astype(np.float32)


@functools.lru_cache(maxsize=None)
def _fft_b_table(n2, scale):
    ang = 2.0 * np.pi * np.outer(np.arange(n2), np.arange(n2)) / n2
    return (scale * np.stack([np.cos(ang), np.sin(ang)], axis=-1).reshape(n2, 2 * n2)).astype(np.float32)


def _mem_kv_kernel(mem_ref, nw_ref, wk_ref, wv_ref, ck_ref, cv_ref):
    mn = _rms(mem_ref[0], nw_ref[...]).astype(BF16)
    ck_ref[0] = _dot(mn, wk_ref[...]).astype(BF16)
    cv_ref[0] = _dot(mn, wv_ref[...]).astype(BF16)


def _mem_kv(mem, norm_w, w_ck, w_cv):
    b, m, d = mem.shape
    out = jax.ShapeDtypeStruct((b, m, d), BF16)
    blk = pl.BlockSpec((1, m, d), lambda i: (i, 0, 0))
    return pl.pallas_call(
        _mem_kv_kernel,
        out_shape=(out, out),
        grid=(b,),
        in_specs=[blk, _resident((1, d)), _resident((d, d)), _resident((d, d))],
        out_specs=(blk, blk),
        compiler_params=pltpu.CompilerParams(
            dimension_semantics=("arbitrary",),
            vmem_limit_bytes=_vmem_limit(2 * d * d * 2 + 4 * m * d * 4)),
        name="mem_kv",
    )(mem, norm_w, w_ck, w_cv)


def _in_proj_kernel(x_ref, nw_ref, w_ref, cos_ref, sin_ref, dft_ref,
                    q_ref, k_ref, v_ref, g_ref, z_ref, gr_ref, gf_ref, perm_ref):
    d = x_ref.shape[-1]
    dk = d // RET_HEADS
    half = dk // 2
    gd = d // FOUR_GROUPS
    xb = _rms(x_ref[0], nw_ref[...]).astype(BF16)
    cos = cos_ref[...]
    sin = sin_ref[...]

    def proj(j):
        return _dot(xb, w_ref[:, j * d:(j + 1) * d])

    def rope(p, out_ref, scale):
        for h in range(RET_HEADS):
            x1 = p[:, h * dk:h * dk + half]
            x2 = p[:, h * dk + half:(h + 1) * dk]
            o1 = x1 * cos - x2 * sin
            o2 = x2 * cos + x1 * sin
            if scale is not None:
                o1 = o1 * scale
                o2 = o2 * scale
            out_ref[0, :, h * dk:h * dk + half] = o1.astype(BF16)
            out_ref[0, :, h * dk + half:(h + 1) * dk] = o2.astype(BF16)

    rope(proj(0), q_ref, dk ** -0.5)
    rope(proj(1), k_ref, None)
    v_ref[0] = proj(2).astype(BF16)
    g_ref[0] = proj(3).astype(BF16)
    u = proj(4)
    for lc in range(d // V7X_LANES):
        for blk in range(u.shape[0] // FFT_N2):
            a, par = divmod(blk, 2)
            perm_ref[lc, pl.ds(2 * FFT_N2 * a + par, FFT_N2, stride=2), :] = (
                u[blk * FFT_N2:(blk + 1) * FFT_N2, lc * V7X_LANES:(lc + 1) * V7X_LANES])
    for gi in range(FOUR_GROUPS):
        ug = jnp.concatenate([perm_ref[lc] for lc in range(gi * gd // V7X_LANES,
                                                           (gi + 1) * gd // V7X_LANES)], axis=1)
        ab = _dot(ug.astype(BF16), dft_ref[...])
        z_ref[0, 0, :, gi * gd:(gi + 1) * gd] = _words(ab[:, :gd])
        z_ref[0, 1, :, gi * gd:(gi + 1) * gd] = _words(ab[:, gd:])
    gr_ref[0] = proj(5).astype(BF16)
    gf_ref[0] = proj(6).astype(BF16)


def _in_proj(x, norm_w, w_in, cos, sin, dft, tm):
    b, s, d = x.shape
    half = d // RET_HEADS // 2
    gd = d // FOUR_GROUPS
    act = jax.ShapeDtypeStruct((b, s, d), BF16)
    assert tm % (2 * FFT_N2) == 0
    zsh = jax.ShapeDtypeStruct((b, 2, s // 2, d), jnp.int32)
    row = pl.BlockSpec((1, tm, d), lambda i, j: (i, j, 0))
    zrow = pl.BlockSpec((1, 2, tm // 2, d), lambda i, j: (i, 0, j, 0))
    tab = pl.BlockSpec((tm, half), lambda i, j: (j, 0))
    vmem = (w_in.size * 2 + 2 * tm * d * 4 + 2 * 8 * tm * d * 2 + 4 * tm * half * 4
            + dft.size * 2 + 5 * tm * d * 4)
    return pl.pallas_call(
        _in_proj_kernel,
        out_shape=(act, act, act, act, zsh, act, act),
        grid=(b, s // tm),
        in_specs=[row, _resident((1, d)), _resident(w_in.shape), tab, tab,
                  _resident((gd, 2 * gd))],
        out_specs=(row, row, row, row, zrow, row, row),
        scratch_shapes=[pltpu.VMEM((d // V7X_LANES, tm, V7X_LANES), F32)],
        compiler_params=pltpu.CompilerParams(
            dimension_semantics=("parallel", "parallel"),
            vmem_limit_bytes=_vmem_limit(vmem)),
        name="in_proj",
    )(x, norm_w, w_in, cos, sin, dft)


def _log_sigmoid(x):
    return jnp.minimum(x, 0.0) - jnp.log1p(jnp.exp(-jnp.abs(x)))


def _ret_bwd_kernel(dec_ref, q_ref, k_ref, v_ref, yb_ref, state_ref, xi_ref, zt_ref, gc_ref):
    c = RET_CHUNK
    t_len, d = q_ref.shape[1], q_ref.shape[2]
    dk = d // RET_HEADS

    @pl.when(pl.program_id(1) == 0)
    def _init():
        state_ref[...] = jnp.zeros_like(state_ref)
        for h in range(RET_HEADS):
            lg = _log_sigmoid(jnp.full((c, dk), dec_ref[h], F32))
            row = lax.broadcasted_iota(jnp.int32, (c, dk), 0).astype(F32)
            col = lax.broadcasted_iota(jnp.int32, (dk, c), 1).astype(F32)
            xi_ref[h] = jnp.exp(lg * (c - row))
            zt_ref[h] = jnp.exp(lg * col)
            gc_ref[h] = jnp.exp(_log_sigmoid(jnp.full((1, dk), dec_ref[h], F32)) * c)

    for h in range(RET_HEADS):
        cols = slice(h * dk, (h + 1) * dk)
        state = state_ref[h]
        for ci in reversed(range(t_len // c)):
            rows = slice(ci * c, (ci + 1) * c)
            q = q_ref[0, rows, cols]
            k = k_ref[0, rows, cols]
            v = v_ref[0, rows, cols]
            cross = _dot(q, state.astype(BF16)) * xi_ref[h]
            yb_ref[0, rows, cols] = cross.astype(BF16)
            kt = (k.astype(F32).T * zt_ref[h]).astype(BF16)
            state = state * gc_ref[h] + _dot(kt, v)
        state_ref[h] = state


def _ret_fwd_kernel(decf_ref, decb_ref, q_ref, k_ref, v_ref, g_ref, yb_ref, gnw_ref, r_ref,
                    state_ref, dmat_ref, xi_ref, zt_ref, gc_ref):
    c = RET_CHUNK
    t_len, d = q_ref.shape[1], q_ref.shape[2]
    dk = d // RET_HEADS

    @pl.when(pl.program_id(1) == 0)
    def _init():
        state_ref[...] = jnp.zeros_like(state_ref)
        for h in range(RET_HEADS):
            lgf = _log_sigmoid(jnp.full((c, dk), decf_ref[h], F32))
            lgb = _log_sigmoid(jnp.full((c, c), decb_ref[h], F32))
            row = lax.broadcasted_iota(jnp.int32, (c, dk), 0).astype(F32)
            col = lax.broadcasted_iota(jnp.int32, (dk, c), 1).astype(F32)
            xi_ref[h] = jnp.exp(lgf * (row + 1.0))
            zt_ref[h] = jnp.exp(lgf * (c - 1.0 - col))
            gc_ref[h] = jnp.exp(_log_sigmoid(jnp.full((1, dk), decf_ref[h], F32)) * c)
            i = lax.broadcasted_iota(jnp.int32, (c, c), 0)
            j = lax.broadcasted_iota(jnp.int32, (c, c), 1)
            diff = (i - j).astype(F32)
            lgf_cc = _log_sigmoid(jnp.full((c, c), decf_ref[h], F32))
            dmat_ref[h] = jnp.where(i >= j, jnp.exp(lgf_cc * jnp.maximum(diff, 0.0)),
                                    jnp.exp(lgb * jnp.maximum(-diff, 0.0)))

    for h in range(RET_HEADS):
        cols = slice(h * dk, (h + 1) * dk)
        state = state_ref[h]
        gnw = gnw_ref[:, cols]
        for ci in range(t_len // c):
            rows = slice(ci * c, (ci + 1) * c)
            q = q_ref[0, rows, cols]
            k = k_ref[0, rows, cols]
            v = v_ref[0, rows, cols]
            scores = lax.dot_general(q, k, (((1,), (1,)), ((), ())), preferred_element_type=F32)
            inner = _dot((scores * dmat_ref[h]).astype(BF16), v)
            cross = _dot(q, state.astype(BF16)) * xi_ref[h]
            y = inner + cross + yb_ref[0, rows, cols].astype(F32)
            kt = (k.astype(F32).T * zt_ref[h]).astype(BF16)
            state = state * gc_ref[h] + _dot(kt, v)
            mu = jnp.mean(y, axis=-1, keepdims=True)
            yc = y - mu
            var = jnp.mean(yc * yc, axis=-1, keepdims=True)
            yn = yc * lax.rsqrt(var + GN_EPS) * gnw
            g = g_ref[0, rows, cols].astype(F32)
            r_ref[0, rows, cols] = (g * jax.nn.sigmoid(g) * yn).astype(BF16)
        state_ref[h] = state


def _fft_a_kernel(z_ref, tab_ref, o_ref):
    for r in range(z_ref.shape[3]):
        zz = jnp.concatenate([_rows(z_ref[0, 0, :, r, :]), _rows(z_ref[0, 1, :, r, :])], axis=0)
        o_ref[0, :, r, :] = _words(_dot(tab_ref[r], zz))


def _fft_b_kernel(zt_ref, tab_ref, o_ref):
    for i in range(zt_ref.shape[1]):
        o_ref[0, :, i, :] = _words(_dot(tab_ref[...], _rows(zt_ref[0, i])))


def _ret_bwd_fft_a_kernel(dec_ref, q_ref, k_ref, v_ref, z_ref, tab_ref, yb_ref, zt_ref,
                          state_ref, xi_ref, zeta_ref, gc_ref):
    _ret_bwd_kernel(dec_ref, q_ref, k_ref, v_ref, yb_ref, state_ref, xi_ref, zeta_ref, gc_ref)
    _fft_a_kernel(z_ref, tab_ref, zt_ref)


def _ret_fwd_fft_b_kernel(decf_ref, decb_ref, q_ref, k_ref, v_ref, g_ref, yb_ref, gnw_ref,
                          zt_ref, tab_ref, r_ref, f_ref,
                          state_ref, dmat_ref, xi_ref, zeta_ref, gc_ref):
    _ret_fwd_kernel(decf_ref, decb_ref, q_ref, k_ref, v_ref, g_ref, yb_ref, gnw_ref, r_ref,
                    state_ref, dmat_ref, xi_ref, zeta_ref, gc_ref)
    _fft_b_kernel(zt_ref, tab_ref, f_ref)


def _mixers(q, k, v, g, z, decay_fwd, decay_bwd, gn_w, tab_a, tab_b):
    b, s, d = q.shape
    dk = d // RET_HEADS
    c = RET_CHUNK
    n2 = tab_b.shape[0]
    n1 = s // n2
    steps_a = n2 // SUBLANES
    steps_b = n1 // SUBLANES
    t_a, t_b = s // steps_a, s // steps_b
    assert t_a % c == 0 and t_b % c == 0
    act = jax.ShapeDtypeStruct((b, s, d), BF16)
    smem = pl.BlockSpec(memory_space=pltpu.SMEM)
    rev = pl.BlockSpec((1, t_a, d), lambda i, t: (i, steps_a - 1 - t, 0))
    fwd = pl.BlockSpec((1, t_b, d), lambda i, t: (i, t, 0))
    tables = [pltpu.VMEM((RET_HEADS, c, dk), F32),
              pltpu.VMEM((RET_HEADS, dk, c), F32),
              pltpu.VMEM((RET_HEADS, 1, dk), F32)]
    state = pltpu.VMEM((RET_HEADS, dk, dk), F32)
    table_bytes = RET_HEADS * (3 * c * dk + dk * dk + dk) * 4
    stage_a_bytes = 2 * (2 * n1 * SUBLANES * d * 4 + SUBLANES * 4 * n1 * n1 * 2) + 6 * n1 * d * 4
    stage_b_bytes = 2 * (SUBLANES * n2 * d * 4 + (n2 // 2) * SUBLANES * d * 4) + 4 * n2 * d * 4
    yb, zt = pl.pallas_call(
        _ret_bwd_fft_a_kernel,
        out_shape=(act, jax.ShapeDtypeStruct((b, n1, n2, d), jnp.int32)),
        grid=(b, steps_a),
        in_specs=[smem, rev, rev, rev,
                  pl.BlockSpec((1, 2, n1 // 2, SUBLANES, d), lambda i, t: (i, 0, 0, t, 0)),
                  pl.BlockSpec((SUBLANES, 2 * n1, 2 * n1), lambda i, t: (t, 0, 0))],
        out_specs=(rev, pl.BlockSpec((1, n1, SUBLANES, d), lambda i, t: (i, 0, t, 0))),
        scratch_shapes=[state] + tables,
        compiler_params=pltpu.CompilerParams(
            dimension_semantics=("arbitrary", "arbitrary"),
            vmem_limit_bytes=_vmem_limit(2 * 4 * t_a * d * 2 + table_bytes + stage_a_bytes)),
        name="ret_bwd_fft_a",
    )(decay_bwd, q, k, v, z.reshape(b, 2, n1 // 2, n2, d), tab_a)
    return pl.pallas_call(
        _ret_fwd_fft_b_kernel,
        out_shape=(act, jax.ShapeDtypeStruct((b, n2 // 2, n1, d), jnp.int32)),
        grid=(b, steps_b),
        in_specs=[smem, smem, fwd, fwd, fwd, fwd, fwd, _resident((1, d)),
                  pl.BlockSpec((1, SUBLANES, n2, d), lambda i, t: (i, t, 0, 0)),
                  _resident((n2, 2 * n2))],
        out_specs=(fwd, pl.BlockSpec((1, n2 // 2, SUBLANES, d), lambda i, t: (i, 0, t, 0))),
        scratch_shapes=[state, pltpu.VMEM((RET_HEADS, c, c), F32)] + tables,
        compiler_params=pltpu.CompilerParams(
            dimension_semantics=("arbitrary", "arbitrary"),
            vmem_limit_bytes=_vmem_limit(2 * 6 * t_b * d * 2 + table_bytes + stage_b_bytes)),
        name="ret_fwd_fft_b",
    )(decay_fwd, decay_bwd, q, k, v, g, yb, gn_w, zt, tab_b)


def _tail_kernel(x_ref, r_ref, f_ref, gr_ref, gf_ref, ck_ref, cv_ref,
                 w_ro_ref, w_fo_ref, w_mo_ref, n_ca_ref, w_cq_ref, w_co_ref,
                 n_mlp_ref, w_up_ref, w_dn_ref, n_fin_ref, o_ref, att_ref, perm_ref):
    d = x_ref.shape[-1]
    dh = d // CA_HEADS
    ff = w_up_ref.shape[1]
    gq, n1 = f_ref.shape[1], f_ref.shape[2]
    x = x_ref[0]
    ret = _dot(r_ref[0], w_ro_ref[...])
    fperm = _dot(_rows(f_ref[0].reshape(gq * n1, d)), w_fo_ref[...])
    for lc in range(d // V7X_LANES):
        perm_ref[lc] = fperm[:, lc * V7X_LANES:(lc + 1) * V7X_LANES]
    fou = jnp.concatenate(
        [jnp.concatenate([perm_ref[lc, pl.ds(blk // 2 * 2 * n1 + blk % 2, n1, stride=2), :]
                          for lc in range(d // V7X_LANES)], axis=1)
         for blk in range(2 * gq)], axis=0)
    merged = (jax.nn.sigmoid(gr_ref[0].astype(F32)) * ret
              + jax.nn.sigmoid(gf_ref[0].astype(F32)) * fou)
    x = x + _dot(merged.astype(BF16), w_mo_ref[...])

    hq = _dot(_rms(x, n_ca_ref[...]).astype(BF16), w_cq_ref[...]).astype(BF16)
    for h in range(CA_HEADS):
        cols = slice(h * dh, (h + 1) * dh)
        logits = lax.dot_general(hq[:, cols], ck_ref[0, :, cols], (((1,), (1,)), ((), ())),
                                 preferred_element_type=F32) * (dh ** -0.5)
        p = jnp.exp(logits - jnp.max(logits, axis=-1, keepdims=True))
        probs = p / jnp.sum(p, axis=-1, keepdims=True)
        att_ref[:, cols] = _dot(probs.astype(BF16), cv_ref[0, :, cols]).astype(BF16)
    x = x + _dot(att_ref[...], w_co_ref[...])

    xb = _rms(x, n_mlp_ref[...]).astype(BF16)
    acc = x
    for j in range(ff // d):
        cols = slice(j * d, (j + 1) * d)
        hid = jnp.square(jnp.maximum(_dot(xb, w_up_ref[:, cols]), 0.0)).astype(BF16)
        acc = acc + _dot(hid, w_dn_ref[cols, :])
    o_ref[0] = _rms(acc, n_fin_ref[...])


def _tail(x, r, four, gate_r, gate_f, ck, cv, w_ro, w_fo, w_mo, n_ca, w_cq, w_co,
          n_mlp, w_up, w_dn, n_fin, tm):
    b, s, d = x.shape
    m = ck.shape[1]
    ff = w_up.shape[1]
    row = pl.BlockSpec((1, tm, d), lambda i, j: (i, j, 0))
    n1 = four.shape[2]
    assert tm % (2 * n1) == 0
    half = pl.BlockSpec((1, tm // (2 * n1), n1, d), lambda i, j: (i, j, 0, 0))
    mem = pl.BlockSpec((1, m, d), lambda i, j: (i, 0, 0))
    vec = _resident((1, d))
    sq = _resident((d, d))
    vmem = ((5 * d * d + 2 * d * ff) * 2 + 4 * m * d * 2 + 2 * tm * d * (4 + 4 + 4 * 2)
            + tm * d * 4 * 8)
    return pl.pallas_call(
        _tail_kernel,
        out_shape=jax.ShapeDtypeStruct((b, s, d), F32),
        grid=(b, s // tm),
        in_specs=[row, row, half, row, row, mem, mem, sq, sq, sq, vec, sq, sq, vec,
                  _resident((d, ff)), _resident((ff, d)), vec],
        out_specs=row,
        scratch_shapes=[pltpu.VMEM((tm, d), BF16),
                        pltpu.VMEM((d // V7X_LANES, tm, V7X_LANES), F32)],
        compiler_params=pltpu.CompilerParams(
            dimension_semantics=("parallel", "parallel"),
            vmem_limit_bytes=_vmem_limit(vmem)),
        name="tail",
    )(x, r, four, gate_r, gate_f, ck, cv, w_ro, w_fo, w_mo, n_ca, w_cq, w_co,
      n_mlp, w_up, w_dn, n_fin)


def _tiles(s):
    return min(512, s), min(512, s)


def _trunk(x, mem, p):
    b, s, d = x.shape
    dk = d // RET_HEADS
    gd = d // FOUR_GROUPS
    n1 = s // FFT_N2
    assert s % FFT_N2 == 0 and s % RET_CHUNK == 0
    tm_in, tm_tail = _tiles(s)
    cos, sin = (jnp.asarray(t) for t in _rope_tables(s, dk))
    dft = jnp.asarray(_chan_dft(gd)).astype(BF16)
    tab_a = jnp.asarray(_fft_a_table(n1, FFT_N2)).astype(BF16)
    tab_b = jnp.asarray(_fft_b_table(FFT_N2, float(s * gd) ** -0.5)).astype(BF16)

    for l in range(p["w_in"].shape[0]):
        ck, cv = _mem_kv(mem, p["norm_mem_w"][l], p["w_ck"][l], p["w_cv"][l])
        q, k, v, g, z, gate_r, gate_f = _in_proj(x, p["norm_mix_w"][l], p["w_in"][l],
                                                 cos, sin, dft, tm_in)
        r, four = _mixers(q, k, v, g, z, p["ret_decay_fwd"][l], p["ret_decay_bwd"][l],
                          p["ret_gn_w"][l], tab_a, tab_b)
        last = l == p["w_in"].shape[0] - 1
        n_fin = p["norm_final_w"] if last else None
        assert last, "a single layer is fused with the final norm"
        x = _tail(x, r, four, gate_r, gate_f, ck, cv, p["w_ret_out"][l], p["w_four_out"][l],
                  p["w_mix_out"][l], p["norm_ca_w"][l], p["w_cq"][l], p["w_co"][l],
                  p["norm_mlp_w"][l], p["w_up"][l], p["w_down"][l], n_fin, tm_tail)
    return x


def kernel(x_prompt, x_sample, mem_prompt, mem_sample, norm_mix_w, w_in, ret_decay_fwd,
           ret_decay_bwd, ret_gn_w, w_ret_out, w_four_out, w_mix_out, norm_ca_w, norm_mem_w,
           w_cq, w_ck, w_cv, w_co, norm_mlp_w, w_up, w_down, norm_final_w):
    depth, d = norm_mix_w.shape
    vec = lambda w: w.reshape(depth, 1, d)
    mat = lambda w: w.astype(BF16)
    p = dict(
        norm_mix_w=vec(norm_mix_w), w_in=mat(w_in), ret_decay_fwd=ret_decay_fwd,
        ret_decay_bwd=ret_decay_bwd, ret_gn_w=vec(ret_gn_w), w_ret_out=mat(w_ret_out),
        w_four_out=mat(w_four_out), w_mix_out=mat(w_mix_out), norm_ca_w=vec(norm_ca_w),
        norm_mem_w=vec(norm_mem_w), w_cq=mat(w_cq), w_ck=mat(w_ck), w_cv=mat(w_cv),
        w_co=mat(w_co), norm_mlp_w=vec(norm_mlp_w), w_up=mat(w_up), w_down=mat(w_down),
        norm_final_w=norm_final_w.reshape(1, d))
    return _trunk(x_prompt, mem_prompt, p), _trunk(x_sample, mem_sample, p)
```

```python
import functools

import numpy as np
import jax
import jax.numpy as jnp
from jax import lax
from jax.experimental import pallas as pl
from jax.experimental.pallas import tpu as pltpu

RET_HEADS = 4
FOUR_GROUPS = 4
CA_HEADS = 4
ROPE_THETA = 10000.0
EPS = 1e-6
GN_EPS = 1e-5

V7X_VMEM_BYTES = 64 * 1024 * 1024
V7X_MXU_DIM = 256
V7X_LANES = 128
SUBLANES = 8

RET_CHUNK = V7X_MXU_DIM
FFT_N2 = 128

BF16 = jnp.bfloat16
F32 = jnp.float32


def _vmem_limit(nbytes):
    return int(min(nbytes + 8 * 1024 * 1024, V7X_VMEM_BYTES - 6 * 1024 * 1024))


def _rms(x, w):
    return x * lax.rsqrt(jnp.mean(x * x, axis=-1, keepdims=True) + EPS) * w


def _dot(a, b):
    return jnp.dot(a, b, preferred_element_type=F32)


def _words(x):
    return pltpu.bitcast(x.astype(BF16), jnp.int32)


def _rows(w):
    return pltpu.bitcast(w, BF16)


def _resident(shape):
    return pl.BlockSpec(shape, lambda *_: (0,) * len(shape), pipeline_mode=pl.Buffered(1))


@functools.lru_cache(maxsize=None)
def _rope_tables(s, dk):
    inv = ROPE_THETA ** (-np.arange(0, dk, 2, dtype=np.float64) / dk)
    ang = np.arange(s, dtype=np.float64)[:, None] * inv[None, :]
    return np.cos(ang).astype(np.float32), np.sin(ang).astype(np.float32)


@functools.lru_cache(maxsize=None)
def _chan_dft(n):
    ang = 2.0 * np.pi * np.outer(np.arange(n), np.arange(n)) / n
    return np.concatenate([np.cos(ang), -np.sin(ang)], axis=1).astype(np.float32)


@functools.lru_cache(maxsize=None)
def _fft_a_table(n1, n2):
    n = n1 * n2
    k1 = np.arange(n1, dtype=np.float64)[None, :, None]
    tok = (n2 * np.arange(n1, dtype=np.float64)[None, None, :]
           + np.arange(n2, dtype=np.float64)[:, None, None])
    ang = 2.0 * np.pi * np.mod(k1 * tok, n) / n
    c, s = np.cos(ang), np.sin(ang)
    re_rows = np.concatenate([c, s], axis=2)
    im_rows = np.concatenate([-s, c], axis=2)
    return np.stack([re_rows, im_rows], axis=2).reshape(n2, 2 * n1, 2 * n1).astype(np.float32)


@functools.lru_cache(maxsize=None)
def _fft_b_table(n2, scale):
    ang = 2.0 * np.pi * np.outer(np.arange(n2), np.arange(n2)) / n2
    return (scale * np.stack([np.cos(ang), np.sin(ang)], axis=-1).reshape(n2, 2 * n2)).astype(np.float32)


def _mem_kv_kernel(mem_ref, nw_ref, wk_ref, wv_ref, wq_ref, wo_ref, wqk_ref, wvo_ref):
    m, d = mem_ref.shape[1], mem_ref.shape[2]
    dh = d // CA_HEADS
    mn = _rms(mem_ref[0], nw_ref[...]).astype(BF16)
    ck = _dot(mn, wk_ref[...]).astype(BF16)
    cv = _dot(mn, wv_ref[...]).astype(BF16)
    for h in range(CA_HEADS):
        cols = slice(h * dh, (h + 1) * dh)
        qk = lax.dot_general(wq_ref[:, cols], ck[:, cols], (((1,), (1,)), ((), ())),
                             preferred_element_type=F32) * (dh ** -0.5)
        wqk_ref[0, :, h * m:(h + 1) * m] = qk.astype(BF16)
        wvo_ref[0, h * m:(h + 1) * m, :] = _dot(cv[:, cols], wo_ref[cols, :]).astype(BF16)


def _mem_kv(mem, norm_w, w_ck, w_cv, w_cq, w_co):
    b, m, d = mem.shape
    hm = CA_HEADS * m
    sq = _resident((d, d))
    return pl.pallas_call(
        _mem_kv_kernel,
        out_shape=(jax.ShapeDtypeStruct((b, d, hm), BF16), jax.ShapeDtypeStruct((b, hm, d), BF16)),
        grid=(b,),
        in_specs=[pl.BlockSpec((1, m, d), lambda i: (i, 0, 0)), _resident((1, d)), sq, sq, sq, sq],
        out_specs=(pl.BlockSpec((1, d, hm), lambda i: (i, 0, 0)),
                   pl.BlockSpec((1, hm, d), lambda i: (i, 0, 0))),
        compiler_params=pltpu.CompilerParams(
            dimension_semantics=("arbitrary",),
            vmem_limit_bytes=_vmem_limit(4 * d * d * 2 + 4 * m * d * 4 + 4 * d * hm * 2
                                         + 2 * d * hm * 4)),
        name="mem_kv",
    )(mem, norm_w, w_ck, w_cv, w_cq, w_co)


def _in_proj_kernel(x_ref, nw_ref, w_ref, cos_ref, sin_ref, dft_ref,
                    q_ref, k_ref, v_ref, g_ref, z_ref, gr_ref, gf_ref, perm_ref):
    d = x_ref.shape[-1]
    dk = d // RET_HEADS
    half = dk // 2
    gd = d // FOUR_GROUPS
    xb = _rms(x_ref[0], nw_ref[...]).astype(BF16)
    cos = cos_ref[...]
    sin = sin_ref[...]

    def proj(j):
        return _dot(xb, w_ref[:, j * d:(j + 1) * d])

    def rope(p, out_ref, scale):
        for h in range(RET_HEADS):
            x1 = p[:, h * dk:h * dk + half]
            x2 = p[:, h * dk + half:(h + 1) * dk]
            o1 = x1 * cos - x2 * sin
            o2 = x2 * cos + x1 * sin
            if scale is not None:
                o1 = o1 * scale
                o2 = o2 * scale
            out_ref[0, :, h * dk:h * dk + half] = o1.astype(BF16)
            out_ref[0, :, h * dk + half:(h + 1) * dk] = o2.astype(BF16)

    rope(proj(0), q_ref, dk ** -0.5)
    rope(proj(1), k_ref, None)
    v_ref[0] = proj(2).astype(BF16)
    g_ref[0] = proj(3).astype(BF16)
    u = proj(4)
    for lc in range(d // V7X_LANES):
        for blk in range(u.shape[0] // FFT_N2):
            a, par = divmod(blk, 2)
            perm_ref[lc, pl.ds(2 * FFT_N2 * a + par, FFT_N2, stride=2), :] = (
                u[blk * FFT_N2:(blk + 1) * FFT_N2, lc * V7X_LANES:(lc + 1) * V7X_LANES])
    for gi in range(FOUR_GROUPS):
        ug = jnp.concatenate([perm_ref[lc] for lc in range(gi * gd // V7X_LANES,
                                                           (gi + 1) * gd // V7X_LANES)], axis=1)
        ab = _dot(ug.astype(BF16), dft_ref[...])
        z_ref[0, 0, :, gi * gd:(gi + 1) * gd] = _words(ab[:, :gd])
        z_ref[0, 1, :, gi * gd:(gi + 1) * gd] = _words(ab[:, gd:])
    gr_ref[0] = proj(5).astype(BF16)
    gf_ref[0] = proj(6).astype(BF16)


def _in_proj(x, norm_w, w_in, cos, sin, dft, tm):
    b, s, d = x.shape
    half = d // RET_HEADS // 2
    gd = d // FOUR_GROUPS
    act = jax.ShapeDtypeStruct((b, s, d), BF16)
    assert tm % (2 * FFT_N2) == 0
    zsh = jax.ShapeDtypeStruct((b, 2, s // 2, d), jnp.int32)
    row = pl.BlockSpec((1, tm, d), lambda i, j: (i, j, 0))
    zrow = pl.BlockSpec((1, 2, tm // 2, d), lambda i, j: (i, 0, j, 0))
    tab = pl.BlockSpec((tm, half), lambda i, j: (j, 0))
    vmem = (w_in.size * 2 + 2 * tm * d * 4 + 2 * 8 * tm * d * 2 + 4 * tm * half * 4
            + dft.size * 2 + 5 * tm * d * 4)
    return pl.pallas_call(
        _in_proj_kernel,
        out_shape=(act, act, act, act, zsh, act, act),
        grid=(b, s // tm),
        in_specs=[row, _resident((1, d)), _resident(w_in.shape), tab, tab,
                  _resident((gd, 2 * gd))],
        out_specs=(row, row, row, row, zrow, row, row),
        scratch_shapes=[pltpu.VMEM((d // V7X_LANES, tm, V7X_LANES), F32)],
        compiler_params=pltpu.CompilerParams(
            dimension_semantics=("parallel", "parallel"),
            vmem_limit_bytes=_vmem_limit(vmem)),
        name="in_proj",
    )(x, norm_w, w_in, cos, sin, dft)


def _log_sigmoid(x):
    return jnp.minimum(x, 0.0) - jnp.log1p(jnp.exp(-jnp.abs(x)))


def _ret_bwd_kernel(dec_ref, q_ref, k_ref, v_ref, yb_ref, state_ref, xi_ref, zt_ref, gc_ref):
    c = RET_CHUNK
    t_len, d = q_ref.shape[1], q_ref.shape[2]
    dk = d // RET_HEADS

    @pl.when(pl.program_id(1) == 0)
    def _init():
        state_ref[...] = jnp.zeros_like(state_ref)
        for h in range(RET_HEADS):
            lg = _log_sigmoid(jnp.full((c, dk), dec_ref[h], F32))
            row = lax.broadcasted_iota(jnp.int32, (c, dk), 0).astype(F32)
            col = lax.broadcasted_iota(jnp.int32, (dk, c), 1).astype(F32)
            xi_ref[h] = jnp.exp(lg * (c - row))
            zt_ref[h] = jnp.exp(lg * col)
            gc_ref[h] = jnp.exp(_log_sigmoid(jnp.full((1, dk), dec_ref[h], F32)) * c)

    for h in range(RET_HEADS):
        cols = slice(h * dk, (h + 1) * dk)
        state = state_ref[h]
        for ci in reversed(range(t_len // c)):
            rows = slice(ci * c, (ci + 1) * c)
            q = q_ref[0, rows, cols]
            k = k_ref[0, rows, cols]
            v = v_ref[0, rows, cols]
            cross = _dot(q, state.astype(BF16)) * xi_ref[h]
            yb_ref[0, rows, cols] = cross.astype(BF16)
            kt = (k.astype(F32).T * zt_ref[h]).astype(BF16)
            state = state * gc_ref[h] + _dot(kt, v)
        state_ref[h] = state


def _ret_fwd_kernel(decf_ref, decb_ref, q_ref, k_ref, v_ref, g_ref, yb_ref, gnw_ref, r_ref,
                    state_ref, dmat_ref, xi_ref, zt_ref, gc_ref):
    c = RET_CHUNK
    t_len, d = q_ref.shape[1], q_ref.shape[2]
    dk = d // RET_HEADS

    @pl.when(pl.program_id(1) == 0)
    def _init():
        state_ref[...] = jnp.zeros_like(state_ref)
        for h in range(RET_HEADS):
            lgf = _log_sigmoid(jnp.full((c, dk), decf_ref[h], F32))
            lgb = _log_sigmoid(jnp.full((c, c), decb_ref[h], F32))
            row = lax.broadcasted_iota(jnp.int32, (c, dk), 0).astype(F32)
            col = lax.broadcasted_iota(jnp.int32, (dk, c), 1).astype(F32)
            xi_ref[h] = jnp.exp(lgf * (row + 1.0))
            zt_ref[h] = jnp.exp(lgf * (c - 1.0 - col))
            gc_ref[h] = jnp.exp(_log_sigmoid(jnp.full((1, dk), decf_ref[h], F32)) * c)
            i = lax.broadcasted_iota(jnp.int32, (c, c), 0)
            j = lax.broadcasted_iota(jnp.int32, (c, c), 1)
            diff = (i - j).astype(F32)
            lgf_cc = _log_sigmoid(jnp.full((c, c), decf_ref[h], F32))
            dmat_ref[h] = jnp.where(i >= j, jnp.exp(lgf_cc * jnp.maximum(diff, 0.0)),
                                    jnp.exp(lgb * jnp.maximum(-diff, 0.0)))

    for h in range(RET_HEADS):
        cols = slice(h * dk, (h + 1) * dk)
        state = state_ref[h]
        gnw = gnw_ref[:, cols]
        for ci in range(t_len // c):
            rows = slice(ci * c, (ci + 1) * c)
            q = q_ref[0, rows, cols]
            k = k_ref[0, rows, cols]
            v = v_ref[0, rows, cols]
            scores = lax.dot_general(q, k, (((1,), (1,)), ((), ())), preferred_element_type=F32)
            inner = _dot((scores * dmat_ref[h]).astype(BF16), v)
            cross = _dot(q, state.astype(BF16)) * xi_ref[h]
            y = inner + cross + yb_ref[0, rows, cols].astype(F32)
            kt = (k.astype(F32).T * zt_ref[h]).astype(BF16)
            state = state * gc_ref[h] + _dot(kt, v)
            mu = jnp.mean(y, axis=-1, keepdims=True)
            yc = y - mu
            var = jnp.mean(yc * yc, axis=-1, keepdims=True)
            yn = yc * lax.rsqrt(var + GN_EPS) * gnw
            g = g_ref[0, rows, cols].astype(F32)
            r_ref[0, rows, cols] = (g * jax.nn.sigmoid(g) * yn).astype(BF16)
        state_ref[h] = state


def _fft_a_kernel(z_ref, tab_ref, o_ref):
    for r in range(z_ref.shape[3]):
        zz = jnp.concatenate([_rows(z_ref[0, 0, :, r, :]), _rows(z_ref[0, 1, :, r, :])], axis=0)
        o_ref[0, :, r, :] = _words(_dot(tab_ref[r], zz))


def _fft_b_kernel(zt_ref, tab_ref, o_ref):
    for i in range(zt_ref.shape[1]):
        o_ref[0, :, i, :] = _words(_dot(tab_ref[...], _rows(zt_ref[0, i])))


def _ret_bwd_fft_a_kernel(dec_ref, q_ref, k_ref, v_ref, z_ref, tab_ref, yb_ref, zt_ref,
                          state_ref, xi_ref, zeta_ref, gc_ref):
    _ret_bwd_kernel(dec_ref, q_ref, k_ref, v_ref, yb_ref, state_ref, xi_ref, zeta_ref, gc_ref)
    _fft_a_kernel(z_ref, tab_ref, zt_ref)


def _ret_fwd_fft_b_kernel(decf_ref, decb_ref, q_ref, k_ref, v_ref, g_ref, yb_ref, gnw_ref,
                          zt_ref, tab_ref, r_ref, f_ref,
                          state_ref, dmat_ref, xi_ref, zeta_ref, gc_ref):
    _ret_fwd_kernel(decf_ref, decb_ref, q_ref, k_ref, v_ref, g_ref, yb_ref, gnw_ref, r_ref,
                    state_ref, dmat_ref, xi_ref, zeta_ref, gc_ref)
    _fft_b_kernel(zt_ref, tab_ref, f_ref)


def _mixers(q, k, v, g, z, decay_fwd, decay_bwd, gn_w, tab_a, tab_b):
    b, s, d = q.shape
    dk = d // RET_HEADS
    c = RET_CHUNK
    n2 = tab_b.shape[0]
    n1 = s // n2
    steps_a = n2 // SUBLANES
    steps_b = n1 // SUBLANES
    t_a, t_b = s // steps_a, s // steps_b
    assert t_a % c == 0 and t_b % c == 0
    act = jax.ShapeDtypeStruct((b, s, d), BF16)
    smem = pl.BlockSpec(memory_space=pltpu.SMEM)
    rev = pl.BlockSpec((1, t_a, d), lambda i, t: (i, steps_a - 1 - t, 0))
    fwd = pl.BlockSpec((1, t_b, d), lambda i, t: (i, t, 0))
    tables = [pltpu.VMEM((RET_HEADS, c, dk), F32),
              pltpu.VMEM((RET_HEADS, dk, c), F32),
              pltpu.VMEM((RET_HEADS, 1, dk), F32)]
    state = pltpu.VMEM((RET_HEADS, dk, dk), F32)
    table_bytes = RET_HEADS * (3 * c * dk + dk * dk + dk) * 4
    stage_a_bytes = 2 * (2 * n1 * SUBLANES * d * 4 + SUBLANES * 4 * n1 * n1 * 2) + 6 * n1 * d * 4
    stage_b_bytes = 2 * (SUBLANES * n2 * d * 4 + (n2 // 2) * SUBLANES * d * 4) + 4 * n2 * d * 4
    yb, zt = pl.pallas_call(
        _ret_bwd_fft_a_kernel,
        out_shape=(act, jax.ShapeDtypeStruct((b, n1, n2, d), jnp.int32)),
        grid=(b, steps_a),
        in_specs=[smem, rev, rev, rev,
                  pl.BlockSpec((1, 2, n1 // 2, SUBLANES, d), lambda i, t: (i, 0, 0, t, 0)),
                  pl.BlockSpec((SUBLANES, 2 * n1, 2 * n1), lambda i, t: (t, 0, 0))],
        out_specs=(rev, pl.BlockSpec((1, n1, SUBLANES, d), lambda i, t: (i, 0, t, 0))),
        scratch_shapes=[state] + tables,
        compiler_params=pltpu.CompilerParams(
            dimension_semantics=("arbitrary", "arbitrary"),
            vmem_limit_bytes=_vmem_limit(2 * 4 * t_a * d * 2 + table_bytes + stage_a_bytes)),
        name="ret_bwd_fft_a",
    )(decay_bwd, q, k, v, z.reshape(b, 2, n1 // 2, n2, d), tab_a)
    return pl.pallas_call(
        _ret_fwd_fft_b_kernel,
        out_shape=(act, jax.ShapeDtypeStruct((b, n2 // 2, n1, d), jnp.int32)),
        grid=(b, steps_b),
        in_specs=[smem, smem, fwd, fwd, fwd, fwd, fwd, _resident((1, d)),
                  pl.BlockSpec((1, SUBLANES, n2, d), lambda i, t: (i, t, 0, 0)),
                  _resident((n2, 2 * n2))],
        out_specs=(fwd, pl.BlockSpec((1, n2 // 2, SUBLANES, d), lambda i, t: (i, 0, t, 0))),
        scratch_shapes=[state, pltpu.VMEM((RET_HEADS, c, c), F32)] + tables,
        compiler_params=pltpu.CompilerParams(
            dimension_semantics=("arbitrary", "arbitrary"),
            vmem_limit_bytes=_vmem_limit(2 * 6 * t_b * d * 2 + table_bytes + stage_b_bytes)),
        name="ret_fwd_fft_b",
    )(decay_fwd, decay_bwd, q, k, v, g, yb, gn_w, zt, tab_b)


def _tail_kernel(x_ref, r_ref, f_ref, gr_ref, gf_ref, wqk_ref, wvo_ref,
                 w_ro_ref, w_fo_ref, w_mo_ref, n_ca_ref,
                 n_mlp_ref, w_up_ref, w_dn_ref, n_fin_ref, o_ref, att_ref, perm_ref):
    d = x_ref.shape[-1]
    m = wqk_ref.shape[2] // CA_HEADS
    ff = w_up_ref.shape[1]
    gq, n1 = f_ref.shape[1], f_ref.shape[2]
    x = x_ref[0]
    ret = _dot(r_ref[0], w_ro_ref[...])
    fperm = _dot(_rows(f_ref[0].reshape(gq * n1, d)), w_fo_ref[...])
    for lc in range(d // V7X_LANES):
        perm_ref[lc] = fperm[:, lc * V7X_LANES:(lc + 1) * V7X_LANES]
    fou = jnp.concatenate(
        [jnp.concatenate([perm_ref[lc, pl.ds(blk // 2 * 2 * n1 + blk % 2, n1, stride=2), :]
                          for lc in range(d // V7X_LANES)], axis=1)
         for blk in range(2 * gq)], axis=0)
    merged = (jax.nn.sigmoid(gr_ref[0].astype(F32)) * ret
              + jax.nn.sigmoid(gf_ref[0].astype(F32)) * fou)
    x = x + _dot(merged.astype(BF16), w_mo_ref[...])

    logits = _dot(_rms(x, n_ca_ref[...]).astype(BF16), wqk_ref[0])
    for h in range(CA_HEADS):
        cols = slice(h * m, (h + 1) * m)
        p = jnp.exp(logits[:, cols] - jnp.max(logits[:, cols], axis=-1, keepdims=True))
        att_ref[:, cols] = (p / jnp.sum(p, axis=-1, keepdims=True)).astype(BF16)
    x = x + _dot(att_ref[...], wvo_ref[0])

    xb = _rms(x, n_mlp_ref[...]).astype(BF16)
    acc = x
    for j in range(ff // d):
        cols = slice(j * d, (j + 1) * d)
        hid = jnp.square(jnp.maximum(_dot(xb, w_up_ref[:, cols]), 0.0)).astype(BF16)
        acc = acc + _dot(hid, w_dn_ref[cols, :])
    o_ref[0] = _rms(acc, n_fin_ref[...])


def _tail(x, r, four, gate_r, gate_f, wqk, wvo, w_ro, w_fo, w_mo, n_ca,
          n_mlp, w_up, w_dn, n_fin, tm):
    b, s, d = x.shape
    hm = wqk.shape[2]
    ff = w_up.shape[1]
    row = pl.BlockSpec((1, tm, d), lambda i, j: (i, j, 0))
    n1 = four.shape[2]
    assert tm % (2 * n1) == 0
    half = pl.BlockSpec((1, tm // (2 * n1), n1, d), lambda i, j: (i, j, 0, 0))
    vec = _resident((1, d))
    sq = _resident((d, d))
    vmem = ((3 * d * d + 2 * d * ff) * 2 + 4 * d * hm * 2 + 2 * tm * d * (4 + 4 + 4 * 2)
            + tm * d * 4 * 8)
    return pl.pallas_call(
        _tail_kernel,
        out_shape=jax.ShapeDtypeStruct((b, s, d), F32),
        grid=(b, s // tm),
        in_specs=[row, row, half, row, row,
                  pl.BlockSpec((1, d, hm), lambda i, j: (i, 0, 0)),
                  pl.BlockSpec((1, hm, d), lambda i, j: (i, 0, 0)),
                  sq, sq, sq, vec, vec, _resident((d, ff)), _resident((ff, d)), vec],
        out_specs=row,
        scratch_shapes=[pltpu.VMEM((tm, hm), BF16),
                        pltpu.VMEM((d // V7X_LANES, tm, V7X_LANES), F32)],
        compiler_params=pltpu.CompilerParams(
            dimension_semantics=("parallel", "parallel"),
            vmem_limit_bytes=_vmem_limit(vmem)),
        name="tail",
    )(x, r, four, gate_r, gate_f, wqk, wvo, w_ro, w_fo, w_mo, n_ca, n_mlp, w_up, w_dn, n_fin)


def _tiles(s):
    return min(512, s), min(512, s)


def _trunk(x, mem, p):
    b, s, d = x.shape
    dk = d // RET_HEADS
    gd = d // FOUR_GROUPS
    n1 = s // FFT_N2
    assert s % FFT_N2 == 0 and s % RET_CHUNK == 0
    tm_in, tm_tail = _tiles(s)
    cos, sin = (jnp.asarray(t) for t in _rope_tables(s, dk))
    dft = jnp.asarray(_chan_dft(gd)).astype(BF16)
    tab_a = jnp.asarray(_fft_a_table(n1, FFT_N2)).astype(BF16)
    tab_b = jnp.asarray(_fft_b_table(FFT_N2, float(s * gd) ** -0.5)).astype(BF16)

    for l in range(p["w_in"].shape[0]):
        wqk, wvo = _mem_kv(mem, p["norm_mem_w"][l], p["w_ck"][l], p["w_cv"][l],
                           p["w_cq"][l], p["w_co"][l])
        q, k, v, g, z, gate_r, gate_f = _in_proj(x, p["norm_mix_w"][l], p["w_in"][l],
                                                 cos, sin, dft, tm_in)
        r, four = _mixers(q, k, v, g, z, p["ret_decay_fwd"][l], p["ret_decay_bwd"][l],
                          p["ret_gn_w"][l], tab_a, tab_b)
        last = l == p["w_in"].shape[0] - 1
        n_fin = p["norm_final_w"] if last else None
        assert last, "a single layer is fused with the final norm"
        x = _tail(x, r, four, gate_r, gate_f, wqk, wvo, p["w_ret_out"][l], p["w_four_out"][l],
                  p["w_mix_out"][l], p["norm_ca_w"][l], p["norm_mlp_w"][l], p["w_up"][l],
                  p["w_down"][l], n_fin, tm_tail)
    return x


def kernel(x_prompt, x_sample, mem_prompt, mem_sample, norm_mix_w, w_in, ret_decay_fwd,
           ret_decay_bwd, ret_gn_w, w_ret_out, w_four_out, w_mix_out, norm_ca_w, norm_mem_w,
           w_cq, w_ck, w_cv, w_co, norm_mlp_w, w_up, w_down, norm_final_w):
    depth, d = norm_mix_w.shape
    vec = lambda w: w.reshape(depth, 1, d)
    mat = lambda w: w.astype(BF16)
    p = dict(
        norm_mix_w=vec(norm_mix_w), w_in=mat(w_in), ret_decay_fwd=ret_decay_fwd,
        ret_decay_bwd=ret_decay_bwd, ret_gn_w=vec(ret_gn_w), w_ret_out=mat(w_ret_out),
        w_four_out=mat(w_four_out), w_mix_out=mat(w_mix_out), norm_ca_w=vec(norm_ca_w),
        norm_mem_w=vec(norm_mem_w), w_cq=mat(w_cq), w_ck=mat(w_ck), w_cv=mat(w_cv),
        w_co=mat(w_co), norm_mlp_w=vec(norm_mlp_w), w_up=mat(w_up), w_down=mat(w_down),
        norm_final_w=norm_final_w.reshape(1, d))
    return _trunk(x_prompt, mem_prompt, p), _trunk(x_sample, mem_sample, p)
```

```python
import functools

import numpy as np
import jax
import jax.numpy as jnp
from jax import lax
from jax.experimental import pallas as pl
from jax.experimental.pallas import tpu as pltpu

RET_HEADS = 4
FOUR_GROUPS = 4
CA_HEADS = 4
ROPE_THETA = 10000.0
EPS = 1e-6
GN_EPS = 1e-5

V7X_VMEM_BYTES = 64 * 1024 * 1024
V7X_MXU_DIM = 256
V7X_LANES = 128
SUBLANES = 8

RET_CHUNK = V7X_MXU_DIM
FFT_N2 = 128

BF16 = jnp.bfloat16
F32 = jnp.float32


def _vmem_limit(nbytes):
    return int(min(nbytes + 8 * 1024 * 1024, V7X_VMEM_BYTES - 6 * 1024 * 1024))


def _rms(x, w):
    return x * lax.rsqrt(jnp.mean(x * x, axis=-1, keepdims=True) + EPS) * w


def _dot(a, b):
    return jnp.dot(a, b, preferred_element_type=F32)


def _words(x):
    return pltpu.bitcast(x.astype(BF16), jnp.int32)


def _rows(w):
    return pltpu.bitcast(w, BF16)


def _resident(shape):
    return pl.BlockSpec(shape, lambda *_: (0,) * len(shape), pipeline_mode=pl.Buffered(1))


@functools.lru_cache(maxsize=None)
def _rope_tables(s, dk):
    inv = ROPE_THETA ** (-np.arange(0, dk, 2, dtype=np.float64) / dk)
    ang = np.arange(s, dtype=np.float64)[:, None] * inv[None, :]
    return np.cos(ang).astype(np.float32), np.sin(ang).astype(np.float32)


@functools.lru_cache(maxsize=None)
def _chan_dft(n):
    ang = 2.0 * np.pi * np.outer(np.arange(n), np.arange(n)) / n
    return np.concatenate([np.cos(ang), -np.sin(ang)], axis=1).astype(np.float32)


@functools.lru_cache(maxsize=None)
def _fft_a_table(n1, n2):
    n = n1 * n2
    k1 = np.arange(n1, dtype=np.float64)[None, :, None]
    tok = (n2 * np.arange(n1, dtype=np.float64)[None, None, :]
           + np.arange(n2, dtype=np.float64)[:, None, None])
    ang = 2.0 * np.pi * np.mod(k1 * tok, n) / n
    c, s = np.cos(ang), np.sin(ang)
    re_rows = np.concatenate([c, s], axis=2)
    im_rows = np.concatenate([-s, c], axis=2)
    return np.stack([re_rows, im_rows], axis=2).reshape(n2, 2 * n1, 2 * n1).astype(np.float32)


@functools.lru_cache(maxsize=None)
def _fft_b_table(n2, scale):
    ang = 2.0 * np.pi * np.outer(np.arange(n2), np.arange(n2)) / n2
    return (scale * np.stack([np.cos(ang), np.sin(ang)], axis=-1).reshape(n2, 2 * n2)).astype(np.float32)


def _mem_kv_kernel(mem_ref, nw_ref, wk_ref, wv_ref, wq_ref, wo_ref, wqk_ref, wvo_ref):
    m, d = mem_ref.shape[1], mem_ref.shape[2]
    dh = d // CA_HEADS
    mn = _rms(mem_ref[0], nw_ref[...]).astype(BF16)
    ck = _dot(mn, wk_ref[...]).astype(BF16)
    cv = _dot(mn, wv_ref[...]).astype(BF16)
    for h in range(CA_HEADS):
        cols = slice(h * dh, (h + 1) * dh)
        qk = lax.dot_general(wq_ref[:, cols], ck[:, cols], (((1,), (1,)), ((), ())),
                             preferred_element_type=F32) * (dh ** -0.5)
        wqk_ref[0, :, h * m:(h + 1) * m] = qk.astype(BF16)
        wvo_ref[0, h * m:(h + 1) * m, :] = _dot(cv[:, cols], wo_ref[cols, :]).astype(BF16)


def _mem_kv(mem, norm_w, w_ck, w_cv, w_cq, w_co):
    b, m, d = mem.shape
    hm = CA_HEADS * m
    sq = _resident((d, d))
    return pl.pallas_call(
        _mem_kv_kernel,
        out_shape=(jax.ShapeDtypeStruct((b, d, hm), BF16), jax.ShapeDtypeStruct((b, hm, d), BF16)),
        grid=(b,),
        in_specs=[pl.BlockSpec((1, m, d), lambda i: (i, 0, 0)), _resident((1, d)), sq, sq, sq, sq],
        out_specs=(pl.BlockSpec((1, d, hm), lambda i: (i, 0, 0)),
                   pl.BlockSpec((1, hm, d), lambda i: (i, 0, 0))),
        compiler_params=pltpu.CompilerParams(
            dimension_semantics=("arbitrary",),
            vmem_limit_bytes=_vmem_limit(4 * d * d * 2 + 4 * m * d * 4 + 4 * d * hm * 2
                                         + 2 * d * hm * 4)),
        name="mem_kv",
    )(mem, norm_w, w_ck, w_cv, w_cq, w_co)


def _log_sigmoid(x):
    return jnp.minimum(x, 0.0) - jnp.log1p(jnp.exp(-jnp.abs(x)))


def _tn_dot(a, b):
    return lax.dot_general(a, b, (((0,), (0,)), ((), ())), preferred_element_type=F32)


def _decay_rows(dec, c, dk, offset, sign):
    lg = _log_sigmoid(jnp.full((c, dk), dec, F32))
    row = lax.broadcasted_iota(jnp.int32, (c, dk), 0).astype(F32)
    return jnp.exp(lg * (offset + sign * row))


def _chunk_decay(dec, c, dk):
    return jnp.exp(_log_sigmoid(jnp.full((1, dk), dec, F32)) * c)


def _ret_bwd_init(dec_ref, state_ref, zeta_ref, gc_ref):
    c, dk = zeta_ref.shape[1], zeta_ref.shape[2]

    @pl.when(pl.program_id(1) == 0)
    def _init():
        state_ref[...] = jnp.zeros_like(state_ref)
        for h in range(RET_HEADS):
            zeta_ref[h] = _decay_rows(dec_ref[h], c, dk, 0.0, 1.0).astype(BF16)
            gc_ref[h] = _chunk_decay(dec_ref[h], c, dk)


def _ret_bwd_states(k_ref, v_ref, sb_ref, state_ref, zeta_ref, gc_ref):
    c = RET_CHUNK
    t_len, d = k_ref.shape[1], k_ref.shape[2]
    dk = d // RET_HEADS
    for h in range(RET_HEADS):
        cols = slice(h * dk, (h + 1) * dk)
        state = state_ref[h]
        for ci in reversed(range(t_len // c)):
            rows = slice(ci * c, (ci + 1) * c)
            sb_ref[0, ci, h] = state.astype(BF16)
            state = state * gc_ref[h] + _tn_dot(k_ref[0, rows, cols],
                                                v_ref[0, rows, cols] * zeta_ref[h])
        state_ref[h] = state


def _in_proj_kernel(dec_ref, x_ref, nw_ref, w_ref, cos_ref, sin_ref,
                    q_ref, k_ref, v_ref, g_ref, z_ref, gr_ref, gf_ref, sb_ref,
                    perm_ref, state_ref, zeta_ref, gc_ref):
    _ret_bwd_init(dec_ref, state_ref, zeta_ref, gc_ref)
    d = x_ref.shape[-1]
    dk = d // RET_HEADS
    half = dk // 2
    xb = _rms(x_ref[0], nw_ref[...]).astype(BF16)
    cos = cos_ref[...]
    sin = sin_ref[...]

    def proj(j):
        return _dot(xb, w_ref[:, j * d:(j + 1) * d])

    def rope(p, out_ref, scale):
        for h in range(RET_HEADS):
            x1 = p[:, h * dk:h * dk + half]
            x2 = p[:, h * dk + half:(h + 1) * dk]
            o1 = x1 * cos - x2 * sin
            o2 = x2 * cos + x1 * sin
            if scale is not None:
                o1 = o1 * scale
                o2 = o2 * scale
            out_ref[0, :, h * dk:h * dk + half] = o1.astype(BF16)
            out_ref[0, :, h * dk + half:(h + 1) * dk] = o2.astype(BF16)

    rope(proj(0), q_ref, dk ** -0.5)
    rope(proj(1), k_ref, None)
    v_ref[0] = proj(2).astype(BF16)
    _ret_bwd_states(k_ref, v_ref, sb_ref, state_ref, zeta_ref, gc_ref)
    g_ref[0] = proj(3).astype(BF16)
    u = proj(4)
    for lc in range(d // V7X_LANES):
        for blk in range(u.shape[0] // FFT_N2):
            a, par = divmod(blk, 2)
            perm_ref[lc, pl.ds(2 * FFT_N2 * a + par, FFT_N2, stride=2), :] = (
                u[blk * FFT_N2:(blk + 1) * FFT_N2, lc * V7X_LANES:(lc + 1) * V7X_LANES])
    for lc in range(d // V7X_LANES):
        z_ref[0, :, lc * V7X_LANES:(lc + 1) * V7X_LANES] = _words(perm_ref[lc])
    gr_ref[0] = proj(5).astype(BF16)
    gf_ref[0] = proj(6).astype(BF16)


def _in_proj(x, norm_w, w_in, cos, sin, decay_bwd, tm):
    b, s, d = x.shape
    dk = d // RET_HEADS
    half = dk // 2
    c = RET_CHUNK
    nt = s // tm
    act = jax.ShapeDtypeStruct((b, s, d), BF16)
    assert tm % (2 * FFT_N2) == 0 and tm % c == 0
    zsh = jax.ShapeDtypeStruct((b, s // 2, d), jnp.int32)
    sbsh = jax.ShapeDtypeStruct((b, s // c, RET_HEADS, dk, dk), BF16)
    row = pl.BlockSpec((1, tm, d), lambda i, j: (i, nt - 1 - j, 0))
    zrow = pl.BlockSpec((1, tm // 2, d), lambda i, j: (i, nt - 1 - j, 0))
    sbrow = pl.BlockSpec((1, tm // c, RET_HEADS, dk, dk), lambda i, j: (i, nt - 1 - j, 0, 0, 0))
    tab = pl.BlockSpec((tm, half), lambda i, j: (nt - 1 - j, 0))
    vmem = (w_in.size * 2 + 2 * tm * d * 4 + 2 * 8 * tm * d * 2 + 4 * tm * half * 4
            + 5 * tm * d * 4 + 2 * tm * d * 2 + RET_HEADS * (dk * dk * 4 + c * dk * 2))
    return pl.pallas_call(
        _in_proj_kernel,
        out_shape=(act, act, act, act, zsh, act, act, sbsh),
        grid=(b, nt),
        in_specs=[pl.BlockSpec(memory_space=pltpu.SMEM), row, _resident((1, d)),
                  _resident(w_in.shape), tab, tab],
        out_specs=(row, row, row, row, zrow, row, row, sbrow),
        scratch_shapes=[pltpu.VMEM((d // V7X_LANES, tm, V7X_LANES), F32),
                        pltpu.VMEM((RET_HEADS, dk, dk), F32),
                        pltpu.VMEM((RET_HEADS, c, dk), BF16),
                        pltpu.VMEM((RET_HEADS, 1, dk), F32)],
        compiler_params=pltpu.CompilerParams(
            dimension_semantics=("arbitrary", "arbitrary"),
            vmem_limit_bytes=_vmem_limit(vmem)),
        name="in_proj",
    )(decay_bwd, x, norm_w, w_in, cos, sin)


def _ret_fwd_kernel(decf_ref, decb_ref, q_ref, k_ref, v_ref, g_ref, sb_ref, gnw_ref, r_ref,
                    state_ref, dmat_ref, xif_ref, xib_ref, zeta_ref, gc_ref):
    c = RET_CHUNK
    t_len, d = q_ref.shape[1], q_ref.shape[2]
    dk = d // RET_HEADS

    @pl.when(pl.program_id(1) == 0)
    def _init():
        state_ref[...] = jnp.zeros_like(state_ref)
        for h in range(RET_HEADS):
            xif_ref[h] = _decay_rows(decf_ref[h], c, dk, 1.0, 1.0)
            xib_ref[h] = _decay_rows(decb_ref[h], c, dk, float(c), -1.0)
            zeta_ref[h] = _decay_rows(decf_ref[h], c, dk, c - 1.0, -1.0).astype(BF16)
            gc_ref[h] = _chunk_decay(decf_ref[h], c, dk)
            i = lax.broadcasted_iota(jnp.int32, (c, c), 0)
            j = lax.broadcasted_iota(jnp.int32, (c, c), 1)
            diff = (i - j).astype(F32)
            lgf = _log_sigmoid(jnp.full((c, c), decf_ref[h], F32))
            lgb = _log_sigmoid(jnp.full((c, c), decb_ref[h], F32))
            dmat_ref[h] = jnp.where(i >= j, jnp.exp(lgf * jnp.maximum(diff, 0.0)),
                                    jnp.exp(lgb * jnp.maximum(-diff, 0.0)))

    for h in range(RET_HEADS):
        cols = slice(h * dk, (h + 1) * dk)
        state = state_ref[h]
        gnw = gnw_ref[:, cols]
        for ci in range(t_len // c):
            rows = slice(ci * c, (ci + 1) * c)
            q = q_ref[0, rows, cols]
            k = k_ref[0, rows, cols]
            v = v_ref[0, rows, cols]
            scores = lax.dot_general(q, k, (((1,), (1,)), ((), ())), preferred_element_type=F32)
            y = (_dot((scores * dmat_ref[h]).astype(BF16), v)
                 + _dot(q, state.astype(BF16)) * xif_ref[h]
                 + _dot(q, sb_ref[0, ci, h]) * xib_ref[h])
            state = state * gc_ref[h] + _tn_dot(k, v * zeta_ref[h])
            yc = y - jnp.mean(y, axis=-1, keepdims=True)
            var = jnp.mean(yc * yc, axis=-1, keepdims=True)
            g = g_ref[0, rows, cols].astype(F32)
            r_ref[0, rows, cols] = (g * jax.nn.sigmoid(g)
                                    * (yc * lax.rsqrt(var + GN_EPS) * gnw)).astype(BF16)
        state_ref[h] = state


def _fft_a_kernel(z_ref, dft_ref, tab_ref, o_ref):
    rn, d = z_ref.shape[2], z_ref.shape[3]
    gd = dft_ref.shape[0]
    u = jnp.concatenate([_rows(z_ref[0, :, r, :]) for r in range(rn)], axis=0)
    n1 = u.shape[0] // rn
    parts = [_dot(u[:, gi * gd:(gi + 1) * gd], dft_ref[...]) for gi in range(d // gd)]
    re = jnp.concatenate([p[:, :gd] for p in parts], axis=1).astype(BF16)
    im = jnp.concatenate([p[:, gd:] for p in parts], axis=1).astype(BF16)
    for r in range(rn):
        rows = slice(r * n1, (r + 1) * n1)
        zz = jnp.concatenate([re[rows], im[rows]], axis=0)
        o_ref[0, :, r, :] = _words(_dot(tab_ref[r], zz))


def _fft_b_kernel(zt_ref, tab_ref, o_ref):
    for i in range(zt_ref.shape[1]):
        o_ref[0, :, i, :] = _words(_dot(tab_ref[...], _rows(zt_ref[0, i])))


def _ret_fwd_fft_b_kernel(decf_ref, decb_ref, q_ref, k_ref, v_ref, g_ref, sb_ref, gnw_ref,
                          zt_ref, tab_ref, r_ref, f_ref,
                          state_ref, dmat_ref, xif_ref, xib_ref, zeta_ref, gc_ref):
    _ret_fwd_kernel(decf_ref, decb_ref, q_ref, k_ref, v_ref, g_ref, sb_ref, gnw_ref, r_ref,
                    state_ref, dmat_ref, xif_ref, xib_ref, zeta_ref, gc_ref)
    _fft_b_kernel(zt_ref, tab_ref, f_ref)


def _mixers(q, k, v, g, sb, z, decay_fwd, decay_bwd, gn_w, dft, tab_a, tab_b):
    b, s, d = q.shape
    dk = d // RET_HEADS
    c = RET_CHUNK
    n2 = tab_b.shape[0]
    n1 = s // n2
    steps_b = n1 // SUBLANES
    t_b = s // steps_b
    assert t_b % c == 0
    smem = pl.BlockSpec(memory_space=pltpu.SMEM)
    fwd = pl.BlockSpec((1, t_b, d), lambda i, t: (i, t, 0))
    xi = pltpu.VMEM((RET_HEADS, c, dk), F32)
    table_bytes = RET_HEADS * (3 * c * dk + c * c + dk * dk + dk) * 4
    stage_b_bytes = 2 * (SUBLANES * n2 * d * 4 + (n2 // 2) * SUBLANES * d * 4) + 4 * n2 * d * 4
    zt = pl.pallas_call(
        _fft_a_kernel,
        out_shape=jax.ShapeDtypeStruct((b, n1, n2, d), jnp.int32),
        grid=(b, n2 // SUBLANES),
        in_specs=[pl.BlockSpec((1, n1 // 2, SUBLANES, d), lambda i, t: (i, 0, t, 0)),
                  _resident(dft.shape),
                  pl.BlockSpec((SUBLANES, 2 * n1, 2 * n1), lambda i, t: (t, 0, 0))],
        out_specs=pl.BlockSpec((1, n1, SUBLANES, d), lambda i, t: (i, 0, t, 0)),
        compiler_params=pltpu.CompilerParams(
            dimension_semantics=("parallel", "parallel"),
            vmem_limit_bytes=_vmem_limit(2 * (3 * n1 * SUBLANES * d * 2
                                              + SUBLANES * 4 * n1 * n1 * 2)
                                         + 5 * n1 * SUBLANES * d * 4)),
        name="fft_a",
    )(z.reshape(b, n1 // 2, n2, d), dft, tab_a)
    return pl.pallas_call(
        _ret_fwd_fft_b_kernel,
        out_shape=(jax.ShapeDtypeStruct((b, s, d), BF16),
                   jax.ShapeDtypeStruct((b, n2 // 2, n1, d), jnp.int32)),
        grid=(b, steps_b),
        in_specs=[smem, smem, fwd, fwd, fwd, fwd,
                  pl.BlockSpec((1, t_b // c, RET_HEADS, dk, dk), lambda i, t: (i, t, 0, 0, 0)),
                  _resident((1, d)),
                  pl.BlockSpec((1, SUBLANES, n2, d), lambda i, t: (i, t, 0, 0)),
                  _resident((n2, 2 * n2))],
        out_specs=(fwd, pl.BlockSpec((1, n2 // 2, SUBLANES, d), lambda i, t: (i, 0, t, 0))),
        scratch_shapes=[pltpu.VMEM((RET_HEADS, dk, dk), F32), pltpu.VMEM((RET_HEADS, c, c), F32),
                        xi, xi, pltpu.VMEM((RET_HEADS, c, dk), BF16),
                        pltpu.VMEM((RET_HEADS, 1, dk), F32)],
        compiler_params=pltpu.CompilerParams(
            dimension_semantics=("arbitrary", "arbitrary"),
            vmem_limit_bytes=_vmem_limit(2 * 6 * t_b * d * 2 + table_bytes + stage_b_bytes)),
        name="ret_fwd_fft_b",
    )(decay_fwd, decay_bwd, q, k, v, g, sb, gn_w, zt, tab_b)


def _tail_kernel(x_ref, r_ref, f_ref, gr_ref, gf_ref, wqk_ref, wvo_ref,
                 w_ro_ref, w_fo_ref, w_mo_ref, n_ca_ref,
                 n_mlp_ref, w_up_ref, w_dn_ref, n_fin_ref, o_ref, att_ref, perm_ref):
    d = x_ref.shape[-1]
    m = wqk_ref.shape[2] // CA_HEADS
    ff = w_up_ref.shape[1]
    gq, n1 = f_ref.shape[1], f_ref.shape[2]
    x = x_ref[0]
    ret = _dot(r_ref[0], w_ro_ref[...])
    fperm = _dot(_rows(f_ref[0].reshape(gq * n1, d)), w_fo_ref[...])
    for lc in range(d // V7X_LANES):
        perm_ref[lc] = fperm[:, lc * V7X_LANES:(lc + 1) * V7X_LANES]
    fou = jnp.concatenate(
        [jnp.concatenate([perm_ref[lc, pl.ds(blk // 2 * 2 * n1 + blk % 2, n1, stride=2), :]
                          for lc in range(d // V7X_LANES)], axis=1)
         for blk in range(2 * gq)], axis=0)
    merged = (jax.nn.sigmoid(gr_ref[0].astype(F32)) * ret
              + jax.nn.sigmoid(gf_ref[0].astype(F32)) * fou)
    x = x + _dot(merged.astype(BF16), w_mo_ref[...])

    logits = _dot(_rms(x, n_ca_ref[...]).astype(BF16), wqk_ref[0])
    for h in range(CA_HEADS):
        cols = slice(h * m, (h + 1) * m)
        p = jnp.exp(logits[:, cols] - jnp.max(logits[:, cols], axis=-1, keepdims=True))
        att_ref[:, cols] = (p / jnp.sum(p, axis=-1, keepdims=True)).astype(BF16)
    x = x + _dot(att_ref[...], wvo_ref[0])

    xb = _rms(x, n_mlp_ref[...]).astype(BF16)
    acc = x
    for j in range(ff // d):
        cols = slice(j * d, (j + 1) * d)
        hid = jnp.square(jnp.maximum(_dot(xb, w_up_ref[:, cols]), 0.0)).astype(BF16)
        acc = acc + _dot(hid, w_dn_ref[cols, :])
    o_ref[0] = _rms(acc, n_fin_ref[...])


def _tail(x, r, four, gate_r, gate_f, wqk, wvo, w_ro, w_fo, w_mo, n_ca,
          n_mlp, w_up, w_dn, n_fin, tm):
    b, s, d = x.shape
    hm = wqk.shape[2]
    ff = w_up.shape[1]
    row = pl.BlockSpec((1, tm, d), lambda i, j: (i, j, 0))
    n1 = four.shape[2]
    assert tm % (2 * n1) == 0
    half = pl.BlockSpec((1, tm // (2 * n1), n1, d), lambda i, j: (i, j, 0, 0))
    vec = _resident((1, d))
    sq = _resident((d, d))
    vmem = ((3 * d * d + 2 * d * ff) * 2 + 4 * d * hm * 2 + 2 * tm * d * (4 + 4 + 4 * 2)
            + tm * d * 4 * 8)
    return pl.pallas_call(
        _tail_kernel,
        out_shape=jax.ShapeDtypeStruct((b, s, d), F32),
        grid=(b, s // tm),
        in_specs=[row, row, half, row, row,
                  pl.BlockSpec((1, d, hm), lambda i, j: (i, 0, 0)),
                  pl.BlockSpec((1, hm, d), lambda i, j: (i, 0, 0)),
                  sq, sq, sq, vec, vec, _resident((d, ff)), _resident((ff, d)), vec],
        out_specs=row,
        scratch_shapes=[pltpu.VMEM((tm, hm), BF16),
                        pltpu.VMEM((d // V7X_LANES, tm, V7X_LANES), F32)],
        compiler_params=pltpu.CompilerParams(
            dimension_semantics=("parallel", "parallel"),
            vmem_limit_bytes=_vmem_limit(vmem)),
        name="tail",
    )(x, r, four, gate_r, gate_f, wqk, wvo, w_ro, w_fo, w_mo, n_ca, n_mlp, w_up, w_dn, n_fin)


def _tiles(s):
    return min(512, s), min(512, s)


def _trunk(x, mem, p):
    b, s, d = x.shape
    dk = d // RET_HEADS
    gd = d // FOUR_GROUPS
    n1 = s // FFT_N2
    assert s % FFT_N2 == 0 and s % RET_CHUNK == 0
    tm_in, tm_tail = _tiles(s)
    cos, sin = (jnp.asarray(t) for t in _rope_tables(s, dk))
    dft = jnp.asarray(_chan_dft(gd)).astype(BF16)
    tab_a = jnp.asarray(_fft_a_table(n1, FFT_N2)).astype(BF16)
    tab_b = jnp.asarray(_fft_b_table(FFT_N2, float(s * gd) ** -0.5)).astype(BF16)

    for l in range(p["w_in"].shape[0]):
        wqk, wvo = _mem_kv(mem, p["norm_mem_w"][l], p["w_ck"][l], p["w_cv"][l],
                           p["w_cq"][l], p["w_co"][l])
        q, k, v, g, z, gate_r, gate_f, sb = _in_proj(x, p["norm_mix_w"][l], p["w_in"][l], cos, sin,
                                                     p["ret_decay_bwd"][l], tm_in)
        r, four = _mixers(q, k, v, g, sb, z, p["ret_decay_fwd"][l], p["ret_decay_bwd"][l],
                          p["ret_gn_w"][l], dft, tab_a, tab_b)
        last = l == p["w_in"].shape[0] - 1
        n_fin = p["norm_final_w"] if last else None
        assert last, "a single layer is fused with the final norm"
        x = _tail(x, r, four, gate_r, gate_f, wqk, wvo, p["w_ret_out"][l], p["w_four_out"][l],
                  p["w_mix_out"][l], p["norm_ca_w"][l], p["norm_mlp_w"][l], p["w_up"][l],
                  p["w_down"][l], n_fin, tm_tail)
    return x


def kernel(x_prompt, x_sample, mem_prompt, mem_sample, norm_mix_w, w_in, ret_decay_fwd,
           ret_decay_bwd, ret_gn_w, w_ret_out, w_four_out, w_mix_out, norm_ca_w, norm_mem_w,
           w_cq, w_ck, w_cv, w_co, norm_mlp_w, w_up, w_down, norm_final_w):
    depth, d = norm_mix_w.shape
    vec = lambda w: w.reshape(depth, 1, d)
    mat = lambda w: w.astype(BF16)
    p = dict(
        norm_mix_w=vec(norm_mix_w), w_in=mat(w_in), ret_decay_fwd=ret_decay_fwd,
        ret_decay_bwd=ret_decay_bwd, ret_gn_w=vec(ret_gn_w), w_ret_out=mat(w_ret_out),
        w_four_out=mat(w_four_out), w_mix_out=mat(w_mix_out), norm_ca_w=vec(norm_ca_w),
        norm_mem_w=vec(norm_mem_w), w_cq=mat(w_cq), w_ck=mat(w_ck), w_cv=mat(w_cv),
        w_co=mat(w_co), norm_mlp_w=vec(norm_mlp_w), w_up=mat(w_up), w_down=mat(w_down),
        norm_final_w=norm_final_w.reshape(1, d))
    return _trunk(x_prompt, mem_prompt, p), _trunk(x_sample, mem_sample, p)
```

```python
import functools

import numpy as np
import jax
import jax.numpy as jnp
from jax import lax
from jax.experimental import pallas as pl
from jax.experimental.pallas import tpu as pltpu

RET_HEADS = 4
FOUR_GROUPS = 4
CA_HEADS = 4
ROPE_THETA = 10000.0
EPS = 1e-6
GN_EPS = 1e-5

V7X_VMEM_BYTES = 64 * 1024 * 1024
V7X_MXU_DIM = 256
V7X_LANES = 128
SUBLANES = 8

RET_CHUNK = V7X_MXU_DIM
FFT_N2 = 128

BF16 = jnp.bfloat16
F32 = jnp.float32


def _vmem_limit(nbytes):
    return int(min(nbytes + 8 * 1024 * 1024, V7X_VMEM_BYTES - 6 * 1024 * 1024))


def _rms(x, w):
    return x * lax.rsqrt(jnp.mean(x * x, axis=-1, keepdims=True) + EPS) * w


def _dot(a, b):
    return jnp.dot(a, b, preferred_element_type=F32)


def _words(x):
    return pltpu.bitcast(x.astype(BF16), jnp.int32)


def _rows(w):
    return pltpu.bitcast(w, BF16)


def _resident(shape):
    return pl.BlockSpec(shape, lambda *_: (0,) * len(shape), pipeline_mode=pl.Buffered(1))


@functools.lru_cache(maxsize=None)
def _rope_tables(s, dk):
    inv = ROPE_THETA ** (-np.arange(0, dk, 2, dtype=np.float64) / dk)
    ang = np.arange(s, dtype=np.float64)[:, None] * inv[None, :]
    return np.cos(ang).astype(np.float32), np.sin(ang).astype(np.float32)


@functools.lru_cache(maxsize=None)
def _chan_dft(n):
    ang = 2.0 * np.pi * np.outer(np.arange(n), np.arange(n)) / n
    return np.concatenate([np.cos(ang), -np.sin(ang)], axis=1).astype(np.float32)


@functools.lru_cache(maxsize=None)
def _fft_a_table(n1, n2):
    n = n1 * n2
    k1 = np.arange(n1, dtype=np.float64)[None, :, None]
    tok = (n2 * np.arange(n1, dtype=np.float64)[None, None, :]
           + np.arange(n2, dtype=np.float64)[:, None, None])
    ang = 2.0 * np.pi * np.mod(k1 * tok, n) / n
    c, s = np.cos(ang), np.sin(ang)
    re_rows = np.concatenate([c, s], axis=2)
    im_rows = np.concatenate([-s, c], axis=2)
    return np.stack([re_rows, im_rows], axis=2).reshape(n2, 2 * n1, 2 * n1).astype(np.float32)


@functools.lru_cache(maxsize=None)
def _fft_b_table(n2, scale):
    ang = 2.0 * np.pi * np.outer(np.arange(n2), np.arange(n2)) / n2
    return (scale * np.stack([np.cos(ang), np.sin(ang)], axis=-1).reshape(n2, 2 * n2)).astype(np.float32)


def _mem_kv_kernel(mem_ref, nw_ref, wk_ref, wv_ref, wq_ref, wo_ref, wqk_ref, wvo_ref):
    m, d = mem_ref.shape[1], mem_ref.shape[2]
    dh = d // CA_HEADS
    mn = _rms(mem_ref[0], nw_ref[...]).astype(BF16)
    ck = _dot(mn, wk_ref[...]).astype(BF16)
    cv = _dot(mn, wv_ref[...]).astype(BF16)
    for h in range(CA_HEADS):
        cols = slice(h * dh, (h + 1) * dh)
        qk = lax.dot_general(wq_ref[:, cols], ck[:, cols], (((1,), (1,)), ((), ())),
                             preferred_element_type=F32) * (dh ** -0.5)
        wqk_ref[0, :, h * m:(h + 1) * m] = qk.astype(BF16)
        wvo_ref[0, h * m:(h + 1) * m, :] = _dot(cv[:, cols], wo_ref[cols, :]).astype(BF16)


def _mem_kv(mem, norm_w, w_ck, w_cv, w_cq, w_co):
    b, m, d = mem.shape
    hm = CA_HEADS * m
    sq = _resident((d, d))
    return pl.pallas_call(
        _mem_kv_kernel,
        out_shape=(jax.ShapeDtypeStruct((b, d, hm), BF16), jax.ShapeDtypeStruct((b, hm, d), BF16)),
        grid=(b,),
        in_specs=[pl.BlockSpec((1, m, d), lambda i: (i, 0, 0)), _resident((1, d)), sq, sq, sq, sq],
        out_specs=(pl.BlockSpec((1, d, hm), lambda i: (i, 0, 0)),
                   pl.BlockSpec((1, hm, d), lambda i: (i, 0, 0))),
        compiler_params=pltpu.CompilerParams(
            dimension_semantics=("arbitrary",),
            vmem_limit_bytes=_vmem_limit(4 * d * d * 2 + 4 * m * d * 4 + 4 * d * hm * 2
                                         + 2 * d * hm * 4)),
        name="mem_kv",
    )(mem, norm_w, w_ck, w_cv, w_cq, w_co)


def _log_sigmoid(x):
    return jnp.minimum(x, 0.0) - jnp.log1p(jnp.exp(-jnp.abs(x)))


def _tn_dot(a, b):
    return lax.dot_general(a, b, (((0,), (0,)), ((), ())), preferred_element_type=F32)


def _decay_rows(dec, c, dk, offset, sign):
    lg = _log_sigmoid(jnp.full((c, dk), dec, F32))
    row = lax.broadcasted_iota(jnp.int32, (c, dk), 0).astype(F32)
    return jnp.exp(lg * (offset + sign * row))


def _chunk_decay(dec, c, dk):
    return jnp.exp(_log_sigmoid(jnp.full((1, dk), dec, F32)) * c)


def _ret_bwd_init(dec_ref, state_ref, zeta_ref, gc_ref):
    c, dk = zeta_ref.shape[1], zeta_ref.shape[2]

    @pl.when(pl.program_id(1) == 0)
    def _init():
        state_ref[...] = jnp.zeros_like(state_ref)
        for h in range(RET_HEADS):
            zeta_ref[h] = _decay_rows(dec_ref[h], c, dk, 0.0, 1.0).astype(BF16)
            gc_ref[h] = _chunk_decay(dec_ref[h], c, dk)


def _ret_bwd_states(k_ref, v_ref, sb_ref, state_ref, zeta_ref, gc_ref):
    c = RET_CHUNK
    t_len, d = k_ref.shape[1], k_ref.shape[2]
    dk = d // RET_HEADS
    for h in range(RET_HEADS):
        cols = slice(h * dk, (h + 1) * dk)
        state = state_ref[h]
        for ci in reversed(range(t_len // c)):
            rows = slice(ci * c, (ci + 1) * c)
            sb_ref[0, ci, h] = state.astype(BF16)
            state = state * gc_ref[h] + _tn_dot(k_ref[0, rows, cols],
                                                v_ref[0, rows, cols] * zeta_ref[h])
        state_ref[h] = state


def _in_proj_kernel(dec_ref, x_ref, nw_ref, w_ref, cos_ref, sin_ref,
                    q_ref, k_ref, v_ref, g_ref, z_ref, gr_ref, gf_ref, sb_ref,
                    perm_ref, state_ref, zeta_ref, gc_ref):
    _ret_bwd_init(dec_ref, state_ref, zeta_ref, gc_ref)
    d = x_ref.shape[-1]
    dk = d // RET_HEADS
    half = dk // 2
    xb = _rms(x_ref[0], nw_ref[...]).astype(BF16)
    cos = cos_ref[...]
    sin = sin_ref[...]

    def proj(j):
        return _dot(xb, w_ref[:, j * d:(j + 1) * d])

    def rope(p, out_ref, scale):
        for h in range(RET_HEADS):
            x1 = p[:, h * dk:h * dk + half]
            x2 = p[:, h * dk + half:(h + 1) * dk]
            o1 = x1 * cos - x2 * sin
            o2 = x2 * cos + x1 * sin
            if scale is not None:
                o1 = o1 * scale
                o2 = o2 * scale
            out_ref[0, :, h * dk:h * dk + half] = o1.astype(BF16)
            out_ref[0, :, h * dk + half:(h + 1) * dk] = o2.astype(BF16)

    rope(proj(0), q_ref, dk ** -0.5)
    rope(proj(1), k_ref, None)
    v_ref[0] = proj(2).astype(BF16)
    _ret_bwd_states(k_ref, v_ref, sb_ref, state_ref, zeta_ref, gc_ref)
    g_ref[0] = proj(3).astype(BF16)
    u = proj(4)
    for lc in range(d // V7X_LANES):
        for blk in range(u.shape[0] // FFT_N2):
            a, par = divmod(blk, 2)
            perm_ref[lc, pl.ds(2 * FFT_N2 * a + par, FFT_N2, stride=2), :] = (
                u[blk * FFT_N2:(blk + 1) * FFT_N2, lc * V7X_LANES:(lc + 1) * V7X_LANES])
    for lc in range(d // V7X_LANES):
        z_ref[0, :, lc * V7X_LANES:(lc + 1) * V7X_LANES] = _words(perm_ref[lc])
    gr_ref[0] = proj(5).astype(BF16)
    gf_ref[0] = proj(6).astype(BF16)


def _in_proj(x, norm_w, w_in, cos, sin, decay_bwd, tm):
    b, s, d = x.shape
    dk = d // RET_HEADS
    half = dk // 2
    c = RET_CHUNK
    nt = s // tm
    act = jax.ShapeDtypeStruct((b, s, d), BF16)
    assert tm % (2 * FFT_N2) == 0 and tm % c == 0
    zsh = jax.ShapeDtypeStruct((b, s // 2, d), jnp.int32)
    sbsh = jax.ShapeDtypeStruct((b, s // c, RET_HEADS, dk, dk), BF16)
    row = pl.BlockSpec((1, tm, d), lambda i, j: (i, nt - 1 - j, 0))
    zrow = pl.BlockSpec((1, tm // 2, d), lambda i, j: (i, nt - 1 - j, 0))
    sbrow = pl.BlockSpec((1, tm // c, RET_HEADS, dk, dk), lambda i, j: (i, nt - 1 - j, 0, 0, 0))
    tab = pl.BlockSpec((tm, half), lambda i, j: (nt - 1 - j, 0))
    vmem = (w_in.size * 2 + 2 * tm * d * 4 + 2 * 8 * tm * d * 2 + 4 * tm * half * 4
            + 5 * tm * d * 4 + 2 * tm * d * 2 + RET_HEADS * (dk * dk * 4 + c * dk * 2))
    return pl.pallas_call(
        _in_proj_kernel,
        out_shape=(act, act, act, act, zsh, act, act, sbsh),
        grid=(b, nt),
        in_specs=[pl.BlockSpec(memory_space=pltpu.SMEM), row, _resident((1, d)),
                  _resident(w_in.shape), tab, tab],
        out_specs=(row, row, row, row, zrow, row, row, sbrow),
        scratch_shapes=[pltpu.VMEM((d // V7X_LANES, tm, V7X_LANES), F32),
                        pltpu.VMEM((RET_HEADS, dk, dk), F32),
                        pltpu.VMEM((RET_HEADS, c, dk), BF16),
                        pltpu.VMEM((RET_HEADS, 1, dk), F32)],
        compiler_params=pltpu.CompilerParams(
            dimension_semantics=("arbitrary", "arbitrary"),
            vmem_limit_bytes=_vmem_limit(vmem)),
        name="in_proj",
    )(decay_bwd, x, norm_w, w_in, cos, sin)


def _ret_fwd_kernel(decf_ref, decb_ref, q_ref, k_ref, v_ref, g_ref, sb_ref, r_ref,
                    state_ref, dmat_ref, xif_ref, xib_ref, zeta_ref, gc_ref):
    c = RET_CHUNK
    t_len, d = q_ref.shape[1], q_ref.shape[2]
    dk = d // RET_HEADS

    @pl.when(pl.program_id(1) == 0)
    def _init():
        state_ref[...] = jnp.zeros_like(state_ref)
        for h in range(RET_HEADS):
            xif_ref[h] = _decay_rows(decf_ref[h], c, dk, 1.0, 1.0).astype(BF16)
            xib_ref[h] = _decay_rows(decb_ref[h], c, dk, float(c), -1.0).astype(BF16)
            zeta_ref[h] = _decay_rows(decf_ref[h], c, dk, c - 1.0, -1.0).astype(BF16)
            gc_ref[h] = _chunk_decay(decf_ref[h], c, dk)
            i = lax.broadcasted_iota(jnp.int32, (c, c), 0)
            j = lax.broadcasted_iota(jnp.int32, (c, c), 1)
            diff = (i - j).astype(F32)
            lgf = _log_sigmoid(jnp.full((c, c), decf_ref[h], F32))
            lgb = _log_sigmoid(jnp.full((c, c), decb_ref[h], F32))
            dmat_ref[h] = jnp.where(i >= j, jnp.exp(lgf * jnp.maximum(diff, 0.0)),
                                    jnp.exp(lgb * jnp.maximum(-diff, 0.0)))

    for h in range(RET_HEADS):
        cols = slice(h * dk, (h + 1) * dk)
        state = state_ref[h]
        for ci in range(t_len // c):
            rows = slice(ci * c, (ci + 1) * c)
            q = q_ref[0, rows, cols]
            k = k_ref[0, rows, cols]
            v = v_ref[0, rows, cols]
            scores = lax.dot_general(q, k, (((1,), (1,)), ((), ())), preferred_element_type=F32)
            lhs = jnp.concatenate([(scores * dmat_ref[h]).astype(BF16),
                                   q * xif_ref[h], q * xib_ref[h]], axis=1)
            rhs = jnp.concatenate([v, state.astype(BF16), sb_ref[0, ci, h]], axis=0)
            y = _dot(lhs, rhs)
            state = state * gc_ref[h] + _tn_dot(k, v * zeta_ref[h])
            yc = y - jnp.mean(y, axis=-1, keepdims=True)
            var = jnp.mean(yc * yc, axis=-1, keepdims=True)
            g = g_ref[0, rows, cols]
            r_ref[0, rows, cols] = (g * jax.nn.sigmoid(g)) * (yc * lax.rsqrt(var + GN_EPS)).astype(BF16)
        state_ref[h] = state


def _fft_a_kernel(z_ref, dft_ref, tab_ref, o_ref):
    rn, d = z_ref.shape[2], z_ref.shape[3]
    gd = dft_ref.shape[0]
    u = jnp.concatenate([_rows(z_ref[0, :, r, :]) for r in range(rn)], axis=0)
    n1 = u.shape[0] // rn
    parts = [_dot(u[:, gi * gd:(gi + 1) * gd], dft_ref[...]) for gi in range(d // gd)]
    re = jnp.concatenate([p[:, :gd] for p in parts], axis=1).astype(BF16)
    im = jnp.concatenate([p[:, gd:] for p in parts], axis=1).astype(BF16)
    for r in range(rn):
        rows = slice(r * n1, (r + 1) * n1)
        zz = jnp.concatenate([re[rows], im[rows]], axis=0)
        o_ref[0, :, r, :] = _words(_dot(tab_ref[r], zz))


def _fft_b_kernel(zt_ref, tab_ref, o_ref):
    for i in range(zt_ref.shape[1]):
        o_ref[0, :, i, :] = _words(_dot(tab_ref[...], _rows(zt_ref[0, i])))


def _ret_fwd_fft_b_kernel(decf_ref, decb_ref, q_ref, k_ref, v_ref, g_ref, sb_ref,
                          zt_ref, tab_ref, r_ref, f_ref,
                          state_ref, dmat_ref, xif_ref, xib_ref, zeta_ref, gc_ref):
    _ret_fwd_kernel(decf_ref, decb_ref, q_ref, k_ref, v_ref, g_ref, sb_ref, r_ref,
                    state_ref, dmat_ref, xif_ref, xib_ref, zeta_ref, gc_ref)
    _fft_b_kernel(zt_ref, tab_ref, f_ref)


def _mixers(q, k, v, g, sb, z, decay_fwd, decay_bwd, dft, tab_a, tab_b):
    b, s, d = q.shape
    dk = d // RET_HEADS
    c = RET_CHUNK
    n2 = tab_b.shape[0]
    n1 = s // n2
    steps_b = n1 // SUBLANES
    t_b = s // steps_b
    assert t_b % c == 0
    smem = pl.BlockSpec(memory_space=pltpu.SMEM)
    fwd = pl.BlockSpec((1, t_b, d), lambda i, t: (i, t, 0))
    xi = pltpu.VMEM((RET_HEADS, c, dk), BF16)
    table_bytes = RET_HEADS * (3 * c * dk + c * c + dk * dk + dk) * 4
    stage_b_bytes = 2 * (SUBLANES * n2 * d * 4 + (n2 // 2) * SUBLANES * d * 4) + 4 * n2 * d * 4
    zt = pl.pallas_call(
        _fft_a_kernel,
        out_shape=jax.ShapeDtypeStruct((b, n1, n2, d), jnp.int32),
        grid=(b, n2 // SUBLANES),
        in_specs=[pl.BlockSpec((1, n1 // 2, SUBLANES, d), lambda i, t: (i, 0, t, 0)),
                  _resident(dft.shape),
                  pl.BlockSpec((SUBLANES, 2 * n1, 2 * n1), lambda i, t: (t, 0, 0))],
        out_specs=pl.BlockSpec((1, n1, SUBLANES, d), lambda i, t: (i, 0, t, 0)),
        compiler_params=pltpu.CompilerParams(
            dimension_semantics=("parallel", "parallel"),
            vmem_limit_bytes=_vmem_limit(2 * (3 * n1 * SUBLANES * d * 2
                                              + SUBLANES * 4 * n1 * n1 * 2)
                                         + 5 * n1 * SUBLANES * d * 4)),
        name="fft_a",
    )(z.reshape(b, n1 // 2, n2, d), dft, tab_a)
    return pl.pallas_call(
        _ret_fwd_fft_b_kernel,
        out_shape=(jax.ShapeDtypeStruct((b, s, d), BF16),
                   jax.ShapeDtypeStruct((b, n2 // 2, n1, d), jnp.int32)),
        grid=(b, steps_b),
        in_specs=[smem, smem, fwd, fwd, fwd, fwd,
                  pl.BlockSpec((1, t_b // c, RET_HEADS, dk, dk), lambda i, t: (i, t, 0, 0, 0)),
                  pl.BlockSpec((1, SUBLANES, n2, d), lambda i, t: (i, t, 0, 0)),
                  _resident((n2, 2 * n2))],
        out_specs=(fwd, pl.BlockSpec((1, n2 // 2, SUBLANES, d), lambda i, t: (i, 0, t, 0))),
        scratch_shapes=[pltpu.VMEM((RET_HEADS, dk, dk), F32), pltpu.VMEM((RET_HEADS, c, c), F32),
                        xi, xi, pltpu.VMEM((RET_HEADS, c, dk), BF16),
                        pltpu.VMEM((RET_HEADS, 1, dk), F32)],
        compiler_params=pltpu.CompilerParams(
            dimension_semantics=("arbitrary", "arbitrary"),
            vmem_limit_bytes=_vmem_limit(2 * 6 * t_b * d * 2 + table_bytes + stage_b_bytes)),
        name="ret_fwd_fft_b",
    )(decay_fwd, decay_bwd, q, k, v, g, sb, zt, tab_b)


def _gain_rows_kernel(w_ref, g_ref, o_ref):
    o_ref[...] = (w_ref[...] * g_ref[...]).astype(BF16)


def _gain_rows(w, gain):
    d_in, d_out = w.shape
    rows = min(d_in, V7X_MXU_DIM)
    return pl.pallas_call(
        _gain_rows_kernel,
        out_shape=jax.ShapeDtypeStruct((d_in, d_out), BF16),
        grid=(d_in // rows,),
        in_specs=[pl.BlockSpec((rows, d_out), lambda i: (i, 0)),
                  pl.BlockSpec((rows, 1), lambda i: (i, 0))],
        out_specs=pl.BlockSpec((rows, d_out), lambda i: (i, 0)),
        compiler_params=pltpu.CompilerParams(dimension_semantics=("parallel",)),
        name="gain_rows",
    )(w, gain.reshape(d_in, 1))


def _tail_kernel(x_ref, r_ref, f_ref, gr_ref, gf_ref, wqk_ref, wvo_ref,
                 w_ro_ref, w_fo_ref, w_mo_ref, n_ca_ref,
                 n_mlp_ref, w_up_ref, w_dn_ref, n_fin_ref, o_ref, att_ref, perm_ref):
    d = x_ref.shape[-1]
    m = wqk_ref.shape[2] // CA_HEADS
    ff = w_up_ref.shape[1]
    gq, n1 = f_ref.shape[1], f_ref.shape[2]
    x = x_ref[0]
    ret = _dot(r_ref[0], w_ro_ref[...])
    fperm = _dot(_rows(f_ref[0].reshape(gq * n1, d)), w_fo_ref[...])
    for lc in range(d // V7X_LANES):
        perm_ref[lc] = fperm[:, lc * V7X_LANES:(lc + 1) * V7X_LANES]
    fou = jnp.concatenate(
        [jnp.concatenate([perm_ref[lc, pl.ds(blk // 2 * 2 * n1 + blk % 2, n1, stride=2), :]
                          for lc in range(d // V7X_LANES)], axis=1)
         for blk in range(2 * gq)], axis=0)
    merged = (jax.nn.sigmoid(gr_ref[0].astype(F32)) * ret
              + jax.nn.sigmoid(gf_ref[0].astype(F32)) * fou)
    x = x + _dot(merged.astype(BF16), w_mo_ref[...])

    logits = _dot(_rms(x, n_ca_ref[...]).astype(BF16), wqk_ref[0])
    for h in range(CA_HEADS):
        cols = slice(h * m, (h + 1) * m)
        p = jnp.exp(logits[:, cols] - jnp.max(logits[:, cols], axis=-1, keepdims=True))
        att_ref[:, cols] = (p / jnp.sum(p, axis=-1, keepdims=True)).astype(BF16)
    x = x + _dot(att_ref[...], wvo_ref[0])

    xb = _rms(x, n_mlp_ref[...]).astype(BF16)
    acc = x
    for j in range(ff // d):
        cols = slice(j * d, (j + 1) * d)
        hid = jnp.square(jnp.maximum(_dot(xb, w_up_ref[:, cols]), 0.0)).astype(BF16)
        acc = acc + _dot(hid, w_dn_ref[cols, :])
    o_ref[0] = _rms(acc, n_fin_ref[...])


def _tail(x, r, four, gate_r, gate_f, wqk, wvo, w_ro, w_fo, w_mo, n_ca,
          n_mlp, w_up, w_dn, n_fin, tm):
    b, s, d = x.shape
    hm = wqk.shape[2]
    ff = w_up.shape[1]
    row = pl.BlockSpec((1, tm, d), lambda i, j: (i, j, 0))
    n1 = four.shape[2]
    assert tm % (2 * n1) == 0
    half = pl.BlockSpec((1, tm // (2 * n1), n1, d), lambda i, j: (i, j, 0, 0))
    vec = _resident((1, d))
    sq = _resident((d, d))
    vmem = ((3 * d * d + 2 * d * ff) * 2 + 4 * d * hm * 2 + 2 * tm * d * (4 + 4 + 4 * 2)
            + tm * d * 4 * 8)
    return pl.pallas_call(
        _tail_kernel,
        out_shape=jax.ShapeDtypeStruct((b, s, d), F32),
        grid=(b, s // tm),
        in_specs=[row, row, half, row, row,
                  pl.BlockSpec((1, d, hm), lambda i, j: (i, 0, 0)),
                  pl.BlockSpec((1, hm, d), lambda i, j: (i, 0, 0)),
                  sq, sq, sq, vec, vec, _resident((d, ff)), _resident((ff, d)), vec],
        out_specs=row,
        scratch_shapes=[pltpu.VMEM((tm, hm), BF16),
                        pltpu.VMEM((d // V7X_LANES, tm, V7X_LANES), F32)],
        compiler_params=pltpu.CompilerParams(
            dimension_semantics=("parallel", "parallel"),
            vmem_limit_bytes=_vmem_limit(vmem)),
        name="tail",
    )(x, r, four, gate_r, gate_f, wqk, wvo, w_ro, w_fo, w_mo, n_ca, n_mlp, w_up, w_dn, n_fin)


def _tiles(s):
    return min(512, s), min(512, s)


def _trunk(x, mem, p):
    b, s, d = x.shape
    dk = d // RET_HEADS
    gd = d // FOUR_GROUPS
    n1 = s // FFT_N2
    assert s % FFT_N2 == 0 and s % RET_CHUNK == 0
    tm_in, tm_tail = _tiles(s)
    cos, sin = (jnp.asarray(t) for t in _rope_tables(s, dk))
    dft = jnp.asarray(_chan_dft(gd)).astype(BF16)
    tab_a = jnp.asarray(_fft_a_table(n1, FFT_N2)).astype(BF16)
    tab_b = jnp.asarray(_fft_b_table(FFT_N2, float(s * gd) ** -0.5)).astype(BF16)

    for l in range(p["w_in"].shape[0]):
        wqk, wvo = _mem_kv(mem, p["norm_mem_w"][l], p["w_ck"][l], p["w_cv"][l],
                           p["w_cq"][l], p["w_co"][l])
        q, k, v, g, z, gate_r, gate_f, sb = _in_proj(x, p["norm_mix_w"][l], p["w_in"][l], cos, sin,
                                                     p["ret_decay_bwd"][l], tm_in)
        r, four = _mixers(q, k, v, g, sb, z, p["ret_decay_fwd"][l], p["ret_decay_bwd"][l],
                          dft, tab_a, tab_b)
        w_ro = _gain_rows(p["w_ret_out"][l], p["ret_gn_w"][l])
        last = l == p["w_in"].shape[0] - 1
        n_fin = p["norm_final_w"] if last else None
        assert last, "a single layer is fused with the final norm"
        x = _tail(x, r, four, gate_r, gate_f, wqk, wvo, w_ro, p["w_four_out"][l],
                  p["w_mix_out"][l], p["norm_ca_w"][l], p["norm_mlp_w"][l], p["w_up"][l],
                  p["w_down"][l], n_fin, tm_tail)
    return x


def kernel(x_prompt, x_sample, mem_prompt, mem_sample, norm_mix_w, w_in, ret_decay_fwd,
           ret_decay_bwd, ret_gn_w, w_ret_out, w_four_out, w_mix_out, norm_ca_w, norm_mem_w,
           w_cq, w_ck, w_cv, w_co, norm_mlp_w, w_up, w_down, norm_final_w):
    depth, d = norm_mix_w.shape
    vec = lambda w: w.reshape(depth, 1, d)
    mat = lambda w: w.astype(BF16)
    p = dict(
        norm_mix_w=vec(norm_mix_w), w_in=mat(w_in), ret_decay_fwd=ret_decay_fwd,
        ret_decay_bwd=ret_decay_bwd, ret_gn_w=ret_gn_w, w_ret_out=w_ret_out,
        w_four_out=mat(w_four_out), w_mix_out=mat(w_mix_out), norm_ca_w=vec(norm_ca_w),
        norm_mem_w=vec(norm_mem_w), w_cq=mat(w_cq), w_ck=mat(w_ck), w_cv=mat(w_cv),
        w_co=mat(w_co), norm_mlp_w=vec(norm_mlp_w), w_up=mat(w_up), w_down=mat(w_down),
        norm_final_w=norm_final_w.reshape(1, d))
    return _trunk(x_prompt, mem_prompt, p), _trunk(x_sample, mem_sample, p)
```

```python
import functools

import numpy as np
import jax
import jax.numpy as jnp
from jax import lax
from jax.experimental import pallas as pl
from jax.experimental.pallas import tpu as pltpu

RET_HEADS = 4
FOUR_GROUPS = 4
CA_HEADS = 4
ROPE_THETA = 10000.0
EPS = 1e-6
GN_EPS = 1e-5

V7X_VMEM_BYTES = 64 * 1024 * 1024
V7X_MXU_DIM = 256
V7X_LANES = 128
SUBLANES = 8

RET_CHUNK = V7X_MXU_DIM
FFT_N2 = 128

BF16 = jnp.bfloat16
F32 = jnp.float32


COMPILER_SCRATCH_BYTES = 8 * 1024 * 1024
VMEM_RESERVE_BYTES = 6 * 1024 * 1024


def _vmem_limit(nbytes):
    return int(min(nbytes + COMPILER_SCRATCH_BYTES, V7X_VMEM_BYTES - VMEM_RESERVE_BYTES))


def _rms(x, w):
    return x * lax.rsqrt(jnp.mean(x * x, axis=-1, keepdims=True) + EPS) * w


def _dot(a, b):
    return jnp.dot(a, b, preferred_element_type=F32)


def _words(x):
    return pltpu.bitcast(x.astype(BF16), jnp.int32)


def _rows(w):
    return pltpu.bitcast(w, BF16)


def _resident(shape):
    return pl.BlockSpec(shape, lambda *_: (0,) * len(shape), pipeline_mode=pl.Buffered(1))


@functools.lru_cache(maxsize=None)
def _rope_tables(s, dk):
    inv = ROPE_THETA ** (-np.arange(0, dk, 2, dtype=np.float64) / dk)
    ang = np.arange(s, dtype=np.float64)[:, None] * inv[None, :]
    return np.cos(ang).astype(np.float32), np.sin(ang).astype(np.float32)


@functools.lru_cache(maxsize=None)
def _chan_dft(n):
    ang = 2.0 * np.pi * np.outer(np.arange(n), np.arange(n)) / n
    return np.concatenate([np.cos(ang), -np.sin(ang)], axis=1).astype(np.float32)


@functools.lru_cache(maxsize=None)
def _fft_a_table(n1, n2):
    n = n1 * n2
    k1 = np.arange(n1, dtype=np.float64)[None, :, None]
    tok = (n2 * np.arange(n1, dtype=np.float64)[None, None, :]
           + np.arange(n2, dtype=np.float64)[:, None, None])
    ang = 2.0 * np.pi * np.mod(k1 * tok, n) / n
    c, s = np.cos(ang), np.sin(ang)
    re_rows = np.concatenate([c, s], axis=2)
    im_rows = np.concatenate([-s, c], axis=2)
    return np.stack([re_rows, im_rows], axis=2).reshape(n2, 2 * n1, 2 * n1).astype(np.float32)


@functools.lru_cache(maxsize=None)
def _fft_b_table(n2, scale):
    ang = 2.0 * np.pi * np.outer(np.arange(n2), np.arange(n2)) / n2
    return (scale * np.stack([np.cos(ang), np.sin(ang)], axis=-1).reshape(n2, 2 * n2)).astype(np.float32)


def _mem_kv_kernel(mem_ref, nw_ref, wk_ref, wv_ref, wq_ref, wo_ref, wqk_ref, wvo_ref):
    m, d = mem_ref.shape[1], mem_ref.shape[2]
    dh = d // CA_HEADS
    mn = _rms(mem_ref[0], nw_ref[...]).astype(BF16)
    ck = _dot(mn, wk_ref[...]).astype(BF16)
    cv = _dot(mn, wv_ref[...]).astype(BF16)
    for h in range(CA_HEADS):
        cols = slice(h * dh, (h + 1) * dh)
        qk = lax.dot_general(wq_ref[:, cols], ck[:, cols], (((1,), (1,)), ((), ())),
                             preferred_element_type=F32) * (dh ** -0.5)
        wqk_ref[0, :, h * m:(h + 1) * m] = qk.astype(BF16)
        wvo_ref[0, h * m:(h + 1) * m, :] = _dot(cv[:, cols], wo_ref[cols, :]).astype(BF16)


def _mem_kv(mem, norm_w, w_ck, w_cv, w_cq, w_co):
    b, m, d = mem.shape
    hm = CA_HEADS * m
    sq = _resident((d, d))
    return pl.pallas_call(
        _mem_kv_kernel,
        out_shape=(jax.ShapeDtypeStruct((b, d, hm), BF16), jax.ShapeDtypeStruct((b, hm, d), BF16)),
        grid=(b,),
        in_specs=[pl.BlockSpec((1, m, d), lambda i: (i, 0, 0)), _resident((1, d)), sq, sq, sq, sq],
        out_specs=(pl.BlockSpec((1, d, hm), lambda i: (i, 0, 0)),
                   pl.BlockSpec((1, hm, d), lambda i: (i, 0, 0))),
        compiler_params=pltpu.CompilerParams(
            dimension_semantics=("arbitrary",),
            vmem_limit_bytes=_vmem_limit(4 * d * d * 2 + 4 * m * d * 4 + 4 * d * hm * 2
                                         + 2 * d * hm * 4)),
        name="mem_kv",
    )(mem, norm_w, w_ck, w_cv, w_cq, w_co)


def _log_sigmoid(x):
    return jnp.minimum(x, 0.0) - jnp.log1p(jnp.exp(-jnp.abs(x)))


def _tn_dot(a, b):
    return lax.dot_general(a, b, (((0,), (0,)), ((), ())), preferred_element_type=F32)


def _decay_rows(dec, c, dk, offset, sign):
    lg = _log_sigmoid(jnp.full((c, dk), dec, F32))
    row = lax.broadcasted_iota(jnp.int32, (c, dk), 0).astype(F32)
    return jnp.exp(lg * (offset + sign * row))


def _chunk_decay(dec, c, dk):
    return jnp.exp(_log_sigmoid(jnp.full((1, dk), dec, F32)) * c)


def _ret_bwd_init(dec_ref, state_ref, zeta_ref, gc_ref):
    c, dk = zeta_ref.shape[1], zeta_ref.shape[2]

    @pl.when(pl.program_id(1) == 0)
    def _init():
        state_ref[...] = jnp.zeros_like(state_ref)
        for h in range(RET_HEADS):
            zeta_ref[h] = _decay_rows(dec_ref[h], c, dk, 0.0, 1.0).astype(BF16)
            gc_ref[h] = _chunk_decay(dec_ref[h], c, dk)


def _ret_bwd_states(k_ref, v_ref, sb_ref, state_ref, zeta_ref, gc_ref):
    c = RET_CHUNK
    t_len, d = k_ref.shape[1], k_ref.shape[2]
    dk = d // RET_HEADS
    for h in range(RET_HEADS):
        cols = slice(h * dk, (h + 1) * dk)
        state = state_ref[h]
        for ci in reversed(range(t_len // c)):
            rows = slice(ci * c, (ci + 1) * c)
            sb_ref[0, ci, h] = state.astype(BF16)
            state = state * gc_ref[h] + _tn_dot(k_ref[0, rows, cols],
                                                v_ref[0, rows, cols] * zeta_ref[h])
        state_ref[h] = state


def _in_proj_kernel(dec_ref, x_ref, nw_ref, w_ref, cos_ref, sin_ref,
                    q_ref, k_ref, v_ref, g_ref, z_ref, gr_ref, gf_ref, sb_ref,
                    perm_ref, state_ref, zeta_ref, gc_ref):
    _ret_bwd_init(dec_ref, state_ref, zeta_ref, gc_ref)
    d = x_ref.shape[-1]
    dk = d // RET_HEADS
    half = dk // 2
    xb = _rms(x_ref[0], nw_ref[...]).astype(BF16)
    cos = cos_ref[...]
    sin = sin_ref[...]

    def proj(j):
        return _dot(xb, w_ref[:, j * d:(j + 1) * d])

    def rope(p, out_ref, scale):
        for h in range(RET_HEADS):
            x1 = p[:, h * dk:h * dk + half]
            x2 = p[:, h * dk + half:(h + 1) * dk]
            o1 = x1 * cos - x2 * sin
            o2 = x2 * cos + x1 * sin
            if scale is not None:
                o1 = o1 * scale
                o2 = o2 * scale
            out_ref[0, :, h * dk:h * dk + half] = o1.astype(BF16)
            out_ref[0, :, h * dk + half:(h + 1) * dk] = o2.astype(BF16)

    rope(proj(0), q_ref, dk ** -0.5)
    rope(proj(1), k_ref, None)
    v_ref[0] = proj(2).astype(BF16)
    _ret_bwd_states(k_ref, v_ref, sb_ref, state_ref, zeta_ref, gc_ref)
    g_ref[0] = proj(3).astype(BF16)
    u = proj(4)
    for lc in range(d // V7X_LANES):
        for blk in range(u.shape[0] // FFT_N2):
            a, par = divmod(blk, 2)
            perm_ref[lc, pl.ds(2 * FFT_N2 * a + par, FFT_N2, stride=2), :] = (
                u[blk * FFT_N2:(blk + 1) * FFT_N2, lc * V7X_LANES:(lc + 1) * V7X_LANES])
    for lc in range(d // V7X_LANES):
        z_ref[0, :, lc * V7X_LANES:(lc + 1) * V7X_LANES] = _words(perm_ref[lc])
    gr_ref[0] = proj(5).astype(BF16)
    gf_ref[0] = proj(6).astype(BF16)


def _in_proj(x, norm_w, w_in, cos, sin, decay_bwd, tm):
    b, s, d = x.shape
    dk = d // RET_HEADS
    half = dk // 2
    c = RET_CHUNK
    nt = s // tm
    act = jax.ShapeDtypeStruct((b, s, d), BF16)
    assert tm % (2 * FFT_N2) == 0 and tm % c == 0
    zsh = jax.ShapeDtypeStruct((b, s // 2, d), jnp.int32)
    sbsh = jax.ShapeDtypeStruct((b, s // c, RET_HEADS, dk, dk), BF16)
    row = pl.BlockSpec((1, tm, d), lambda i, j: (i, nt - 1 - j, 0))
    zrow = pl.BlockSpec((1, tm // 2, d), lambda i, j: (i, nt - 1 - j, 0))
    sbrow = pl.BlockSpec((1, tm // c, RET_HEADS, dk, dk), lambda i, j: (i, nt - 1 - j, 0, 0, 0))
    tab = pl.BlockSpec((tm, half), lambda i, j: (nt - 1 - j, 0))
    vmem = (w_in.size * 2 + 2 * tm * d * 4 + 2 * 8 * tm * d * 2 + 4 * tm * half * 4
            + 5 * tm * d * 4 + 2 * tm * d * 2 + RET_HEADS * (dk * dk * 4 + c * dk * 2))
    return pl.pallas_call(
        _in_proj_kernel,
        out_shape=(act, act, act, act, zsh, act, act, sbsh),
        grid=(b, nt),
        in_specs=[pl.BlockSpec(memory_space=pltpu.SMEM), row, _resident((1, d)),
                  _resident(w_in.shape), tab, tab],
        out_specs=(row, row, row, row, zrow, row, row, sbrow),
        scratch_shapes=[pltpu.VMEM((d // V7X_LANES, tm, V7X_LANES), F32),
                        pltpu.VMEM((RET_HEADS, dk, dk), F32),
                        pltpu.VMEM((RET_HEADS, c, dk), BF16),
                        pltpu.VMEM((RET_HEADS, 1, dk), F32)],
        compiler_params=pltpu.CompilerParams(
            dimension_semantics=("arbitrary", "arbitrary"),
            vmem_limit_bytes=_vmem_limit(vmem)),
        name="in_proj",
    )(decay_bwd, x, norm_w, w_in, cos, sin)


def _ret_fwd_kernel(decf_ref, decb_ref, q_ref, k_ref, v_ref, g_ref, sb_ref, r_ref,
                    state_ref, dmat_ref, xif_ref, xib_ref, zeta_ref, gc_ref):
    c = RET_CHUNK
    t_len, d = q_ref.shape[1], q_ref.shape[2]
    dk = d // RET_HEADS

    @pl.when(pl.program_id(1) == 0)
    def _init():
        state_ref[...] = jnp.zeros_like(state_ref)
        for h in range(RET_HEADS):
            xif_ref[h] = _decay_rows(decf_ref[h], c, dk, 1.0, 1.0).astype(BF16)
            xib_ref[h] = _decay_rows(decb_ref[h], c, dk, float(c), -1.0).astype(BF16)
            zeta_ref[h] = _decay_rows(decf_ref[h], c, dk, c - 1.0, -1.0).astype(BF16)
            gc_ref[h] = _chunk_decay(decf_ref[h], c, dk)
            i = lax.broadcasted_iota(jnp.int32, (c, c), 0)
            j = lax.broadcasted_iota(jnp.int32, (c, c), 1)
            diff = (i - j).astype(F32)
            lgf = _log_sigmoid(jnp.full((c, c), decf_ref[h], F32))
            lgb = _log_sigmoid(jnp.full((c, c), decb_ref[h], F32))
            dmat_ref[h] = jnp.where(i >= j, jnp.exp(lgf * jnp.maximum(diff, 0.0)),
                                    jnp.exp(lgb * jnp.maximum(-diff, 0.0)))

    for h in range(RET_HEADS):
        cols = slice(h * dk, (h + 1) * dk)
        state = state_ref[h]
        for ci in range(t_len // c):
            rows = slice(ci * c, (ci + 1) * c)
            q = q_ref[0, rows, cols]
            k = k_ref[0, rows, cols]
            v = v_ref[0, rows, cols]
            scores = lax.dot_general(q, k, (((1,), (1,)), ((), ())), preferred_element_type=F32)
            lhs = jnp.concatenate([(scores * dmat_ref[h]).astype(BF16),
                                   q * xif_ref[h], q * xib_ref[h]], axis=1)
            rhs = jnp.concatenate([v, state.astype(BF16), sb_ref[0, ci, h]], axis=0)
            y = _dot(lhs, rhs)
            state = state * gc_ref[h] + _tn_dot(k, v * zeta_ref[h])
            yc = y - jnp.mean(y, axis=-1, keepdims=True)
            var = jnp.mean(yc * yc, axis=-1, keepdims=True)
            g = g_ref[0, rows, cols]
            r_ref[0, rows, cols] = (g * jax.nn.sigmoid(g)) * (yc * lax.rsqrt(var + GN_EPS)).astype(BF16)
        state_ref[h] = state


def _fft_a_kernel(z_ref, dft_ref, tab_ref, o_ref):
    rn, d = z_ref.shape[2], z_ref.shape[3]
    gd = dft_ref.shape[0]
    u = jnp.concatenate([_rows(z_ref[0, :, r, :]) for r in range(rn)], axis=0)
    n1 = u.shape[0] // rn
    parts = [_dot(u[:, gi * gd:(gi + 1) * gd], dft_ref[...]) for gi in range(d // gd)]
    re = jnp.concatenate([p[:, :gd] for p in parts], axis=1).astype(BF16)
    im = jnp.concatenate([p[:, gd:] for p in parts], axis=1).astype(BF16)
    for r in range(rn):
        rows = slice(r * n1, (r + 1) * n1)
        zz = jnp.concatenate([re[rows], im[rows]], axis=0)
        o_ref[0, :, r, :] = _words(_dot(tab_ref[r], zz))


def _fft_b_kernel(zt_ref, tab_ref, o_ref):
    for i in range(zt_ref.shape[1]):
        o_ref[0, :, i, :] = _words(_dot(tab_ref[...], _rows(zt_ref[0, i])))


def _ret_fwd_fft_b_kernel(decf_ref, decb_ref, q_ref, k_ref, v_ref, g_ref, sb_ref,
                          zt_ref, tab_ref, r_ref, f_ref,
                          state_ref, dmat_ref, xif_ref, xib_ref, zeta_ref, gc_ref):
    _ret_fwd_kernel(decf_ref, decb_ref, q_ref, k_ref, v_ref, g_ref, sb_ref, r_ref,
                    state_ref, dmat_ref, xif_ref, xib_ref, zeta_ref, gc_ref)
    _fft_b_kernel(zt_ref, tab_ref, f_ref)


def _mixers(q, k, v, g, sb, z, decay_fwd, decay_bwd, dft, tab_a, tab_b):
    b, s, d = q.shape
    dk = d // RET_HEADS
    c = RET_CHUNK
    n2 = tab_b.shape[0]
    n1 = s // n2
    steps_b = n1 // SUBLANES
    t_b = s // steps_b
    assert t_b % c == 0
    smem = pl.BlockSpec(memory_space=pltpu.SMEM)
    fwd = pl.BlockSpec((1, t_b, d), lambda i, t: (i, t, 0))
    xi = pltpu.VMEM((RET_HEADS, c, dk), BF16)
    table_bytes = RET_HEADS * (3 * c * dk + c * c + dk * dk + dk) * 4
    stage_b_bytes = 2 * (SUBLANES * n2 * d * 4 + (n2 // 2) * SUBLANES * d * 4) + 4 * n2 * d * 4
    zt = pl.pallas_call(
        _fft_a_kernel,
        out_shape=jax.ShapeDtypeStruct((b, n1, n2, d), jnp.int32),
        grid=(b, n2 // SUBLANES),
        in_specs=[pl.BlockSpec((1, n1 // 2, SUBLANES, d), lambda i, t: (i, 0, t, 0)),
                  _resident(dft.shape),
                  pl.BlockSpec((SUBLANES, 2 * n1, 2 * n1), lambda i, t: (t, 0, 0))],
        out_specs=pl.BlockSpec((1, n1, SUBLANES, d), lambda i, t: (i, 0, t, 0)),
        compiler_params=pltpu.CompilerParams(
            dimension_semantics=("parallel", "parallel"),
            vmem_limit_bytes=_vmem_limit(2 * (3 * n1 * SUBLANES * d * 2
                                              + SUBLANES * 4 * n1 * n1 * 2)
                                         + 5 * n1 * SUBLANES * d * 4)),
        name="fft_a",
    )(z.reshape(b, n1 // 2, n2, d), dft, tab_a)
    return pl.pallas_call(
        _ret_fwd_fft_b_kernel,
        out_shape=(jax.ShapeDtypeStruct((b, s, d), BF16),
                   jax.ShapeDtypeStruct((b, n2 // 2, n1, d), jnp.int32)),
        grid=(b, steps_b),
        in_specs=[smem, smem, fwd, fwd, fwd, fwd,
                  pl.BlockSpec((1, t_b // c, RET_HEADS, dk, dk), lambda i, t: (i, t, 0, 0, 0)),
                  pl.BlockSpec((1, SUBLANES, n2, d), lambda i, t: (i, t, 0, 0)),
                  _resident((n2, 2 * n2))],
        out_specs=(fwd, pl.BlockSpec((1, n2 // 2, SUBLANES, d), lambda i, t: (i, 0, t, 0))),
        scratch_shapes=[pltpu.VMEM((RET_HEADS, dk, dk), F32), pltpu.VMEM((RET_HEADS, c, c), F32),
                        xi, xi, pltpu.VMEM((RET_HEADS, c, dk), BF16),
                        pltpu.VMEM((RET_HEADS, 1, dk), F32)],
        compiler_params=pltpu.CompilerParams(
            dimension_semantics=("arbitrary", "arbitrary"),
            vmem_limit_bytes=_vmem_limit(2 * 6 * t_b * d * 2 + table_bytes + stage_b_bytes)),
        name="ret_fwd_fft_b",
    )(decay_fwd, decay_bwd, q, k, v, g, sb, zt, tab_b)


def _gain_rows_kernel(w_ref, g_ref, o_ref):
    o_ref[...] = (w_ref[...] * g_ref[...]).astype(BF16)


def _gain_rows(w, gain):
    d_in, d_out = w.shape
    rows = min(d_in, V7X_MXU_DIM)
    return pl.pallas_call(
        _gain_rows_kernel,
        out_shape=jax.ShapeDtypeStruct((d_in, d_out), BF16),
        grid=(d_in // rows,),
        in_specs=[pl.BlockSpec((rows, d_out), lambda i: (i, 0)),
                  pl.BlockSpec((rows, 1), lambda i: (i, 0))],
        out_specs=pl.BlockSpec((rows, d_out), lambda i: (i, 0)),
        compiler_params=pltpu.CompilerParams(dimension_semantics=("parallel",)),
        name="gain_rows",
    )(w, gain.reshape(d_in, 1))


def _tail_kernel(x_ref, r_ref, f_ref, gr_ref, gf_ref, wqk_ref, wvo_ref,
                 w_ro_ref, w_fo_ref, w_mo_ref, n_ca_ref,
                 n_mlp_ref, w_up_ref, w_dn_ref, n_fin_ref, o_ref, att_ref, perm_ref):
    d = x_ref.shape[-1]
    m = wqk_ref.shape[2] // CA_HEADS
    ff = w_up_ref.shape[1]
    gq, n1 = f_ref.shape[1], f_ref.shape[2]
    x = x_ref[0]
    ret = _dot(r_ref[0], w_ro_ref[...])
    fperm = _dot(_rows(f_ref[0].reshape(gq * n1, d)), w_fo_ref[...])
    for lc in range(d // V7X_LANES):
        perm_ref[lc] = fperm[:, lc * V7X_LANES:(lc + 1) * V7X_LANES]
    fou = jnp.concatenate(
        [jnp.concatenate([perm_ref[lc, pl.ds(blk // 2 * 2 * n1 + blk % 2, n1, stride=2), :]
                          for lc in range(d // V7X_LANES)], axis=1)
         for blk in range(2 * gq)], axis=0)
    merged = (jax.nn.sigmoid(gr_ref[0].astype(F32)) * ret
              + jax.nn.sigmoid(gf_ref[0].astype(F32)) * fou)
    x = x + _dot(merged.astype(BF16), w_mo_ref[...])

    logits = _dot(_rms(x, n_ca_ref[...]).astype(BF16), wqk_ref[0])
    for h in range(CA_HEADS):
        cols = slice(h * m, (h + 1) * m)
        p = jnp.exp(logits[:, cols] - jnp.max(logits[:, cols], axis=-1, keepdims=True))
        att_ref[:, cols] = (p / jnp.sum(p, axis=-1, keepdims=True)).astype(BF16)
    x = x + _dot(att_ref[...], wvo_ref[0])

    xb = _rms(x, n_mlp_ref[...]).astype(BF16)
    acc = x
    for j in range(ff // d):
        cols = slice(j * d, (j + 1) * d)
        hid = jnp.square(jnp.maximum(_dot(xb, w_up_ref[:, cols]), 0.0)).astype(BF16)
        acc = acc + _dot(hid, w_dn_ref[cols, :])
    o_ref[0] = _rms(acc, n_fin_ref[...])


def _tail(x, r, four, gate_r, gate_f, wqk, wvo, w_ro, w_fo, w_mo, n_ca,
          n_mlp, w_up, w_dn, n_fin, tm):
    b, s, d = x.shape
    hm = wqk.shape[2]
    ff = w_up.shape[1]
    row = pl.BlockSpec((1, tm, d), lambda i, j: (i, j, 0))
    n1 = four.shape[2]
    assert tm % (2 * n1) == 0
    half = pl.BlockSpec((1, tm // (2 * n1), n1, d), lambda i, j: (i, j, 0, 0))
    vec = _resident((1, d))
    sq = _resident((d, d))
    vmem = ((3 * d * d + 2 * d * ff) * 2 + 4 * d * hm * 2 + 2 * tm * d * (4 + 4 + 4 * 2)
            + tm * d * 4 * 8)
    return pl.pallas_call(
        _tail_kernel,
        out_shape=jax.ShapeDtypeStruct((b, s, d), F32),
        grid=(b, s // tm),
        in_specs=[row, row, half, row, row,
                  pl.BlockSpec((1, d, hm), lambda i, j: (i, 0, 0)),
                  pl.BlockSpec((1, hm, d), lambda i, j: (i, 0, 0)),
                  sq, sq, sq, vec, vec, _resident((d, ff)), _resident((ff, d)), vec],
        out_specs=row,
        scratch_shapes=[pltpu.VMEM((tm, hm), BF16),
                        pltpu.VMEM((d // V7X_LANES, tm, V7X_LANES), F32)],
        compiler_params=pltpu.CompilerParams(
            dimension_semantics=("parallel", "parallel"),
            vmem_limit_bytes=_vmem_limit(vmem)),
        name="tail",
    )(x, r, four, gate_r, gate_f, wqk, wvo, w_ro, w_fo, w_mo, n_ca, n_mlp, w_up, w_dn, n_fin)


def _tiles(s):
    return min(512, s), min(512, s)


def _trunk(x, mem, p):
    b, s, d = x.shape
    dk = d // RET_HEADS
    gd = d // FOUR_GROUPS
    n1 = s // FFT_N2
    assert s % FFT_N2 == 0 and s % RET_CHUNK == 0
    tm_in, tm_tail = _tiles(s)
    cos, sin = (jnp.asarray(t) for t in _rope_tables(s, dk))
    dft = jnp.asarray(_chan_dft(gd)).astype(BF16)
    tab_a = jnp.asarray(_fft_a_table(n1, FFT_N2)).astype(BF16)
    tab_b = jnp.asarray(_fft_b_table(FFT_N2, float(s * gd) ** -0.5)).astype(BF16)

    for l in range(p["w_in"].shape[0]):
        wqk, wvo = _mem_kv(mem, p["norm_mem_w"][l], p["w_ck"][l], p["w_cv"][l],
                           p["w_cq"][l], p["w_co"][l])
        q, k, v, g, z, gate_r, gate_f, sb = _in_proj(x, p["norm_mix_w"][l], p["w_in"][l], cos, sin,
                                                     p["ret_decay_bwd"][l], tm_in)
        r, four = _mixers(q, k, v, g, sb, z, p["ret_decay_fwd"][l], p["ret_decay_bwd"][l],
                          dft, tab_a, tab_b)
        w_ro = _gain_rows(p["w_ret_out"][l], p["ret_gn_w"][l])
        x = _tail(x, r, four, gate_r, gate_f, wqk, wvo, w_ro, p["w_four_out"][l],
                  p["w_mix_out"][l], p["norm_ca_w"][l], p["norm_mlp_w"][l], p["w_up"][l],
                  p["w_down"][l], p["norm_final_w"], tm_tail)
    return x


def kernel(x_prompt, x_sample, mem_prompt, mem_sample, norm_mix_w, w_in, ret_decay_fwd,
           ret_decay_bwd, ret_gn_w, w_ret_out, w_four_out, w_mix_out, norm_ca_w, norm_mem_w,
           w_cq, w_ck, w_cv, w_co, norm_mlp_w, w_up, w_down, norm_final_w):
    depth, d = norm_mix_w.shape
    assert depth == 1, "the tail fuses the final norm into the (single) layer"
    vec = lambda w: w.reshape(depth, 1, d)
    mat = lambda w: w.astype(BF16)
    p = dict(
        norm_mix_w=vec(norm_mix_w), w_in=mat(w_in), ret_decay_fwd=ret_decay_fwd,
        ret_decay_bwd=ret_decay_bwd, ret_gn_w=ret_gn_w, w_ret_out=w_ret_out,
        w_four_out=mat(w_four_out), w_mix_out=mat(w_mix_out), norm_ca_w=vec(norm_ca_w),
        norm_mem_w=vec(norm_mem_w), w_cq=mat(w_cq), w_ck=mat(w_ck), w_cv=mat(w_cv),
        w_co=mat(w_co), norm_mlp_w=vec(norm_mlp_w), w_up=mat(w_up), w_down=mat(w_down),
        norm_final_w=norm_final_w.reshape(1, d))
    return _trunk(x_prompt, mem_prompt, p), _trunk(x_sample, mem_sample, p)
```

```python
import functools

import numpy as np
import jax
import jax.numpy as jnp
from jax import lax
from jax.experimental import pallas as pl
from jax.experimental.pallas import tpu as pltpu

RET_HEADS = 4
FOUR_GROUPS = 4
CA_HEADS = 4
ROPE_THETA = 10000.0
EPS = 1e-6
GN_EPS = 1e-5

V7X_VMEM_BYTES = 64 * 1024 * 1024
V7X_MXU_DIM = 256
V7X_LANES = 128
SUBLANES = 8

RET_CHUNK = V7X_MXU_DIM
FFT_N2 = 128
ROW_BLOCKS = 2

BF16 = jnp.bfloat16
F32 = jnp.float32


COMPILER_SCRATCH_BYTES = 8 * 1024 * 1024
VMEM_RESERVE_BYTES = 6 * 1024 * 1024


def _vmem_limit(nbytes):
    return int(min(nbytes + COMPILER_SCRATCH_BYTES, V7X_VMEM_BYTES - VMEM_RESERVE_BYTES))


def _rms(x, w):
    return x * lax.rsqrt(jnp.mean(x * x, axis=-1, keepdims=True) + EPS) * w


def _dot(a, b):
    return jnp.dot(a, b, preferred_element_type=F32)


def _words(x):
    return pltpu.bitcast(x.astype(BF16), jnp.int32)


def _rows(w):
    return pltpu.bitcast(w, BF16)


def _resident(shape):
    return pl.BlockSpec(shape, lambda *_: (0,) * len(shape), pipeline_mode=pl.Buffered(1))


@functools.lru_cache(maxsize=None)
def _rope_tables(s, dk):
    inv = ROPE_THETA ** (-np.arange(0, dk, 2, dtype=np.float64) / dk)
    ang = np.arange(s, dtype=np.float64)[:, None] * inv[None, :]
    return np.cos(ang).astype(np.float32), np.sin(ang).astype(np.float32)


@functools.lru_cache(maxsize=None)
def _chan_dft(n):
    ang = 2.0 * np.pi * np.outer(np.arange(n), np.arange(n)) / n
    return np.concatenate([np.cos(ang), -np.sin(ang)], axis=1).astype(np.float32)


@functools.lru_cache(maxsize=None)
def _fft_a_table(n1, n2):
    n = n1 * n2
    k1 = np.arange(n1, dtype=np.float64)[None, :, None]
    tok = (n2 * np.arange(n1, dtype=np.float64)[None, None, :]
           + np.arange(n2, dtype=np.float64)[:, None, None])
    ang = 2.0 * np.pi * np.mod(k1 * tok, n) / n
    c, s = np.cos(ang), np.sin(ang)
    re_rows = np.concatenate([c, s], axis=2)
    im_rows = np.concatenate([-s, c], axis=2)
    return np.stack([re_rows, im_rows], axis=2).reshape(n2, 2 * n1, 2 * n1).astype(np.float32)


@functools.lru_cache(maxsize=None)
def _fft_b_table(n2, scale):
    ang = 2.0 * np.pi * np.outer(np.arange(n2), np.arange(n2)) / n2
    return (scale * np.stack([np.cos(ang), np.sin(ang)], axis=-1).reshape(n2, 2 * n2)).astype(np.float32)


def _mem_kv_kernel(mem_ref, nw_ref, wk_ref, wv_ref, wq_ref, wo_ref, wqk_ref, wvo_ref):
    m, d = mem_ref.shape[1], mem_ref.shape[2]
    dh = d // CA_HEADS
    mn = _rms(mem_ref[0], nw_ref[...]).astype(BF16)
    ck = _dot(mn, wk_ref[...]).astype(BF16)
    cv = _dot(mn, wv_ref[...]).astype(BF16)
    for h in range(CA_HEADS):
        cols = slice(h * dh, (h + 1) * dh)
        qk = lax.dot_general(wq_ref[:, cols], ck[:, cols], (((1,), (1,)), ((), ())),
                             preferred_element_type=F32) * (dh ** -0.5)
        wqk_ref[0, :, h * m:(h + 1) * m] = qk.astype(BF16)
        wvo_ref[0, h * m:(h + 1) * m, :] = _dot(cv[:, cols], wo_ref[cols, :]).astype(BF16)


def _mem_kv(mem, norm_w, w_ck, w_cv, w_cq, w_co):
    b, m, d = mem.shape
    hm = CA_HEADS * m
    sq = _resident((d, d))
    return pl.pallas_call(
        _mem_kv_kernel,
        out_shape=(jax.ShapeDtypeStruct((b, d, hm), BF16), jax.ShapeDtypeStruct((b, hm, d), BF16)),
        grid=(b,),
        in_specs=[pl.BlockSpec((1, m, d), lambda i: (i, 0, 0)), _resident((1, d)), sq, sq, sq, sq],
        out_specs=(pl.BlockSpec((1, d, hm), lambda i: (i, 0, 0)),
                   pl.BlockSpec((1, hm, d), lambda i: (i, 0, 0))),
        compiler_params=pltpu.CompilerParams(
            dimension_semantics=("arbitrary",),
            vmem_limit_bytes=_vmem_limit(4 * d * d * 2 + 4 * m * d * 4 + 4 * d * hm * 2
                                         + 2 * d * hm * 4)),
        name="mem_kv",
    )(mem, norm_w, w_ck, w_cv, w_cq, w_co)


def _log_sigmoid(x):
    return jnp.minimum(x, 0.0) - jnp.log1p(jnp.exp(-jnp.abs(x)))


def _tn_dot(a, b):
    return lax.dot_general(a, b, (((0,), (0,)), ((), ())), preferred_element_type=F32)


def _decay_rows(dec, c, dk, offset, sign):
    lg = _log_sigmoid(jnp.full((c, dk), dec, F32))
    row = lax.broadcasted_iota(jnp.int32, (c, dk), 0).astype(F32)
    return jnp.exp(lg * (offset + sign * row))


def _chunk_decay(dec, c, dk):
    return jnp.exp(_log_sigmoid(jnp.full((1, dk), dec, F32)) * c)


def _ret_bwd_init(dec_ref, state_ref, zeta_ref, gc_ref):
    c, dk = zeta_ref.shape[1], zeta_ref.shape[2]

    @pl.when(pl.program_id(1) == 0)
    def _init():
        state_ref[...] = jnp.zeros_like(state_ref)
        for h in range(RET_HEADS):
            zeta_ref[h] = _decay_rows(dec_ref[h], c, dk, 0.0, 1.0).astype(BF16)
            gc_ref[h] = _chunk_decay(dec_ref[h], c, dk)


def _ret_bwd_states(k_ref, v_ref, sb_ref, state_ref, zeta_ref, gc_ref):
    c = RET_CHUNK
    t_len, d = k_ref.shape[1], k_ref.shape[2]
    dk = d // RET_HEADS
    for h in range(RET_HEADS):
        cols = slice(h * dk, (h + 1) * dk)
        state = state_ref[h]
        for ci in reversed(range(t_len // c)):
            rows = slice(ci * c, (ci + 1) * c)
            sb_ref[0, ci, h] = state.astype(BF16)
            state = state * gc_ref[h] + _tn_dot(k_ref[0, rows, cols],
                                                v_ref[0, rows, cols] * zeta_ref[h])
        state_ref[h] = state


def _in_proj_kernel(dec_ref, x_ref, nw_ref, w_ref, cos_ref, sin_ref,
                    q_ref, k_ref, v_ref, g_ref, z_ref, gr_ref, gf_ref, sb_ref,
                    perm_ref, state_ref, zeta_ref, gc_ref):
    _ret_bwd_init(dec_ref, state_ref, zeta_ref, gc_ref)
    d = x_ref.shape[-1]
    dk = d // RET_HEADS
    half = dk // 2
    tm = x_ref.shape[1]
    hb = tm // ROW_BLOCKS
    blocks = [slice(i * hb, (i + 1) * hb) for i in range(ROW_BLOCKS)]
    xb = [_rms(x_ref[0, rows, :], nw_ref[...]).astype(BF16) for rows in blocks]

    def proj(j):
        return [_dot(b, w_ref[:, j * d:(j + 1) * d]) for b in xb]

    def rope(ps, out_ref, scale):
        for p, rows in zip(ps, blocks):
            cos = cos_ref[rows, :]
            sin = sin_ref[rows, :]
            for h in range(RET_HEADS):
                x1 = p[:, h * dk:h * dk + half]
                x2 = p[:, h * dk + half:(h + 1) * dk]
                o1 = x1 * cos - x2 * sin
                o2 = x2 * cos + x1 * sin
                if scale is not None:
                    o1 = o1 * scale
                    o2 = o2 * scale
                out_ref[0, rows, h * dk:h * dk + half] = o1.astype(BF16)
                out_ref[0, rows, h * dk + half:(h + 1) * dk] = o2.astype(BF16)

    def store(ps, out_ref):
        for p, rows in zip(ps, blocks):
            out_ref[0, rows, :] = p.astype(BF16)

    rope(proj(0), q_ref, dk ** -0.5)
    rope(proj(1), k_ref, None)
    store(proj(2), v_ref)
    _ret_bwd_states(k_ref, v_ref, sb_ref, state_ref, zeta_ref, gc_ref)
    store(proj(3), g_ref)
    us = proj(4)
    for lc in range(d // V7X_LANES):
        for blk in range(tm // FFT_N2):
            a, par = divmod(blk, 2)
            u = us[blk * FFT_N2 // hb]
            r0 = blk * FFT_N2 % hb
            perm_ref[lc, pl.ds(2 * FFT_N2 * a + par, FFT_N2, stride=2), :] = (
                u[r0:r0 + FFT_N2, lc * V7X_LANES:(lc + 1) * V7X_LANES])
    for lc in range(d // V7X_LANES):
        z_ref[0, :, lc * V7X_LANES:(lc + 1) * V7X_LANES] = _words(perm_ref[lc])
    store(proj(5), gr_ref)
    store(proj(6), gf_ref)


def _in_proj(x, norm_w, w_in, cos, sin, decay_bwd, tm):
    b, s, d = x.shape
    dk = d // RET_HEADS
    half = dk // 2
    c = RET_CHUNK
    nt = s // tm
    act = jax.ShapeDtypeStruct((b, s, d), BF16)
    assert tm % (2 * FFT_N2) == 0 and tm % c == 0 and (tm // ROW_BLOCKS) % FFT_N2 == 0
    zsh = jax.ShapeDtypeStruct((b, s // 2, d), jnp.int32)
    sbsh = jax.ShapeDtypeStruct((b, s // c, RET_HEADS, dk, dk), BF16)
    row = pl.BlockSpec((1, tm, d), lambda i, j: (i, nt - 1 - j, 0))
    zrow = pl.BlockSpec((1, tm // 2, d), lambda i, j: (i, nt - 1 - j, 0))
    sbrow = pl.BlockSpec((1, tm // c, RET_HEADS, dk, dk), lambda i, j: (i, nt - 1 - j, 0, 0, 0))
    tab = pl.BlockSpec((tm, half), lambda i, j: (nt - 1 - j, 0))
    vmem = (w_in.size * 2 + 2 * tm * d * 4 + 2 * 8 * tm * d * 2 + 4 * tm * half * 4
            + 5 * tm * d * 4 + 2 * tm * d * 2 + RET_HEADS * (dk * dk * 4 + c * dk * 2))
    return pl.pallas_call(
        _in_proj_kernel,
        out_shape=(act, act, act, act, zsh, act, act, sbsh),
        grid=(b, nt),
        in_specs=[pl.BlockSpec(memory_space=pltpu.SMEM), row, _resident((1, d)),
                  _resident(w_in.shape), tab, tab],
        out_specs=(row, row, row, row, zrow, row, row, sbrow),
        scratch_shapes=[pltpu.VMEM((d // V7X_LANES, tm, V7X_LANES), F32),
                        pltpu.VMEM((RET_HEADS, dk, dk), F32),
                        pltpu.VMEM((RET_HEADS, c, dk), BF16),
                        pltpu.VMEM((RET_HEADS, 1, dk), F32)],
        compiler_params=pltpu.CompilerParams(
            dimension_semantics=("arbitrary", "arbitrary"),
            vmem_limit_bytes=_vmem_limit(vmem)),
        name="in_proj",
    )(decay_bwd, x, norm_w, w_in, cos, sin)


def _ret_fwd_kernel(decf_ref, decb_ref, q_ref, k_ref, v_ref, g_ref, sb_ref, r_ref,
                    state_ref, dmat_ref, xif_ref, xib_ref, zeta_ref, gc_ref):
    c = RET_CHUNK
    t_len, d = q_ref.shape[1], q_ref.shape[2]
    dk = d // RET_HEADS

    @pl.when(pl.program_id(1) == 0)
    def _init():
        state_ref[...] = jnp.zeros_like(state_ref)
        for h in range(RET_HEADS):
            xif_ref[h] = _decay_rows(decf_ref[h], c, dk, 1.0, 1.0).astype(BF16)
            xib_ref[h] = _decay_rows(decb_ref[h], c, dk, float(c), -1.0).astype(BF16)
            zeta_ref[h] = _decay_rows(decf_ref[h], c, dk, c - 1.0, -1.0).astype(BF16)
            gc_ref[h] = _chunk_decay(decf_ref[h], c, dk)
            i = lax.broadcasted_iota(jnp.int32, (c, c), 0)
            j = lax.broadcasted_iota(jnp.int32, (c, c), 1)
            diff = (i - j).astype(F32)
            lgf = _log_sigmoid(jnp.full((c, c), decf_ref[h], F32))
            lgb = _log_sigmoid(jnp.full((c, c), decb_ref[h], F32))
            dmat_ref[h] = jnp.where(i >= j, jnp.exp(lgf * jnp.maximum(diff, 0.0)),
                                    jnp.exp(lgb * jnp.maximum(-diff, 0.0)))

    for h in range(RET_HEADS):
        cols = slice(h * dk, (h + 1) * dk)
        state = state_ref[h]
        for ci in range(t_len // c):
            rows = slice(ci * c, (ci + 1) * c)
            q = q_ref[0, rows, cols]
            k = k_ref[0, rows, cols]
            v = v_ref[0, rows, cols]
            scores = lax.dot_general(q, k, (((1,), (1,)), ((), ())), preferred_element_type=F32)
            lhs = jnp.concatenate([(scores * dmat_ref[h]).astype(BF16),
                                   q * xif_ref[h], q * xib_ref[h]], axis=1)
            rhs = jnp.concatenate([v, state.astype(BF16), sb_ref[0, ci, h]], axis=0)
            y = _dot(lhs, rhs)
            state = state * gc_ref[h] + _tn_dot(k, v * zeta_ref[h])
            yc = y - jnp.mean(y, axis=-1, keepdims=True)
            var = jnp.mean(yc * yc, axis=-1, keepdims=True)
            g = g_ref[0, rows, cols]
            r_ref[0, rows, cols] = (g * jax.nn.sigmoid(g)) * (yc * lax.rsqrt(var + GN_EPS)).astype(BF16)
        state_ref[h] = state


def _fft_a_kernel(z_ref, dft_ref, tab_ref, o_ref):
    rn, d = z_ref.shape[2], z_ref.shape[3]
    gd = dft_ref.shape[0]
    u = jnp.concatenate([_rows(z_ref[0, :, r, :]) for r in range(rn)], axis=0)
    n1 = u.shape[0] // rn
    parts = [_dot(u[:, gi * gd:(gi + 1) * gd], dft_ref[...]) for gi in range(d // gd)]
    re = jnp.concatenate([p[:, :gd] for p in parts], axis=1).astype(BF16)
    im = jnp.concatenate([p[:, gd:] for p in parts], axis=1).astype(BF16)
    for r in range(rn):
        rows = slice(r * n1, (r + 1) * n1)
        zz = jnp.concatenate([re[rows], im[rows]], axis=0)
        o_ref[0, :, r, :] = _words(_dot(tab_ref[r], zz))


def _fft_b_kernel(zt_ref, tab_ref, o_ref):
    for i in range(zt_ref.shape[1]):
        o_ref[0, :, i, :] = _words(_dot(tab_ref[...], _rows(zt_ref[0, i])))


def _ret_fwd_fft_b_kernel(decf_ref, decb_ref, q_ref, k_ref, v_ref, g_ref, sb_ref,
                          zt_ref, tab_ref, r_ref, f_ref,
                          state_ref, dmat_ref, xif_ref, xib_ref, zeta_ref, gc_ref):
    _ret_fwd_kernel(decf_ref, decb_ref, q_ref, k_ref, v_ref, g_ref, sb_ref, r_ref,
                    state_ref, dmat_ref, xif_ref, xib_ref, zeta_ref, gc_ref)
    _fft_b_kernel(zt_ref, tab_ref, f_ref)


def _mixers(q, k, v, g, sb, z, decay_fwd, decay_bwd, dft, tab_a, tab_b):
    b, s, d = q.shape
    dk = d // RET_HEADS
    c = RET_CHUNK
    n2 = tab_b.shape[0]
    n1 = s // n2
    steps_b = n1 // SUBLANES
    t_b = s // steps_b
    assert t_b % c == 0
    smem = pl.BlockSpec(memory_space=pltpu.SMEM)
    fwd = pl.BlockSpec((1, t_b, d), lambda i, t: (i, t, 0))
    xi = pltpu.VMEM((RET_HEADS, c, dk), BF16)
    table_bytes = RET_HEADS * (3 * c * dk + c * c + dk * dk + dk) * 4
    stage_b_bytes = 2 * (SUBLANES * n2 * d * 4 + (n2 // 2) * SUBLANES * d * 4) + 4 * n2 * d * 4
    zt = pl.pallas_call(
        _fft_a_kernel,
        out_shape=jax.ShapeDtypeStruct((b, n1, n2, d), jnp.int32),
        grid=(b, n2 // SUBLANES),
        in_specs=[pl.BlockSpec((1, n1 // 2, SUBLANES, d), lambda i, t: (i, 0, t, 0)),
                  _resident(dft.shape),
                  pl.BlockSpec((SUBLANES, 2 * n1, 2 * n1), lambda i, t: (t, 0, 0))],
        out_specs=pl.BlockSpec((1, n1, SUBLANES, d), lambda i, t: (i, 0, t, 0)),
        compiler_params=pltpu.CompilerParams(
            dimension_semantics=("parallel", "parallel"),
            vmem_limit_bytes=_vmem_limit(2 * (3 * n1 * SUBLANES * d * 2
                                              + SUBLANES * 4 * n1 * n1 * 2)
                                         + 5 * n1 * SUBLANES * d * 4)),
        name="fft_a",
    )(z.reshape(b, n1 // 2, n2, d), dft, tab_a)
    return pl.pallas_call(
        _ret_fwd_fft_b_kernel,
        out_shape=(jax.ShapeDtypeStruct((b, s, d), BF16),
                   jax.ShapeDtypeStruct((b, n2 // 2, n1, d), jnp.int32)),
        grid=(b, steps_b),
        in_specs=[smem, smem, fwd, fwd, fwd, fwd,
                  pl.BlockSpec((1, t_b // c, RET_HEADS, dk, dk), lambda i, t: (i, t, 0, 0, 0)),
                  pl.BlockSpec((1, SUBLANES, n2, d), lambda i, t: (i, t, 0, 0)),
                  _resident((n2, 2 * n2))],
        out_specs=(fwd, pl.BlockSpec((1, n2 // 2, SUBLANES, d), lambda i, t: (i, 0, t, 0))),
        scratch_shapes=[pltpu.VMEM((RET_HEADS, dk, dk), F32), pltpu.VMEM((RET_HEADS, c, c), F32),
                        xi, xi, pltpu.VMEM((RET_HEADS, c, dk), BF16),
                        pltpu.VMEM((RET_HEADS, 1, dk), F32)],
        compiler_params=pltpu.CompilerParams(
            dimension_semantics=("arbitrary", "arbitrary"),
            vmem_limit_bytes=_vmem_limit(2 * 6 * t_b * d * 2 + table_bytes + stage_b_bytes)),
        name="ret_fwd_fft_b",
    )(decay_fwd, decay_bwd, q, k, v, g, sb, zt, tab_b)


def _gain_rows_kernel(w_ref, g_ref, o_ref):
    o_ref[...] = (w_ref[...] * g_ref[...]).astype(BF16)


def _gain_rows(w, gain):
    d_in, d_out = w.shape
    rows = min(d_in, V7X_MXU_DIM)
    return pl.pallas_call(
        _gain_rows_kernel,
        out_shape=jax.ShapeDtypeStruct((d_in, d_out), BF16),
        grid=(d_in // rows,),
        in_specs=[pl.BlockSpec((rows, d_out), lambda i: (i, 0)),
                  pl.BlockSpec((rows, 1), lambda i: (i, 0))],
        out_specs=pl.BlockSpec((rows, d_out), lambda i: (i, 0)),
        compiler_params=pltpu.CompilerParams(dimension_semantics=("parallel",)),
        name="gain_rows",
    )(w, gain.reshape(d_in, 1))


def _tail_kernel(x_ref, r_ref, f_ref, gr_ref, gf_ref, wqk_ref, wvo_ref,
                 w_ro_ref, w_fo_ref, w_mo_ref, n_ca_ref,
                 n_mlp_ref, w_up_ref, w_dn_ref, n_fin_ref, o_ref, att_ref, perm_ref):
    d = x_ref.shape[-1]
    m = wqk_ref.shape[2] // CA_HEADS
    ff = w_up_ref.shape[1]
    gq, n1 = f_ref.shape[1], f_ref.shape[2]
    hb = x_ref.shape[1] // ROW_BLOCKS
    lanes = range(d // V7X_LANES)
    blocks = [slice(i * hb, (i + 1) * hb) for i in range(ROW_BLOCKS)]
    ret = [_dot(r_ref[0, rows, :], w_ro_ref[...]) for rows in blocks]
    for i, rows in enumerate(blocks):
        g0, g1 = i * gq // ROW_BLOCKS, (i + 1) * gq // ROW_BLOCKS
        fperm = _dot(_rows(f_ref[0, g0:g1].reshape(hb // 2, d)), w_fo_ref[...])
        for lc in lanes:
            perm_ref[lc, rows, :] = fperm[:, lc * V7X_LANES:(lc + 1) * V7X_LANES]
    xs = []
    for i, rows in enumerate(blocks):
        fou = jnp.concatenate(
            [jnp.concatenate([perm_ref[lc, pl.ds(i * hb + blk // 2 * 2 * n1 + blk % 2, n1,
                                                 stride=2), :] for lc in lanes], axis=1)
             for blk in range(2 * gq // ROW_BLOCKS)], axis=0)
        merged = (jax.nn.sigmoid(gr_ref[0, rows, :].astype(F32)) * ret[i]
                  + jax.nn.sigmoid(gf_ref[0, rows, :].astype(F32)) * fou)
        xs.append(x_ref[0, rows, :] + _dot(merged.astype(BF16), w_mo_ref[...]))

    logits = [_dot(_rms(x, n_ca_ref[...]).astype(BF16), wqk_ref[0]) for x in xs]
    for lg, rows in zip(logits, blocks):
        for h in range(CA_HEADS):
            cols = slice(h * m, (h + 1) * m)
            p = jnp.exp(lg[:, cols] - jnp.max(lg[:, cols], axis=-1, keepdims=True))
            att_ref[rows, cols] = (p / jnp.sum(p, axis=-1, keepdims=True)).astype(BF16)
    xs = [x + _dot(att_ref[rows, :], wvo_ref[0]) for x, rows in zip(xs, blocks)]

    xb = [_rms(x, n_mlp_ref[...]).astype(BF16) for x in xs]
    for j in range(ff // d):
        cols = slice(j * d, (j + 1) * d)
        hid = [jnp.square(jnp.maximum(_dot(b, w_up_ref[:, cols]), 0.0)).astype(BF16) for b in xb]
        xs = [x + _dot(hd, w_dn_ref[cols, :]) for x, hd in zip(xs, hid)]
    for x, rows in zip(xs, blocks):
        o_ref[0, rows, :] = _rms(x, n_fin_ref[...])


def _tail(x, r, four, gate_r, gate_f, wqk, wvo, w_ro, w_fo, w_mo, n_ca,
          n_mlp, w_up, w_dn, n_fin, tm):
    b, s, d = x.shape
    hm = wqk.shape[2]
    ff = w_up.shape[1]
    row = pl.BlockSpec((1, tm, d), lambda i, j: (i, j, 0))
    n1 = four.shape[2]
    assert tm % (2 * n1 * ROW_BLOCKS) == 0
    half = pl.BlockSpec((1, tm // (2 * n1), n1, d), lambda i, j: (i, j, 0, 0))
    vec = _resident((1, d))
    sq = _resident((d, d))
    vmem = ((3 * d * d + 2 * d * ff) * 2 + 4 * d * hm * 2 + 2 * tm * d * (4 + 4 + 4 * 2)
            + tm * d * 4 * 8)
    return pl.pallas_call(
        _tail_kernel,
        out_shape=jax.ShapeDtypeStruct((b, s, d), F32),
        grid=(b, s // tm),
        in_specs=[row, row, half, row, row,
                  pl.BlockSpec((1, d, hm), lambda i, j: (i, 0, 0)),
                  pl.BlockSpec((1, hm, d), lambda i, j: (i, 0, 0)),
                  sq, sq, sq, vec, vec, _resident((d, ff)), _resident((ff, d)), vec],
        out_specs=row,
        scratch_shapes=[pltpu.VMEM((tm, hm), BF16),
                        pltpu.VMEM((d // V7X_LANES, tm, V7X_LANES), F32)],
        compiler_params=pltpu.CompilerParams(
            dimension_semantics=("parallel", "parallel"),
            vmem_limit_bytes=_vmem_limit(vmem)),
        name="tail",
    )(x, r, four, gate_r, gate_f, wqk, wvo, w_ro, w_fo, w_mo, n_ca, n_mlp, w_up, w_dn, n_fin)


def _tiles(s):
    return min(512, s), min(512, s)


def _trunk(x, mem, p):
    b, s, d = x.shape
    dk = d // RET_HEADS
    gd = d // FOUR_GROUPS
    n1 = s // FFT_N2
    assert s % FFT_N2 == 0 and s % RET_CHUNK == 0
    tm_in, tm_tail = _tiles(s)
    cos, sin = (jnp.asarray(t) for t in _rope_tables(s, dk))
    dft = jnp.asarray(_chan_dft(gd)).astype(BF16)
    tab_a = jnp.asarray(_fft_a_table(n1, FFT_N2)).astype(BF16)
    tab_b = jnp.asarray(_fft_b_table(FFT_N2, float(s * gd) ** -0.5)).astype(BF16)

    for l in range(p["w_in"].shape[0]):
        wqk, wvo = _mem_kv(mem, p["norm_mem_w"][l], p["w_ck"][l], p["w_cv"][l],
                           p["w_cq"][l], p["w_co"][l])
        q, k, v, g, z, gate_r, gate_f, sb = _in_proj(x, p["norm_mix_w"][l], p["w_in"][l], cos, sin,
                                                     p["ret_decay_bwd"][l], tm_in)
        r, four = _mixers(q, k, v, g, sb, z, p["ret_decay_fwd"][l], p["ret_decay_bwd"][l],
                          dft, tab_a, tab_b)
        w_ro = _gain_rows(p["w_ret_out"][l], p["ret_gn_w"][l])
        x = _tail(x, r, four, gate_r, gate_f, wqk, wvo, w_ro, p["w_four_out"][l],
                  p["w_mix_out"][l], p["norm_ca_w"][l], p["norm_mlp_w"][l], p["w_up"][l],
                  p["w_down"][l], p["norm_final_w"], tm_tail)
    return x


def kernel(x_prompt, x_sample, mem_prompt, mem_sample, norm_mix_w, w_in, ret_decay_fwd,
           ret_decay_bwd, ret_gn_w, w_ret_out, w_four_out, w_mix_out, norm_ca_w, norm_mem_w,
           w_cq, w_ck, w_cv, w_co, norm_mlp_w, w_up, w_down, norm_final_w):
    depth, d = norm_mix_w.shape
    assert depth == 1, "the tail fuses the final norm into the (single) layer"
    vec = lambda w: w.reshape(depth, 1, d)
    mat = lambda w: w.astype(BF16)
    p = dict(
        norm_mix_w=vec(norm_mix_w), w_in=mat(w_in), ret_decay_fwd=ret_decay_fwd,
        ret_decay_bwd=ret_decay_bwd, ret_gn_w=ret_gn_w, w_ret_out=w_ret_out,
        w_four_out=mat(w_four_out), w_mix_out=mat(w_mix_out), norm_ca_w=vec(norm_ca_w),
        norm_mem_w=vec(norm_mem_w), w_cq=mat(w_cq), w_ck=mat(w_ck), w_cv=mat(w_cv),
        w_co=mat(w_co), norm_mlp_w=vec(norm_mlp_w), w_up=mat(w_up), w_down=mat(w_down),
        norm_final_w=norm_final_w.reshape(1, d))
    return _trunk(x_prompt, mem_prompt, p), _trunk(x_sample, mem_sample, p)
```

```python
import functools

import numpy as np
import jax
import jax.numpy as jnp
from jax import lax
from jax.experimental import pallas as pl
from jax.experimental.pallas import tpu as pltpu

RET_HEADS = 4
FOUR_GROUPS = 4
CA_HEADS = 4
ROPE_THETA = 10000.0
EPS = 1e-6
GN_EPS = 1e-5

V7X_VMEM_BYTES = 64 * 1024 * 1024
V7X_MXU_DIM = 256
V7X_LANES = 128
SUBLANES = 8

RET_CHUNK = V7X_MXU_DIM
FFT_N2 = 128
ROW_BLOCKS = 2

BF16 = jnp.bfloat16
F32 = jnp.float32


COMPILER_SCRATCH_BYTES = 8 * 1024 * 1024
VMEM_RESERVE_BYTES = 6 * 1024 * 1024


def _vmem_limit(nbytes):
    return int(min(nbytes + COMPILER_SCRATCH_BYTES, V7X_VMEM_BYTES - VMEM_RESERVE_BYTES))


def _rms(x, w):
    return x * lax.rsqrt(jnp.mean(x * x, axis=-1, keepdims=True) + EPS) * w


def _dot(a, b):
    return jnp.dot(a, b, preferred_element_type=F32)


def _words(x):
    return pltpu.bitcast(x.astype(BF16), jnp.int32)


def _rows(w):
    return pltpu.bitcast(w, BF16)


def _resident(shape):
    return pl.BlockSpec(shape, lambda *_: (0,) * len(shape), pipeline_mode=pl.Buffered(1))


@functools.lru_cache(maxsize=None)
def _rope_tables(s, dk):
    inv = ROPE_THETA ** (-np.arange(0, dk, 2, dtype=np.float64) / dk)
    ang = np.arange(s, dtype=np.float64)[:, None] * inv[None, :]
    return np.cos(ang).astype(np.float32), np.sin(ang).astype(np.float32)


@functools.lru_cache(maxsize=None)
def _chan_dft(n):
    ang = 2.0 * np.pi * np.outer(np.arange(n), np.arange(n)) / n
    return np.concatenate([np.cos(ang), -np.sin(ang)], axis=1).astype(np.float32)


@functools.lru_cache(maxsize=None)
def _fft_a_table(n1, n2):
    n = n1 * n2
    k1 = np.arange(n1, dtype=np.float64)[None, :, None]
    tok = (n2 * np.arange(n1, dtype=np.float64)[None, None, :]
           + np.arange(n2, dtype=np.float64)[:, None, None])
    ang = 2.0 * np.pi * np.mod(k1 * tok, n) / n
    c, s = np.cos(ang), np.sin(ang)
    re_rows = np.concatenate([c, s], axis=2)
    im_rows = np.concatenate([-s, c], axis=2)
    return np.stack([re_rows, im_rows], axis=2).reshape(n2, 2 * n1, 2 * n1).astype(np.float32)


@functools.lru_cache(maxsize=None)
def _fft_b_table(n2, scale):
    ang = 2.0 * np.pi * np.outer(np.arange(n2), np.arange(n2)) / n2
    return (scale * np.stack([np.cos(ang), np.sin(ang)], axis=-1).reshape(n2, 2 * n2)).astype(np.float32)


def _mem_kv_kernel(mem_ref, nw_ref, wk_ref, wv_ref, wq_ref, wo_ref, wqk_ref, wvo_ref):
    m, d = mem_ref.shape[1], mem_ref.shape[2]
    dh = d // CA_HEADS
    mn = _rms(mem_ref[0], nw_ref[...]).astype(BF16)
    ck = _dot(mn, wk_ref[...]).astype(BF16)
    cv = _dot(mn, wv_ref[...]).astype(BF16)
    for h in range(CA_HEADS):
        cols = slice(h * dh, (h + 1) * dh)
        qk = lax.dot_general(wq_ref[:, cols], ck[:, cols], (((1,), (1,)), ((), ())),
                             preferred_element_type=F32) * (dh ** -0.5)
        wqk_ref[0, :, h * m:(h + 1) * m] = qk.astype(BF16)
        wvo_ref[0, h * m:(h + 1) * m, :] = _dot(cv[:, cols], wo_ref[cols, :]).astype(BF16)


def _mem_kv(mem, norm_w, w_ck, w_cv, w_cq, w_co):
    b, m, d = mem.shape
    hm = CA_HEADS * m
    sq = _resident((d, d))
    return pl.pallas_call(
        _mem_kv_kernel,
        out_shape=(jax.ShapeDtypeStruct((b, d, hm), BF16), jax.ShapeDtypeStruct((b, hm, d), BF16)),
        grid=(b,),
        in_specs=[pl.BlockSpec((1, m, d), lambda i: (i, 0, 0)), _resident((1, d)), sq, sq, sq, sq],
        out_specs=(pl.BlockSpec((1, d, hm), lambda i: (i, 0, 0)),
                   pl.BlockSpec((1, hm, d), lambda i: (i, 0, 0))),
        compiler_params=pltpu.CompilerParams(
            dimension_semantics=("arbitrary",),
            vmem_limit_bytes=_vmem_limit(4 * d * d * 2 + 4 * m * d * 4 + 4 * d * hm * 2
                                         + 2 * d * hm * 4)),
        name="mem_kv",
    )(mem, norm_w, w_ck, w_cv, w_cq, w_co)


def _log_sigmoid(x):
    return jnp.minimum(x, 0.0) - jnp.log1p(jnp.exp(-jnp.abs(x)))


def _tn_dot(a, b):
    return lax.dot_general(a, b, (((0,), (0,)), ((), ())), preferred_element_type=F32)


def _decay_rows(dec, c, dk, offset, sign):
    lg = _log_sigmoid(jnp.full((c, dk), dec, F32))
    row = lax.broadcasted_iota(jnp.int32, (c, dk), 0).astype(F32)
    return jnp.exp(lg * (offset + sign * row))


def _chunk_decay(dec, c, dk):
    return jnp.exp(_log_sigmoid(jnp.full((1, dk), dec, F32)) * c)


def _ret_bwd_init(dec_ref, state_ref, zeta_ref, gc_ref):
    c, dk = zeta_ref.shape[1], zeta_ref.shape[2]

    @pl.when(pl.program_id(1) == 0)
    def _init():
        state_ref[...] = jnp.zeros_like(state_ref)
        for h in range(RET_HEADS):
            zeta_ref[h] = _decay_rows(dec_ref[h], c, dk, 0.0, 1.0).astype(BF16)
            gc_ref[h] = _chunk_decay(dec_ref[h], c, dk)


def _ret_bwd_states(k_ref, v_ref, sb_ref, state_ref, zeta_ref, gc_ref):
    c = RET_CHUNK
    t_len, d = k_ref.shape[1], k_ref.shape[2]
    dk = d // RET_HEADS
    for h in range(RET_HEADS):
        cols = slice(h * dk, (h + 1) * dk)
        state = state_ref[h]
        for ci in reversed(range(t_len // c)):
            rows = slice(ci * c, (ci + 1) * c)
            sb_ref[0, ci, h] = state.astype(BF16)
            state = state * gc_ref[h] + _tn_dot(k_ref[0, rows, cols],
                                                v_ref[0, rows, cols] * zeta_ref[h])
        state_ref[h] = state


def _in_proj_kernel(dec_ref, x_ref, nw_ref, w_ref, cos_ref, sin_ref,
                    q_ref, k_ref, v_ref, g_ref, z_ref, gr_ref, gf_ref, sb_ref,
                    perm_ref, state_ref, zeta_ref, gc_ref):
    _ret_bwd_init(dec_ref, state_ref, zeta_ref, gc_ref)
    d = x_ref.shape[-1]
    dk = d // RET_HEADS
    half = dk // 2
    tm = x_ref.shape[1]
    hb = tm // ROW_BLOCKS
    blocks = [slice(i * hb, (i + 1) * hb) for i in range(ROW_BLOCKS)]
    xb = [_rms(x_ref[0, rows, :], nw_ref[...]).astype(BF16) for rows in blocks]

    def proj(j):
        return [_dot(b, w_ref[:, j * d:(j + 1) * d]) for b in xb]

    def rope(ps, out_ref, scale):
        for p, rows in zip(ps, blocks):
            cos = cos_ref[rows, :]
            sin = sin_ref[rows, :]
            for h in range(RET_HEADS):
                x1 = p[:, h * dk:h * dk + half]
                x2 = p[:, h * dk + half:(h + 1) * dk]
                o1 = x1 * cos - x2 * sin
                o2 = x2 * cos + x1 * sin
                if scale is not None:
                    o1 = o1 * scale
                    o2 = o2 * scale
                out_ref[0, rows, h * dk:h * dk + half] = o1.astype(BF16)
                out_ref[0, rows, h * dk + half:(h + 1) * dk] = o2.astype(BF16)

    def store(ps, out_ref):
        for p, rows in zip(ps, blocks):
            out_ref[0, rows, :] = p.astype(BF16)

    rope(proj(0), q_ref, dk ** -0.5)
    rope(proj(1), k_ref, None)
    store(proj(2), v_ref)
    _ret_bwd_states(k_ref, v_ref, sb_ref, state_ref, zeta_ref, gc_ref)
    store(proj(3), g_ref)
    us = proj(4)
    for lc in range(d // V7X_LANES):
        for blk in range(tm // FFT_N2):
            a, par = divmod(blk, 2)
            u = us[blk * FFT_N2 // hb]
            r0 = blk * FFT_N2 % hb
            perm_ref[lc, pl.ds(2 * FFT_N2 * a + par, FFT_N2, stride=2), :] = (
                u[r0:r0 + FFT_N2, lc * V7X_LANES:(lc + 1) * V7X_LANES])
    for lc in range(d // V7X_LANES):
        z_ref[0, :, lc * V7X_LANES:(lc + 1) * V7X_LANES] = _words(perm_ref[lc])
    store(proj(5), gr_ref)
    store(proj(6), gf_ref)


def _in_proj(x, norm_w, w_in, cos, sin, decay_bwd, tm):
    b, s, d = x.shape
    dk = d // RET_HEADS
    half = dk // 2
    c = RET_CHUNK
    nt = s // tm
    act = jax.ShapeDtypeStruct((b, s, d), BF16)
    assert tm % (2 * FFT_N2) == 0 and tm % c == 0 and (tm // ROW_BLOCKS) % FFT_N2 == 0
    zsh = jax.ShapeDtypeStruct((b, s // 2, d), jnp.int32)
    sbsh = jax.ShapeDtypeStruct((b, s // c, RET_HEADS, dk, dk), BF16)
    row = pl.BlockSpec((1, tm, d), lambda i, j: (i, nt - 1 - j, 0))
    zrow = pl.BlockSpec((1, tm // 2, d), lambda i, j: (i, nt - 1 - j, 0))
    sbrow = pl.BlockSpec((1, tm // c, RET_HEADS, dk, dk), lambda i, j: (i, nt - 1 - j, 0, 0, 0))
    tab = pl.BlockSpec((tm, half), lambda i, j: (nt - 1 - j, 0))
    vmem = (w_in.size * 2 + 2 * tm * d * 4 + 2 * 8 * tm * d * 2 + 4 * tm * half * 4
            + 5 * tm * d * 4 + 2 * tm * d * 2 + RET_HEADS * (dk * dk * 4 + c * dk * 2))
    return pl.pallas_call(
        _in_proj_kernel,
        out_shape=(act, act, act, act, zsh, act, act, sbsh),
        grid=(b, nt),
        in_specs=[pl.BlockSpec(memory_space=pltpu.SMEM), row, _resident((1, d)),
                  _resident(w_in.shape), tab, tab],
        out_specs=(row, row, row, row, zrow, row, row, sbrow),
        scratch_shapes=[pltpu.VMEM((d // V7X_LANES, tm, V7X_LANES), F32),
                        pltpu.VMEM((RET_HEADS, dk, dk), F32),
                        pltpu.VMEM((RET_HEADS, c, dk), BF16),
                        pltpu.VMEM((RET_HEADS, 1, dk), F32)],
        compiler_params=pltpu.CompilerParams(
            dimension_semantics=("arbitrary", "arbitrary"),
            vmem_limit_bytes=_vmem_limit(vmem)),
        name="in_proj",
    )(decay_bwd, x, norm_w, w_in, cos, sin)


def _ret_fwd_kernel(decf_ref, decb_ref, q_ref, k_ref, v_ref, sb_ref, r_ref,
                    state_ref, dmat_ref, xif_ref, xib_ref, zeta_ref, gc_ref):
    c = RET_CHUNK
    t_len, d = q_ref.shape[1], q_ref.shape[2]
    dk = d // RET_HEADS

    @pl.when(pl.program_id(1) == 0)
    def _init():
        state_ref[...] = jnp.zeros_like(state_ref)
        for h in range(RET_HEADS):
            xif_ref[h] = _decay_rows(decf_ref[h], c, dk, 1.0, 1.0).astype(BF16)
            xib_ref[h] = _decay_rows(decb_ref[h], c, dk, float(c), -1.0).astype(BF16)
            zeta_ref[h] = _decay_rows(decf_ref[h], c, dk, c - 1.0, -1.0).astype(BF16)
            gc_ref[h] = _chunk_decay(decf_ref[h], c, dk)
            i = lax.broadcasted_iota(jnp.int32, (c, c), 0)
            j = lax.broadcasted_iota(jnp.int32, (c, c), 1)
            diff = (i - j).astype(F32)
            lgf = _log_sigmoid(jnp.full((c, c), decf_ref[h], F32))
            lgb = _log_sigmoid(jnp.full((c, c), decb_ref[h], F32))
            dmat_ref[h] = jnp.where(i >= j, jnp.exp(lgf * jnp.maximum(diff, 0.0)),
                                    jnp.exp(lgb * jnp.maximum(-diff, 0.0)))

    for h in range(RET_HEADS):
        cols = slice(h * dk, (h + 1) * dk)
        state = state_ref[h]
        for ci in range(t_len // c):
            rows = slice(ci * c, (ci + 1) * c)
            q = q_ref[0, rows, cols]
            k = k_ref[0, rows, cols]
            v = v_ref[0, rows, cols]
            scores = lax.dot_general(q, k, (((1,), (1,)), ((), ())), preferred_element_type=F32)
            lhs = jnp.concatenate([(scores * dmat_ref[h]).astype(BF16),
                                   q * xif_ref[h], q * xib_ref[h]], axis=1)
            rhs = jnp.concatenate([v, state.astype(BF16), sb_ref[0, ci, h]], axis=0)
            y = _dot(lhs, rhs)
            state = state * gc_ref[h] + _tn_dot(k, v * zeta_ref[h])
            yc = y - jnp.mean(y, axis=-1, keepdims=True)
            var = jnp.mean(yc * yc, axis=-1, keepdims=True)
            r_ref[0, rows, cols] = (yc * lax.rsqrt(var + GN_EPS)).astype(BF16)
        state_ref[h] = state


def _fft_a_kernel(z_ref, dft_ref, tab_ref, o_ref):
    rn, d = z_ref.shape[2], z_ref.shape[3]
    gd = dft_ref.shape[0]
    u = jnp.concatenate([_rows(z_ref[0, :, r, :]) for r in range(rn)], axis=0)
    n1 = u.shape[0] // rn
    parts = [_dot(u[:, gi * gd:(gi + 1) * gd], dft_ref[...]) for gi in range(d // gd)]
    re = jnp.concatenate([p[:, :gd] for p in parts], axis=1).astype(BF16)
    im = jnp.concatenate([p[:, gd:] for p in parts], axis=1).astype(BF16)
    for r in range(rn):
        rows = slice(r * n1, (r + 1) * n1)
        zz = jnp.concatenate([re[rows], im[rows]], axis=0)
        o_ref[0, :, r, :] = _words(_dot(tab_ref[r], zz))


def _fft_b_kernel(zt_ref, tab_ref, o_ref):
    for i in range(zt_ref.shape[1]):
        o_ref[0, :, i, :] = _words(_dot(tab_ref[...], _rows(zt_ref[0, i])))


def _ret_fwd_fft_b_kernel(decf_ref, decb_ref, q_ref, k_ref, v_ref, sb_ref,
                          zt_ref, tab_ref, r_ref, f_ref,
                          state_ref, dmat_ref, xif_ref, xib_ref, zeta_ref, gc_ref):
    _ret_fwd_kernel(decf_ref, decb_ref, q_ref, k_ref, v_ref, sb_ref, r_ref,
                    state_ref, dmat_ref, xif_ref, xib_ref, zeta_ref, gc_ref)
    _fft_b_kernel(zt_ref, tab_ref, f_ref)


def _mixers(q, k, v, sb, z, decay_fwd, decay_bwd, dft, tab_a, tab_b):
    b, s, d = q.shape
    dk = d // RET_HEADS
    c = RET_CHUNK
    n2 = tab_b.shape[0]
    n1 = s // n2
    steps_b = n1 // SUBLANES
    t_b = s // steps_b
    assert t_b % c == 0
    smem = pl.BlockSpec(memory_space=pltpu.SMEM)
    fwd = pl.BlockSpec((1, t_b, d), lambda i, t: (i, t, 0))
    xi = pltpu.VMEM((RET_HEADS, c, dk), BF16)
    table_bytes = RET_HEADS * (3 * c * dk + c * c + dk * dk + dk) * 4
    stage_b_bytes = 2 * (SUBLANES * n2 * d * 4 + (n2 // 2) * SUBLANES * d * 4) + 4 * n2 * d * 4
    zt = pl.pallas_call(
        _fft_a_kernel,
        out_shape=jax.ShapeDtypeStruct((b, n1, n2, d), jnp.int32),
        grid=(b, n2 // SUBLANES),
        in_specs=[pl.BlockSpec((1, n1 // 2, SUBLANES, d), lambda i, t: (i, 0, t, 0)),
                  _resident(dft.shape),
                  pl.BlockSpec((SUBLANES, 2 * n1, 2 * n1), lambda i, t: (t, 0, 0))],
        out_specs=pl.BlockSpec((1, n1, SUBLANES, d), lambda i, t: (i, 0, t, 0)),
        compiler_params=pltpu.CompilerParams(
            dimension_semantics=("parallel", "parallel"),
            vmem_limit_bytes=_vmem_limit(2 * (3 * n1 * SUBLANES * d * 2
                                              + SUBLANES * 4 * n1 * n1 * 2)
                                         + 5 * n1 * SUBLANES * d * 4)),
        name="fft_a",
    )(z.reshape(b, n1 // 2, n2, d), dft, tab_a)
    return pl.pallas_call(
        _ret_fwd_fft_b_kernel,
        out_shape=(jax.ShapeDtypeStruct((b, s, d), BF16),
                   jax.ShapeDtypeStruct((b, n2 // 2, n1, d), jnp.int32)),
        grid=(b, steps_b),
        in_specs=[smem, smem, fwd, fwd, fwd,
                  pl.BlockSpec((1, t_b // c, RET_HEADS, dk, dk), lambda i, t: (i, t, 0, 0, 0)),
                  pl.BlockSpec((1, SUBLANES, n2, d), lambda i, t: (i, t, 0, 0)),
                  _resident((n2, 2 * n2))],
        out_specs=(fwd, pl.BlockSpec((1, n2 // 2, SUBLANES, d), lambda i, t: (i, 0, t, 0))),
        scratch_shapes=[pltpu.VMEM((RET_HEADS, dk, dk), F32), pltpu.VMEM((RET_HEADS, c, c), F32),
                        xi, xi, pltpu.VMEM((RET_HEADS, c, dk), BF16),
                        pltpu.VMEM((RET_HEADS, 1, dk), F32)],
        compiler_params=pltpu.CompilerParams(
            dimension_semantics=("arbitrary", "arbitrary"),
            vmem_limit_bytes=_vmem_limit(2 * 5 * t_b * d * 2 + table_bytes + stage_b_bytes)),
        name="ret_fwd_fft_b",
    )(decay_fwd, decay_bwd, q, k, v, sb, zt, tab_b)


def _gain_rows_kernel(w_ref, g_ref, o_ref):
    o_ref[...] = (w_ref[...] * g_ref[...]).astype(BF16)


def _gain_rows(w, gain):
    d_in, d_out = w.shape
    rows = min(d_in, V7X_MXU_DIM)
    return pl.pallas_call(
        _gain_rows_kernel,
        out_shape=jax.ShapeDtypeStruct((d_in, d_out), BF16),
        grid=(d_in // rows,),
        in_specs=[pl.BlockSpec((rows, d_out), lambda i: (i, 0)),
                  pl.BlockSpec((rows, 1), lambda i: (i, 0))],
        out_specs=pl.BlockSpec((rows, d_out), lambda i: (i, 0)),
        compiler_params=pltpu.CompilerParams(dimension_semantics=("parallel",)),
        name="gain_rows",
    )(w, gain.reshape(d_in, 1))


def _tail_kernel(x_ref, r_ref, g_ref, f_ref, gr_ref, gf_ref, wqk_ref, wvo_ref,
                 w_ro_ref, w_fo_ref, w_mo_ref, n_ca_ref,
                 n_mlp_ref, w_up_ref, w_dn_ref, n_fin_ref, o_ref, att_ref, perm_ref):
    d = x_ref.shape[-1]
    m = wqk_ref.shape[2] // CA_HEADS
    ff = w_up_ref.shape[1]
    gq, n1 = f_ref.shape[1], f_ref.shape[2]
    hb = x_ref.shape[1] // ROW_BLOCKS
    lanes = range(d // V7X_LANES)
    blocks = [slice(i * hb, (i + 1) * hb) for i in range(ROW_BLOCKS)]
    def gated(rows):
        g = g_ref[0, rows, :]
        return g * jax.nn.sigmoid(g) * r_ref[0, rows, :]

    ret = [_dot(gated(rows), w_ro_ref[...]) for rows in blocks]
    for i, rows in enumerate(blocks):
        g0, g1 = i * gq // ROW_BLOCKS, (i + 1) * gq // ROW_BLOCKS
        fperm = _dot(_rows(f_ref[0, g0:g1].reshape(hb // 2, d)), w_fo_ref[...])
        for lc in lanes:
            perm_ref[lc, rows, :] = fperm[:, lc * V7X_LANES:(lc + 1) * V7X_LANES]
    xs = []
    for i, rows in enumerate(blocks):
        fou = jnp.concatenate(
            [jnp.concatenate([perm_ref[lc, pl.ds(i * hb + blk // 2 * 2 * n1 + blk % 2, n1,
                                                 stride=2), :] for lc in lanes], axis=1)
             for blk in range(2 * gq // ROW_BLOCKS)], axis=0)
        merged = (jax.nn.sigmoid(gr_ref[0, rows, :].astype(F32)) * ret[i]
                  + jax.nn.sigmoid(gf_ref[0, rows, :].astype(F32)) * fou)
        xs.append(x_ref[0, rows, :] + _dot(merged.astype(BF16), w_mo_ref[...]))

    logits = [_dot(_rms(x, n_ca_ref[...]).astype(BF16), wqk_ref[0]) for x in xs]
    for lg, rows in zip(logits, blocks):
        for h in range(CA_HEADS):
            cols = slice(h * m, (h + 1) * m)
            p = jnp.exp(lg[:, cols] - jnp.max(lg[:, cols], axis=-1, keepdims=True))
            att_ref[rows, cols] = (p / jnp.sum(p, axis=-1, keepdims=True)).astype(BF16)
    xs = [x + _dot(att_ref[rows, :], wvo_ref[0]) for x, rows in zip(xs, blocks)]

    xb = [_rms(x, n_mlp_ref[...]).astype(BF16) for x in xs]
    for j in range(ff // d):
        cols = slice(j * d, (j + 1) * d)
        hid = [jnp.square(jnp.maximum(_dot(b, w_up_ref[:, cols]), 0.0)).astype(BF16) for b in xb]
        xs = [x + _dot(hd, w_dn_ref[cols, :]) for x, hd in zip(xs, hid)]
    for x, rows in zip(xs, blocks):
        o_ref[0, rows, :] = _rms(x, n_fin_ref[...])


def _tail(x, r, g, four, gate_r, gate_f, wqk, wvo, w_ro, w_fo, w_mo, n_ca,
          n_mlp, w_up, w_dn, n_fin, tm):
    b, s, d = x.shape
    hm = wqk.shape[2]
    ff = w_up.shape[1]
    row = pl.BlockSpec((1, tm, d), lambda i, j: (i, j, 0))
    n1 = four.shape[2]
    assert tm % (2 * n1 * ROW_BLOCKS) == 0
    half = pl.BlockSpec((1, tm // (2 * n1), n1, d), lambda i, j: (i, j, 0, 0))
    vec = _resident((1, d))
    sq = _resident((d, d))
    vmem = ((3 * d * d + 2 * d * ff) * 2 + 4 * d * hm * 2 + 2 * tm * d * (4 + 4 + 5 * 2)
            + tm * d * 4 * 8)
    return pl.pallas_call(
        _tail_kernel,
        out_shape=jax.ShapeDtypeStruct((b, s, d), F32),
        grid=(b, s // tm),
        in_specs=[row, row, row, half, row, row,
                  pl.BlockSpec((1, d, hm), lambda i, j: (i, 0, 0)),
                  pl.BlockSpec((1, hm, d), lambda i, j: (i, 0, 0)),
                  sq, sq, sq, vec, vec, _resident((d, ff)), _resident((ff, d)), vec],
        out_specs=row,
        scratch_shapes=[pltpu.VMEM((tm, hm), BF16),
                        pltpu.VMEM((d // V7X_LANES, tm, V7X_LANES), F32)],
        compiler_params=pltpu.CompilerParams(
            dimension_semantics=("parallel", "parallel"),
            vmem_limit_bytes=_vmem_limit(vmem)),
        name="tail",
    )(x, r, g, four, gate_r, gate_f, wqk, wvo, w_ro, w_fo, w_mo, n_ca, n_mlp, w_up, w_dn, n_fin)


def _tiles(s):
    return min(512, s), min(512, s)


def _trunk(x, mem, p):
    b, s, d = x.shape
    dk = d // RET_HEADS
    gd = d // FOUR_GROUPS
    n1 = s // FFT_N2
    assert s % FFT_N2 == 0 and s % RET_CHUNK == 0
    tm_in, tm_tail = _tiles(s)
    cos, sin = (jnp.asarray(t) for t in _rope_tables(s, dk))
    dft = jnp.asarray(_chan_dft(gd)).astype(BF16)
    tab_a = jnp.asarray(_fft_a_table(n1, FFT_N2)).astype(BF16)
    tab_b = jnp.asarray(_fft_b_table(FFT_N2, float(s * gd) ** -0.5)).astype(BF16)

    for l in range(p["w_in"].shape[0]):
        wqk, wvo = _mem_kv(mem, p["norm_mem_w"][l], p["w_ck"][l], p["w_cv"][l],
                           p["w_cq"][l], p["w_co"][l])
        q, k, v, g, z, gate_r, gate_f, sb = _in_proj(x, p["norm_mix_w"][l], p["w_in"][l], cos, sin,
                                                     p["ret_decay_bwd"][l], tm_in)
        r, four = _mixers(q, k, v, sb, z, p["ret_decay_fwd"][l], p["ret_decay_bwd"][l],
                          dft, tab_a, tab_b)
        w_ro = _gain_rows(p["w_ret_out"][l], p["ret_gn_w"][l])
        x = _tail(x, r, g, four, gate_r, gate_f, wqk, wvo, w_ro, p["w_four_out"][l],
                  p["w_mix_out"][l], p["norm_ca_w"][l], p["norm_mlp_w"][l], p["w_up"][l],
                  p["w_down"][l], p["norm_final_w"], tm_tail)
    return x


def kernel(x_prompt, x_sample, mem_prompt, mem_sample, norm_mix_w, w_in, ret_decay_fwd,
           ret_decay_bwd, ret_gn_w, w_ret_out, w_four_out, w_mix_out, norm_ca_w, norm_mem_w,
           w_cq, w_ck, w_cv, w_co, norm_mlp_w, w_up, w_down, norm_final_w):
    depth, d = norm_mix_w.shape
    assert depth == 1, "the tail fuses the final norm into the (single) layer"
    vec = lambda w: w.reshape(depth, 1, d)
    mat = lambda w: w.astype(BF16)
    p = dict(
        norm_mix_w=vec(norm_mix_w), w_in=mat(w_in), ret_decay_fwd=ret_decay_fwd,
        ret_decay_bwd=ret_decay_bwd, ret_gn_w=ret_gn_w, w_ret_out=w_ret_out,
        w_four_out=mat(w_four_out), w_mix_out=mat(w_mix_out), norm_ca_w=vec(norm_ca_w),
        norm_mem_w=vec(norm_mem_w), w_cq=mat(w_cq), w_ck=mat(w_ck), w_cv=mat(w_cv),
        w_co=mat(w_co), norm_mlp_w=vec(norm_mlp_w), w_up=mat(w_up), w_down=mat(w_down),
        norm_final_w=norm_final_w.reshape(1, d))
    return _trunk(x_prompt, mem_prompt, p), _trunk(x_sample, mem_sample, p)
```

```python
import functools

import numpy as np
import jax
import jax.numpy as jnp
from jax import lax
from jax.experimental import pallas as pl
from jax.experimental.pallas import tpu as pltpu

RET_HEADS = 4
FOUR_GROUPS = 4
CA_HEADS = 4
ROPE_THETA = 10000.0
EPS = 1e-6
GN_EPS = 1e-5

V7X_VMEM_BYTES = 64 * 1024 * 1024
V7X_MXU_DIM = 256
V7X_LANES = 128
SUBLANES = 8

RET_CHUNK = V7X_MXU_DIM
FFT_N2 = 128
ROW_BLOCKS = 2

BF16 = jnp.bfloat16
F32 = jnp.float32


COMPILER_SCRATCH_BYTES = 8 * 1024 * 1024
VMEM_RESERVE_BYTES = 6 * 1024 * 1024


def _vmem_limit(nbytes):
    return int(min(nbytes + COMPILER_SCRATCH_BYTES, V7X_VMEM_BYTES - VMEM_RESERVE_BYTES))


def _rms(x, w):
    return x * lax.rsqrt(jnp.mean(x * x, axis=-1, keepdims=True) + EPS) * w


def _dot(a, b):
    return jnp.dot(a, b, preferred_element_type=F32)


def _words(x):
    return pltpu.bitcast(x.astype(BF16), jnp.int32)


def _rows(w):
    return pltpu.bitcast(w, BF16)


def _resident(shape):
    return pl.BlockSpec(shape, lambda *_: (0,) * len(shape), pipeline_mode=pl.Buffered(1))


@functools.lru_cache(maxsize=None)
def _rope_tables(s, dk):
    inv = ROPE_THETA ** (-np.arange(0, dk, 2, dtype=np.float64) / dk)
    ang = np.arange(s, dtype=np.float64)[:, None] * inv[None, :]
    return np.cos(ang).astype(np.float32), np.sin(ang).astype(np.float32)


@functools.lru_cache(maxsize=None)
def _chan_dft(n):
    ang = 2.0 * np.pi * np.outer(np.arange(n), np.arange(n)) / n
    return np.concatenate([np.cos(ang), -np.sin(ang)], axis=1).astype(np.float32)


@functools.lru_cache(maxsize=None)
def _fft_a_table(n1, n2):
    n = n1 * n2
    k1 = np.arange(n1, dtype=np.float64)[None, :, None]
    tok = (n2 * np.arange(n1, dtype=np.float64)[None, None, :]
           + np.arange(n2, dtype=np.float64)[:, None, None])
    ang = 2.0 * np.pi * np.mod(k1 * tok, n) / n
    c, s = np.cos(ang), np.sin(ang)
    re_rows = np.concatenate([c, s], axis=2)
    im_rows = np.concatenate([-s, c], axis=2)
    return np.stack([re_rows, im_rows], axis=2).reshape(n2, 2 * n1, 2 * n1).astype(np.float32)


@functools.lru_cache(maxsize=None)
def _fft_b_table(n2, scale):
    ang = 2.0 * np.pi * np.outer(np.arange(n2), np.arange(n2)) / n2
    return (scale * np.stack([np.cos(ang), np.sin(ang)], axis=-1).reshape(n2, 2 * n2)).astype(np.float32)


def _mem_kv_kernel(mem_ref, nw_ref, wk_ref, wv_ref, wq_ref, wo_ref, wqk_ref, wvo_ref):
    m, d = mem_ref.shape[1], mem_ref.shape[2]
    dh = d // CA_HEADS
    mn = _rms(mem_ref[0], nw_ref[...]).astype(BF16)
    ck = _dot(mn, wk_ref[...]).astype(BF16)
    cv = _dot(mn, wv_ref[...]).astype(BF16)
    for h in range(CA_HEADS):
        cols = slice(h * dh, (h + 1) * dh)
        qk = lax.dot_general(wq_ref[:, cols], ck[:, cols], (((1,), (1,)), ((), ())),
                             preferred_element_type=F32) * (dh ** -0.5)
        wqk_ref[0, :, h * m:(h + 1) * m] = qk.astype(BF16)
        wvo_ref[0, h * m:(h + 1) * m, :] = _dot(cv[:, cols], wo_ref[cols, :]).astype(BF16)


def _mem_kv(mem, norm_w, w_ck, w_cv, w_cq, w_co):
    b, m, d = mem.shape
    hm = CA_HEADS * m
    sq = _resident((d, d))
    return pl.pallas_call(
        _mem_kv_kernel,
        out_shape=(jax.ShapeDtypeStruct((b, d, hm), BF16), jax.ShapeDtypeStruct((b, hm, d), BF16)),
        grid=(b,),
        in_specs=[pl.BlockSpec((1, m, d), lambda i: (i, 0, 0)), _resident((1, d)), sq, sq, sq, sq],
        out_specs=(pl.BlockSpec((1, d, hm), lambda i: (i, 0, 0)),
                   pl.BlockSpec((1, hm, d), lambda i: (i, 0, 0))),
        compiler_params=pltpu.CompilerParams(
            dimension_semantics=("arbitrary",),
            vmem_limit_bytes=_vmem_limit(4 * d * d * 2 + 4 * m * d * 4 + 4 * d * hm * 2
                                         + 2 * d * hm * 4)),
        name="mem_kv",
    )(mem, norm_w, w_ck, w_cv, w_cq, w_co)


def _log_sigmoid(x):
    return jnp.minimum(x, 0.0) - jnp.log1p(jnp.exp(-jnp.abs(x)))


def _tn_dot(a, b):
    return lax.dot_general(a, b, (((0,), (0,)), ((), ())), preferred_element_type=F32)


def _decay_rows(dec, c, dk, offset, sign):
    lg = _log_sigmoid(jnp.full((c, dk), dec, F32))
    row = lax.broadcasted_iota(jnp.int32, (c, dk), 0).astype(F32)
    return jnp.exp(lg * (offset + sign * row))


def _chunk_decay(dec, c, dk):
    return jnp.exp(_log_sigmoid(jnp.full((1, dk), dec, F32)) * c)


def _ret_bwd_init(dec_ref, state_ref, zeta_ref, gc_ref):
    c, dk = zeta_ref.shape[1], zeta_ref.shape[2]

    @pl.when(pl.program_id(1) == 0)
    def _init():
        state_ref[...] = jnp.zeros_like(state_ref)
        for h in range(RET_HEADS):
            zeta_ref[h] = _decay_rows(dec_ref[h], c, dk, 0.0, 1.0).astype(BF16)
            gc_ref[h] = _chunk_decay(dec_ref[h], c, dk)


def _ret_bwd_states(k_ref, v_ref, sb_ref, state_ref, zeta_ref, gc_ref):
    c = RET_CHUNK
    t_len, d = k_ref.shape[1], k_ref.shape[2]
    dk = d // RET_HEADS
    for h in range(RET_HEADS):
        cols = slice(h * dk, (h + 1) * dk)
        state = state_ref[h]
        for ci in reversed(range(t_len // c)):
            rows = slice(ci * c, (ci + 1) * c)
            sb_ref[0, ci, h] = state.astype(BF16)
            state = state * gc_ref[h] + _tn_dot(k_ref[0, rows, cols],
                                                v_ref[0, rows, cols] * zeta_ref[h])
        state_ref[h] = state


def _in_proj_kernel(dec_ref, x_ref, nw_ref, w_ref, cos_ref, sin_ref,
                    qkv_ref, gates_ref, z_ref, sb_ref,
                    perm_ref, state_ref, zeta_ref, gc_ref):
    _ret_bwd_init(dec_ref, state_ref, zeta_ref, gc_ref)
    q_ref, k_ref, v_ref = (qkv_ref.at[:, i] for i in range(3))
    g_ref, gr_ref, gf_ref = (gates_ref.at[:, i] for i in range(3))
    d = x_ref.shape[-1]
    dk = d // RET_HEADS
    half = dk // 2
    tm = x_ref.shape[1]
    hb = tm // ROW_BLOCKS
    blocks = [slice(i * hb, (i + 1) * hb) for i in range(ROW_BLOCKS)]
    xb = [_rms(x_ref[0, rows, :], nw_ref[...]).astype(BF16) for rows in blocks]

    def proj(j):
        return [_dot(b, w_ref[:, j * d:(j + 1) * d]) for b in xb]

    def rope(ps, out_ref, scale):
        for p, rows in zip(ps, blocks):
            cos = cos_ref[rows, :]
            sin = sin_ref[rows, :]
            for h in range(RET_HEADS):
                x1 = p[:, h * dk:h * dk + half]
                x2 = p[:, h * dk + half:(h + 1) * dk]
                o1 = x1 * cos - x2 * sin
                o2 = x2 * cos + x1 * sin
                if scale is not None:
                    o1 = o1 * scale
                    o2 = o2 * scale
                out_ref[0, rows, h * dk:h * dk + half] = o1.astype(BF16)
                out_ref[0, rows, h * dk + half:(h + 1) * dk] = o2.astype(BF16)

    def store(ps, out_ref):
        for p, rows in zip(ps, blocks):
            out_ref[0, rows, :] = p.astype(BF16)

    rope(proj(0), q_ref, dk ** -0.5)
    rope(proj(1), k_ref, None)
    store(proj(2), v_ref)
    _ret_bwd_states(k_ref, v_ref, sb_ref, state_ref, zeta_ref, gc_ref)
    store(proj(3), g_ref)
    us = proj(4)
    for lc in range(d // V7X_LANES):
        for blk in range(tm // FFT_N2):
            a, par = divmod(blk, 2)
            u = us[blk * FFT_N2 // hb]
            r0 = blk * FFT_N2 % hb
            perm_ref[lc, pl.ds(2 * FFT_N2 * a + par, FFT_N2, stride=2), :] = (
                u[r0:r0 + FFT_N2, lc * V7X_LANES:(lc + 1) * V7X_LANES])
    for lc in range(d // V7X_LANES):
        z_ref[0, :, lc * V7X_LANES:(lc + 1) * V7X_LANES] = _words(perm_ref[lc])
    store(proj(5), gr_ref)
    store(proj(6), gf_ref)


def _in_proj(x, norm_w, w_in, cos, sin, decay_bwd, tm):
    b, s, d = x.shape
    dk = d // RET_HEADS
    half = dk // 2
    c = RET_CHUNK
    nt = s // tm
    assert tm % (2 * FFT_N2) == 0 and tm % c == 0 and (tm // ROW_BLOCKS) % FFT_N2 == 0
    zsh = jax.ShapeDtypeStruct((b, s // 2, d), jnp.int32)
    sbsh = jax.ShapeDtypeStruct((b, s // c, RET_HEADS, dk, dk), BF16)
    tri = jax.ShapeDtypeStruct((b, 3, s, d), BF16)
    row = pl.BlockSpec((1, tm, d), lambda i, j: (i, nt - 1 - j, 0))
    trow = pl.BlockSpec((1, 3, tm, d), lambda i, j: (i, 0, nt - 1 - j, 0))
    zrow = pl.BlockSpec((1, tm // 2, d), lambda i, j: (i, nt - 1 - j, 0))
    sbrow = pl.BlockSpec((1, tm // c, RET_HEADS, dk, dk), lambda i, j: (i, nt - 1 - j, 0, 0, 0))
    tab = pl.BlockSpec((tm, half), lambda i, j: (nt - 1 - j, 0))
    vmem = (w_in.size * 2 + 2 * tm * d * 4 + 2 * 8 * tm * d * 2 + 4 * tm * half * 4
            + 5 * tm * d * 4 + 2 * tm * d * 2 + RET_HEADS * (dk * dk * 4 + c * dk * 2))
    return pl.pallas_call(
        _in_proj_kernel,
        out_shape=(tri, tri, zsh, sbsh),
        grid=(b, nt),
        in_specs=[pl.BlockSpec(memory_space=pltpu.SMEM), row, _resident((1, d)),
                  _resident(w_in.shape), tab, tab],
        out_specs=(trow, trow, zrow, sbrow),
        scratch_shapes=[pltpu.VMEM((d // V7X_LANES, tm, V7X_LANES), F32),
                        pltpu.VMEM((RET_HEADS, dk, dk), F32),
                        pltpu.VMEM((RET_HEADS, c, dk), BF16),
                        pltpu.VMEM((RET_HEADS, 1, dk), F32)],
        compiler_params=pltpu.CompilerParams(
            dimension_semantics=("arbitrary", "arbitrary"),
            vmem_limit_bytes=_vmem_limit(vmem)),
        name="in_proj",
    )(decay_bwd, x, norm_w, w_in, cos, sin)


def _ret_fwd_kernel(decf_ref, decb_ref, q_ref, k_ref, v_ref, sb_ref, r_ref,
                    state_ref, dmat_ref, xif_ref, xib_ref, zeta_ref, gc_ref):
    c = RET_CHUNK
    t_len, d = q_ref.shape[1], q_ref.shape[2]
    dk = d // RET_HEADS

    @pl.when(pl.program_id(1) == 0)
    def _init():
        state_ref[...] = jnp.zeros_like(state_ref)
        for h in range(RET_HEADS):
            xif_ref[h] = _decay_rows(decf_ref[h], c, dk, 1.0, 1.0).astype(BF16)
            xib_ref[h] = _decay_rows(decb_ref[h], c, dk, float(c), -1.0).astype(BF16)
            zeta_ref[h] = _decay_rows(decf_ref[h], c, dk, c - 1.0, -1.0).astype(BF16)
            gc_ref[h] = _chunk_decay(decf_ref[h], c, dk)
            i = lax.broadcasted_iota(jnp.int32, (c, c), 0)
            j = lax.broadcasted_iota(jnp.int32, (c, c), 1)
            diff = (i - j).astype(F32)
            lgf = _log_sigmoid(jnp.full((c, c), decf_ref[h], F32))
            lgb = _log_sigmoid(jnp.full((c, c), decb_ref[h], F32))
            dmat_ref[h] = jnp.where(i >= j, jnp.exp(lgf * jnp.maximum(diff, 0.0)),
                                    jnp.exp(lgb * jnp.maximum(-diff, 0.0)))

    for h in range(RET_HEADS):
        cols = slice(h * dk, (h + 1) * dk)
        state = state_ref[h]
        for ci in range(t_len // c):
            rows = slice(ci * c, (ci + 1) * c)
            q = q_ref[0, rows, cols]
            k = k_ref[0, rows, cols]
            v = v_ref[0, rows, cols]
            scores = lax.dot_general(q, k, (((1,), (1,)), ((), ())), preferred_element_type=F32)
            lhs = jnp.concatenate([(scores * dmat_ref[h]).astype(BF16),
                                   q * xif_ref[h], q * xib_ref[h]], axis=1)
            rhs = jnp.concatenate([v, state.astype(BF16), sb_ref[0, ci, h]], axis=0)
            y = _dot(lhs, rhs)
            state = state * gc_ref[h] + _tn_dot(k, v * zeta_ref[h])
            yc = y - jnp.mean(y, axis=-1, keepdims=True)
            var = jnp.mean(yc * yc, axis=-1, keepdims=True)
            r_ref[0, rows, cols] = (yc * lax.rsqrt(var + GN_EPS)).astype(BF16)
        state_ref[h] = state


def _fft_a_kernel(z_ref, dft_ref, tab_ref, o_ref):
    rn, d = z_ref.shape[2], z_ref.shape[3]
    gd = dft_ref.shape[0]
    u = jnp.concatenate([_rows(z_ref[0, :, r, :]) for r in range(rn)], axis=0)
    n1 = u.shape[0] // rn
    parts = [_dot(u[:, gi * gd:(gi + 1) * gd], dft_ref[...]) for gi in range(d // gd)]
    re = jnp.concatenate([p[:, :gd] for p in parts], axis=1).astype(BF16)
    im = jnp.concatenate([p[:, gd:] for p in parts], axis=1).astype(BF16)
    for r in range(rn):
        rows = slice(r * n1, (r + 1) * n1)
        zz = jnp.concatenate([re[rows], im[rows]], axis=0)
        o_ref[0, :, r, :] = _words(_dot(tab_ref[r], zz))


def _fft_b_kernel(zt_ref, tab_ref, o_ref):
    for i in range(zt_ref.shape[1]):
        o_ref[0, :, i, :] = _words(_dot(tab_ref[...], _rows(zt_ref[0, i])))


def _ret_fwd_fft_b_kernel(decf_ref, decb_ref, qkv_ref, sb_ref,
                          zt_ref, tab_ref, r_ref, f_ref,
                          state_ref, dmat_ref, xif_ref, xib_ref, zeta_ref, gc_ref):
    q_ref, k_ref, v_ref = (qkv_ref.at[:, i] for i in range(3))
    _ret_fwd_kernel(decf_ref, decb_ref, q_ref, k_ref, v_ref, sb_ref, r_ref,
                    state_ref, dmat_ref, xif_ref, xib_ref, zeta_ref, gc_ref)
    _fft_b_kernel(zt_ref, tab_ref, f_ref)


def _mixers(qkv, sb, z, decay_fwd, decay_bwd, dft, tab_a, tab_b):
    b, _, s, d = qkv.shape
    dk = d // RET_HEADS
    c = RET_CHUNK
    n2 = tab_b.shape[0]
    n1 = s // n2
    steps_b = n1 // SUBLANES
    t_b = s // steps_b
    assert t_b % c == 0
    smem = pl.BlockSpec(memory_space=pltpu.SMEM)
    fwd = pl.BlockSpec((1, t_b, d), lambda i, t: (i, t, 0))
    xi = pltpu.VMEM((RET_HEADS, c, dk), BF16)
    table_bytes = RET_HEADS * (3 * c * dk + c * c + dk * dk + dk) * 4
    stage_b_bytes = 2 * (SUBLANES * n2 * d * 4 + (n2 // 2) * SUBLANES * d * 4) + 4 * n2 * d * 4
    zt = pl.pallas_call(
        _fft_a_kernel,
        out_shape=jax.ShapeDtypeStruct((b, n1, n2, d), jnp.int32),
        grid=(b, n2 // SUBLANES),
        in_specs=[pl.BlockSpec((1, n1 // 2, SUBLANES, d), lambda i, t: (i, 0, t, 0)),
                  _resident(dft.shape),
                  pl.BlockSpec((SUBLANES, 2 * n1, 2 * n1), lambda i, t: (t, 0, 0))],
        out_specs=pl.BlockSpec((1, n1, SUBLANES, d), lambda i, t: (i, 0, t, 0)),
        compiler_params=pltpu.CompilerParams(
            dimension_semantics=("parallel", "parallel"),
            vmem_limit_bytes=_vmem_limit(2 * (3 * n1 * SUBLANES * d * 2
                                              + SUBLANES * 4 * n1 * n1 * 2)
                                         + 5 * n1 * SUBLANES * d * 4)),
        name="fft_a",
    )(z.reshape(b, n1 // 2, n2, d), dft, tab_a)
    return pl.pallas_call(
        _ret_fwd_fft_b_kernel,
        out_shape=(jax.ShapeDtypeStruct((b, s, d), BF16),
                   jax.ShapeDtypeStruct((b, n2 // 2, n1, d), jnp.int32)),
        grid=(b, steps_b),
        in_specs=[smem, smem, pl.BlockSpec((1, 3, t_b, d), lambda i, t: (i, 0, t, 0)),
                  pl.BlockSpec((1, t_b // c, RET_HEADS, dk, dk), lambda i, t: (i, t, 0, 0, 0)),
                  pl.BlockSpec((1, SUBLANES, n2, d), lambda i, t: (i, t, 0, 0)),
                  _resident((n2, 2 * n2))],
        out_specs=(fwd, pl.BlockSpec((1, n2 // 2, SUBLANES, d), lambda i, t: (i, 0, t, 0))),
        scratch_shapes=[pltpu.VMEM((RET_HEADS, dk, dk), F32), pltpu.VMEM((RET_HEADS, c, c), F32),
                        xi, xi, pltpu.VMEM((RET_HEADS, c, dk), BF16),
                        pltpu.VMEM((RET_HEADS, 1, dk), F32)],
        compiler_params=pltpu.CompilerParams(
            dimension_semantics=("arbitrary", "arbitrary"),
            vmem_limit_bytes=_vmem_limit(2 * 5 * t_b * d * 2 + table_bytes + stage_b_bytes)),
        name="ret_fwd_fft_b",
    )(decay_fwd, decay_bwd, qkv, sb, zt, tab_b)


def _gain_rows_kernel(w_ref, g_ref, o_ref):
    o_ref[...] = (w_ref[...] * g_ref[...]).astype(BF16)


def _gain_rows(w, gain):
    d_in, d_out = w.shape
    rows = min(d_in, V7X_MXU_DIM)
    return pl.pallas_call(
        _gain_rows_kernel,
        out_shape=jax.ShapeDtypeStruct((d_in, d_out), BF16),
        grid=(d_in // rows,),
        in_specs=[pl.BlockSpec((rows, d_out), lambda i: (i, 0)),
                  pl.BlockSpec((rows, 1), lambda i: (i, 0))],
        out_specs=pl.BlockSpec((rows, d_out), lambda i: (i, 0)),
        compiler_params=pltpu.CompilerParams(dimension_semantics=("parallel",)),
        name="gain_rows",
    )(w, gain.reshape(d_in, 1))


def _tail_kernel(x_ref, r_ref, gates_ref, f_ref, wqk_ref, wvo_ref,
                 w_ro_ref, w_fo_ref, w_mo_ref, n_ca_ref,
                 n_mlp_ref, w_up_ref, w_dn_ref, n_fin_ref, o_ref, att_ref, perm_ref):
    g_ref, gr_ref, gf_ref = (gates_ref.at[:, i] for i in range(3))
    d = x_ref.shape[-1]
    m = wqk_ref.shape[2] // CA_HEADS
    ff = w_up_ref.shape[1]
    gq, n1 = f_ref.shape[1], f_ref.shape[2]
    hb = x_ref.shape[1] // ROW_BLOCKS
    lanes = range(d // V7X_LANES)
    blocks = [slice(i * hb, (i + 1) * hb) for i in range(ROW_BLOCKS)]
    def gated(rows):
        g = g_ref[0, rows, :]
        return g * jax.nn.sigmoid(g) * r_ref[0, rows, :]

    ret = [_dot(gated(rows), w_ro_ref[...]) for rows in blocks]
    for i, rows in enumerate(blocks):
        g0, g1 = i * gq // ROW_BLOCKS, (i + 1) * gq // ROW_BLOCKS
        fperm = _dot(_rows(f_ref[0, g0:g1].reshape(hb // 2, d)), w_fo_ref[...])
        for lc in lanes:
            perm_ref[lc, rows, :] = fperm[:, lc * V7X_LANES:(lc + 1) * V7X_LANES]
    xs = []
    for i, rows in enumerate(blocks):
        fou = jnp.concatenate(
            [jnp.concatenate([perm_ref[lc, pl.ds(i * hb + blk // 2 * 2 * n1 + blk % 2, n1,
                                                 stride=2), :] for lc in lanes], axis=1)
             for blk in range(2 * gq // ROW_BLOCKS)], axis=0)
        merged = (jax.nn.sigmoid(gr_ref[0, rows, :].astype(F32)) * ret[i]
                  + jax.nn.sigmoid(gf_ref[0, rows, :].astype(F32)) * fou)
        xs.append(x_ref[0, rows, :] + _dot(merged.astype(BF16), w_mo_ref[...]))

    logits = [_dot(_rms(x, n_ca_ref[...]).astype(BF16), wqk_ref[0]) for x in xs]
    for lg, rows in zip(logits, blocks):
        for h in range(CA_HEADS):
            cols = slice(h * m, (h + 1) * m)
            p = jnp.exp(lg[:, cols] - jnp.max(lg[:, cols], axis=-1, keepdims=True))
            att_ref[rows, cols] = (p / jnp.sum(p, axis=-1, keepdims=True)).astype(BF16)
    xs = [x + _dot(att_ref[rows, :], wvo_ref[0]) for x, rows in zip(xs, blocks)]

    xb = [_rms(x, n_mlp_ref[...]).astype(BF16) for x in xs]
    for j in range(ff // d):
        cols = slice(j * d, (j + 1) * d)
        hid = [jnp.square(jnp.maximum(_dot(b, w_up_ref[:, cols]), 0.0)).astype(BF16) for b in xb]
        xs = [x + _dot(hd, w_dn_ref[cols, :]) for x, hd in zip(xs, hid)]
    for x, rows in zip(xs, blocks):
        o_ref[0, rows, :] = _rms(x, n_fin_ref[...])


def _tail(x, r, gates, four, wqk, wvo, w_ro, w_fo, w_mo, n_ca,
          n_mlp, w_up, w_dn, n_fin, tm):
    b, s, d = x.shape
    hm = wqk.shape[2]
    ff = w_up.shape[1]
    row = pl.BlockSpec((1, tm, d), lambda i, j: (i, j, 0))
    n1 = four.shape[2]
    assert tm % (2 * n1 * ROW_BLOCKS) == 0
    half = pl.BlockSpec((1, tm // (2 * n1), n1, d), lambda i, j: (i, j, 0, 0))
    vec = _resident((1, d))
    sq = _resident((d, d))
    vmem = ((3 * d * d + 2 * d * ff) * 2 + 4 * d * hm * 2 + 2 * tm * d * (4 + 4 + 5 * 2)
            + tm * d * 4 * 8)
    return pl.pallas_call(
        _tail_kernel,
        out_shape=jax.ShapeDtypeStruct((b, s, d), F32),
        grid=(b, s // tm),
        in_specs=[row, row, pl.BlockSpec((1, 3, tm, d), lambda i, j: (i, 0, j, 0)), half,
                  pl.BlockSpec((1, d, hm), lambda i, j: (i, 0, 0)),
                  pl.BlockSpec((1, hm, d), lambda i, j: (i, 0, 0)),
                  sq, sq, sq, vec, vec, _resident((d, ff)), _resident((ff, d)), vec],
        out_specs=row,
        scratch_shapes=[pltpu.VMEM((tm, hm), BF16),
                        pltpu.VMEM((d // V7X_LANES, tm, V7X_LANES), F32)],
        compiler_params=pltpu.CompilerParams(
            dimension_semantics=("parallel", "parallel"),
            vmem_limit_bytes=_vmem_limit(vmem)),
        name="tail",
    )(x, r, gates, four, wqk, wvo, w_ro, w_fo, w_mo, n_ca, n_mlp, w_up, w_dn, n_fin)


def _tiles(s):
    return min(512, s), min(512, s)


def _trunk(x, mem, p):
    b, s, d = x.shape
    dk = d // RET_HEADS
    gd = d // FOUR_GROUPS
    n1 = s // FFT_N2
    assert s % FFT_N2 == 0 and s % RET_CHUNK == 0
    tm_in, tm_tail = _tiles(s)
    cos, sin = (jnp.asarray(t) for t in _rope_tables(s, dk))
    dft = jnp.asarray(_chan_dft(gd)).astype(BF16)
    tab_a = jnp.asarray(_fft_a_table(n1, FFT_N2)).astype(BF16)
    tab_b = jnp.asarray(_fft_b_table(FFT_N2, float(s * gd) ** -0.5)).astype(BF16)

    for l in range(p["w_in"].shape[0]):
        wqk, wvo = _mem_kv(mem, p["norm_mem_w"][l], p["w_ck"][l], p["w_cv"][l],
                           p["w_cq"][l], p["w_co"][l])
        qkv, gates, z, sb = _in_proj(x, p["norm_mix_w"][l], p["w_in"][l], cos, sin,
                                     p["ret_decay_bwd"][l], tm_in)
        r, four = _mixers(qkv, sb, z, p["ret_decay_fwd"][l], p["ret_decay_bwd"][l],
                          dft, tab_a, tab_b)
        w_ro = _gain_rows(p["w_ret_out"][l], p["ret_gn_w"][l])
        x = _tail(x, r, gates, four, wqk, wvo, w_ro, p["w_four_out"][l],
                  p["w_mix_out"][l], p["norm_ca_w"][l], p["norm_mlp_w"][l], p["w_up"][l],
                  p["w_down"][l], p["norm_final_w"], tm_tail)
    return x


def kernel(x_prompt, x_sample, mem_prompt, mem_sample, norm_mix_w, w_in, ret_decay_fwd,
           ret_decay_bwd, ret_gn_w, w_ret_out, w_four_out, w_mix_out, norm_ca_w, norm_mem_w,
           w_cq, w_ck, w_cv, w_co, norm_mlp_w, w_up, w_down, norm_final_w):
    depth, d = norm_mix_w.shape
    assert depth == 1, "the tail fuses the final norm into the (single) layer"
    vec = lambda w: w.reshape(depth, 1, d)
    mat = lambda w: w.astype(BF16)
    p = dict(
        norm_mix_w=vec(norm_mix_w), w_in=mat(w_in), ret_decay_fwd=ret_decay_fwd,
        ret_decay_bwd=ret_decay_bwd, ret_gn_w=ret_gn_w, w_ret_out=w_ret_out,
        w_four_out=mat(w_four_out), w_mix_out=mat(w_mix_out), norm_ca_w=vec(norm_ca_w),
        norm_mem_w=vec(norm_mem_w), w_cq=mat(w_cq), w_ck=mat(w_ck), w_cv=mat(w_cv),
        w_co=mat(w_co), norm_mlp_w=vec(norm_mlp_w), w_up=mat(w_up), w_down=mat(w_down),
        norm_final_w=norm_final_w.reshape(1, d))
    return _trunk(x_prompt, mem_prompt, p), _trunk(x_sample, mem_sample, p)
```

```python
import functools

import numpy as np
import jax
import jax.numpy as jnp
from jax import lax
from jax.experimental import pallas as pl
from jax.experimental.pallas import tpu as pltpu

RET_HEADS = 4
FOUR_GROUPS = 4
CA_HEADS = 4
ROPE_THETA = 10000.0
EPS = 1e-6
GN_EPS = 1e-5

V7X_VMEM_BYTES = 64 * 1024 * 1024
V7X_MXU_DIM = 256
V7X_LANES = 128
SUBLANES = 8

RET_CHUNK = V7X_MXU_DIM
FFT_N2 = 128
ROW_BLOCKS = 2

BF16 = jnp.bfloat16
F32 = jnp.float32


COMPILER_SCRATCH_BYTES = 8 * 1024 * 1024
VMEM_RESERVE_BYTES = 6 * 1024 * 1024


def _vmem_limit(nbytes):
    return int(min(nbytes + COMPILER_SCRATCH_BYTES, V7X_VMEM_BYTES - VMEM_RESERVE_BYTES))


def _rms(x, w):
    return x * lax.rsqrt(jnp.mean(x * x, axis=-1, keepdims=True) + EPS) * w


def _dot(a, b):
    return jnp.dot(a, b, preferred_element_type=F32)


def _words(x):
    return pltpu.bitcast(x.astype(BF16), jnp.int32)


def _rows(w):
    return pltpu.bitcast(w, BF16)


def _resident(shape):
    return pl.BlockSpec(shape, lambda *_: (0,) * len(shape), pipeline_mode=pl.Buffered(1))


@functools.lru_cache(maxsize=None)
def _rope_tables(s, dk):
    inv = ROPE_THETA ** (-np.arange(0, dk, 2, dtype=np.float64) / dk)
    ang = np.arange(s, dtype=np.float64)[:, None] * inv[None, :]
    return np.cos(ang).astype(np.float32), np.sin(ang).astype(np.float32)


@functools.lru_cache(maxsize=None)
def _chan_dft(n):
    ang = 2.0 * np.pi * np.outer(np.arange(n), np.arange(n)) / n
    return np.concatenate([np.cos(ang), -np.sin(ang)], axis=1).astype(np.float32)


@functools.lru_cache(maxsize=None)
def _fft_a_table(n1, n2):
    n = n1 * n2
    k1 = np.arange(n1, dtype=np.float64)[None, :, None]
    tok = (n2 * np.arange(n1, dtype=np.float64)[None, None, :]
           + np.arange(n2, dtype=np.float64)[:, None, None])
    ang = 2.0 * np.pi * np.mod(k1 * tok, n) / n
    c, s = np.cos(ang), np.sin(ang)
    re_rows = np.concatenate([c, s], axis=2)
    im_rows = np.concatenate([-s, c], axis=2)
    return np.stack([re_rows, im_rows], axis=2).reshape(n2, 2 * n1, 2 * n1).astype(np.float32)


@functools.lru_cache(maxsize=None)
def _fft_b_table(n2, scale):
    ang = 2.0 * np.pi * np.outer(np.arange(n2), np.arange(n2)) / n2
    return (scale * np.stack([np.cos(ang), np.sin(ang)], axis=-1).reshape(n2, 2 * n2)).astype(np.float32)


def _mem_kv_kernel(mem_ref, nw_ref, wk_ref, wv_ref, wq_ref, wo_ref, wqk_ref, wvo_ref):
    m, d = mem_ref.shape[1], mem_ref.shape[2]
    dh = d // CA_HEADS
    mn = _rms(mem_ref[0], nw_ref[...]).astype(BF16)
    ck = _dot(mn, wk_ref[...].astype(BF16)).astype(BF16)
    cv = _dot(mn, wv_ref[...].astype(BF16)).astype(BF16)
    for h in range(CA_HEADS):
        cols = slice(h * dh, (h + 1) * dh)
        qk = lax.dot_general(wq_ref[:, cols].astype(BF16), ck[:, cols], (((1,), (1,)), ((), ())),
                             preferred_element_type=F32) * (dh ** -0.5)
        wqk_ref[0, :, h * m:(h + 1) * m] = qk.astype(BF16)
        wvo_ref[0, h * m:(h + 1) * m, :] = _dot(cv[:, cols],
                                                wo_ref[cols, :].astype(BF16)).astype(BF16)


def _mem_kv(mem, norm_w, w_ck, w_cv, w_cq, w_co):
    b, m, d = mem.shape
    hm = CA_HEADS * m
    sq = _resident((d, d))
    return pl.pallas_call(
        _mem_kv_kernel,
        out_shape=(jax.ShapeDtypeStruct((b, d, hm), BF16), jax.ShapeDtypeStruct((b, hm, d), BF16)),
        grid=(b,),
        in_specs=[pl.BlockSpec((1, m, d), lambda i: (i, 0, 0)), _resident((1, d)), sq, sq, sq, sq],
        out_specs=(pl.BlockSpec((1, d, hm), lambda i: (i, 0, 0)),
                   pl.BlockSpec((1, hm, d), lambda i: (i, 0, 0))),
        compiler_params=pltpu.CompilerParams(
            dimension_semantics=("arbitrary",),
            vmem_limit_bytes=_vmem_limit(4 * d * d * 4 + 4 * d * d * 2 + 4 * m * d * 4
                                         + 4 * d * hm * 2 + 2 * d * hm * 4)),
        name="mem_kv",
    )(mem, norm_w, w_ck, w_cv, w_cq, w_co)


def _log_sigmoid(x):
    return jnp.minimum(x, 0.0) - jnp.log1p(jnp.exp(-jnp.abs(x)))


def _tn_dot(a, b):
    return lax.dot_general(a, b, (((0,), (0,)), ((), ())), preferred_element_type=F32)


def _decay_rows(dec, c, dk, offset, sign):
    lg = _log_sigmoid(jnp.full((c, dk), dec, F32))
    row = lax.broadcasted_iota(jnp.int32, (c, dk), 0).astype(F32)
    return jnp.exp(lg * (offset + sign * row))


def _chunk_decay(dec, c, dk):
    return jnp.exp(_log_sigmoid(jnp.full((1, dk), dec, F32)) * c)


def _ret_bwd_init(dec_ref, state_ref, zeta_ref, gc_ref):
    c, dk = zeta_ref.shape[1], zeta_ref.shape[2]

    @pl.when(pl.program_id(1) == 0)
    def _init():
        state_ref[...] = jnp.zeros_like(state_ref)
        for h in range(RET_HEADS):
            zeta_ref[h] = _decay_rows(dec_ref[h], c, dk, 0.0, 1.0).astype(BF16)
            gc_ref[h] = _chunk_decay(dec_ref[h], c, dk)


def _ret_bwd_states(k_ref, v_ref, sb_ref, state_ref, zeta_ref, gc_ref):
    c = RET_CHUNK
    t_len, d = k_ref.shape[1], k_ref.shape[2]
    dk = d // RET_HEADS
    for h in range(RET_HEADS):
        cols = slice(h * dk, (h + 1) * dk)
        state = state_ref[h]
        for ci in reversed(range(t_len // c)):
            rows = slice(ci * c, (ci + 1) * c)
            sb_ref[0, ci, h] = state.astype(BF16)
            state = state * gc_ref[h] + _tn_dot(k_ref[0, rows, cols],
                                                v_ref[0, rows, cols] * zeta_ref[h])
        state_ref[h] = state


def _in_proj_kernel(dec_ref, x_ref, nw_ref, w_ref, cos_ref, sin_ref,
                    q_ref, k_ref, v_ref, g_ref, z_ref, gr_ref, gf_ref, sb_ref,
                    perm_ref, state_ref, zeta_ref, gc_ref):
    _ret_bwd_init(dec_ref, state_ref, zeta_ref, gc_ref)
    d = x_ref.shape[-1]
    dk = d // RET_HEADS
    half = dk // 2
    tm = x_ref.shape[1]
    hb = tm // ROW_BLOCKS
    blocks = [slice(i * hb, (i + 1) * hb) for i in range(ROW_BLOCKS)]
    xb = [_rms(x_ref[0, rows, :], nw_ref[...]).astype(BF16) for rows in blocks]

    def proj(j):
        return [_dot(b, w_ref[:, j * d:(j + 1) * d]) for b in xb]

    def rope(ps, out_ref, scale):
        for p, rows in zip(ps, blocks):
            cos = cos_ref[rows, :]
            sin = sin_ref[rows, :]
            for h in range(RET_HEADS):
                x1 = p[:, h * dk:h * dk + half]
                x2 = p[:, h * dk + half:(h + 1) * dk]
                o1 = x1 * cos - x2 * sin
                o2 = x2 * cos + x1 * sin
                if scale is not None:
                    o1 = o1 * scale
                    o2 = o2 * scale
                out_ref[0, rows, h * dk:h * dk + half] = o1.astype(BF16)
                out_ref[0, rows, h * dk + half:(h + 1) * dk] = o2.astype(BF16)

    def store(ps, out_ref):
        for p, rows in zip(ps, blocks):
            out_ref[0, rows, :] = p.astype(BF16)

    rope(proj(0), q_ref, dk ** -0.5)
    rope(proj(1), k_ref, None)
    store(proj(2), v_ref)
    _ret_bwd_states(k_ref, v_ref, sb_ref, state_ref, zeta_ref, gc_ref)
    store(proj(3), g_ref)
    us = proj(4)
    for lc in range(d // V7X_LANES):
        for blk in range(tm // FFT_N2):
            a, par = divmod(blk, 2)
            u = us[blk * FFT_N2 // hb]
            r0 = blk * FFT_N2 % hb
            perm_ref[lc, pl.ds(2 * FFT_N2 * a + par, FFT_N2, stride=2), :] = (
                u[r0:r0 + FFT_N2, lc * V7X_LANES:(lc + 1) * V7X_LANES])
    for lc in range(d // V7X_LANES):
        z_ref[0, :, lc * V7X_LANES:(lc + 1) * V7X_LANES] = _words(perm_ref[lc])
    store(proj(5), gr_ref)
    store(proj(6), gf_ref)


def _in_proj(x, norm_w, w_in, cos, sin, decay_bwd, tm):
    b, s, d = x.shape
    dk = d // RET_HEADS
    half = dk // 2
    c = RET_CHUNK
    nt = s // tm
    act = jax.ShapeDtypeStruct((b, s, d), BF16)
    assert tm % (2 * FFT_N2) == 0 and tm % c == 0 and (tm // ROW_BLOCKS) % FFT_N2 == 0
    zsh = jax.ShapeDtypeStruct((b, s // 2, d), jnp.int32)
    sbsh = jax.ShapeDtypeStruct((b, s // c, RET_HEADS, dk, dk), BF16)
    row = pl.BlockSpec((1, tm, d), lambda i, j: (i, nt - 1 - j, 0))
    zrow = pl.BlockSpec((1, tm // 2, d), lambda i, j: (i, nt - 1 - j, 0))
    sbrow = pl.BlockSpec((1, tm // c, RET_HEADS, dk, dk), lambda i, j: (i, nt - 1 - j, 0, 0, 0))
    tab = pl.BlockSpec((tm, half), lambda i, j: (nt - 1 - j, 0))
    vmem = (w_in.size * 2 + 2 * tm * d * 4 + 2 * 8 * tm * d * 2 + 4 * tm * half * 4
            + 5 * tm * d * 4 + 2 * tm * d * 2 + RET_HEADS * (dk * dk * 4 + c * dk * 2))
    return pl.pallas_call(
        _in_proj_kernel,
        out_shape=(act, act, act, act, zsh, act, act, sbsh),
        grid=(b, nt),
        in_specs=[pl.BlockSpec(memory_space=pltpu.SMEM), row, _resident((1, d)),
                  _resident(w_in.shape), tab, tab],
        out_specs=(row, row, row, row, zrow, row, row, sbrow),
        scratch_shapes=[pltpu.VMEM((d // V7X_LANES, tm, V7X_LANES), F32),
                        pltpu.VMEM((RET_HEADS, dk, dk), F32),
                        pltpu.VMEM((RET_HEADS, c, dk), BF16),
                        pltpu.VMEM((RET_HEADS, 1, dk), F32)],
        compiler_params=pltpu.CompilerParams(
            dimension_semantics=("arbitrary", "arbitrary"),
            vmem_limit_bytes=_vmem_limit(vmem)),
        name="in_proj",
    )(decay_bwd, x, norm_w, w_in, cos, sin)


def _ret_fwd_kernel(decf_ref, decb_ref, q_ref, k_ref, v_ref, sb_ref, r_ref,
                    state_ref, dmat_ref, xif_ref, xib_ref, zeta_ref, gc_ref):
    c = RET_CHUNK
    t_len, d = q_ref.shape[1], q_ref.shape[2]
    dk = d // RET_HEADS

    @pl.when(pl.program_id(1) == 0)
    def _init():
        state_ref[...] = jnp.zeros_like(state_ref)
        for h in range(RET_HEADS):
            xif_ref[h] = _decay_rows(decf_ref[h], c, dk, 1.0, 1.0).astype(BF16)
            xib_ref[h] = _decay_rows(decb_ref[h], c, dk, float(c), -1.0).astype(BF16)
            zeta_ref[h] = _decay_rows(decf_ref[h], c, dk, c - 1.0, -1.0).astype(BF16)
            gc_ref[h] = _chunk_decay(decf_ref[h], c, dk)
            i = lax.broadcasted_iota(jnp.int32, (c, c), 0)
            j = lax.broadcasted_iota(jnp.int32, (c, c), 1)
            diff = (i - j).astype(F32)
            lgf = _log_sigmoid(jnp.full((c, c), decf_ref[h], F32))
            lgb = _log_sigmoid(jnp.full((c, c), decb_ref[h], F32))
            dmat_ref[h] = jnp.where(i >= j, jnp.exp(lgf * jnp.maximum(diff, 0.0)),
                                    jnp.exp(lgb * jnp.maximum(-diff, 0.0))).astype(BF16)

    states = [state_ref[h] for h in range(RET_HEADS)]
    for ci in range(t_len // c):
        rows = slice(ci * c, (ci + 1) * c)
        scores = []
        for h in range(RET_HEADS):
            cols = slice(h * dk, (h + 1) * dk)
            scores.append(lax.dot_general(q_ref[0, rows, cols], k_ref[0, rows, cols],
                                          (((1,), (1,)), ((), ())), preferred_element_type=F32))
        for h in range(RET_HEADS):
            cols = slice(h * dk, (h + 1) * dk)
            q = q_ref[0, rows, cols]
            v = v_ref[0, rows, cols]
            lhs = jnp.concatenate([scores[h].astype(BF16) * dmat_ref[h],
                                   q * xif_ref[h], q * xib_ref[h]], axis=1)
            rhs = jnp.concatenate([v, states[h].astype(BF16), sb_ref[0, ci, h]], axis=0)
            y = _dot(lhs, rhs)
            states[h] = states[h] * gc_ref[h] + _tn_dot(k_ref[0, rows, cols], v * zeta_ref[h])
            yc = y - jnp.mean(y, axis=-1, keepdims=True)
            var = jnp.mean(yc * yc, axis=-1, keepdims=True)
            r_ref[0, rows, cols] = (yc * lax.rsqrt(var + GN_EPS)).astype(BF16)
    for h in range(RET_HEADS):
        state_ref[h] = states[h]


def _fft_a_kernel(z_ref, dft_ref, tab_ref, o_ref):
    rn, d = z_ref.shape[2], z_ref.shape[3]
    gd = dft_ref.shape[0]
    u = jnp.concatenate([_rows(z_ref[0, :, r, :]) for r in range(rn)], axis=0)
    n1 = u.shape[0] // rn
    parts = [_dot(u[:, gi * gd:(gi + 1) * gd], dft_ref[...]) for gi in range(d // gd)]
    re = jnp.concatenate([p[:, :gd] for p in parts], axis=1).astype(BF16)
    im = jnp.concatenate([p[:, gd:] for p in parts], axis=1).astype(BF16)
    for r in range(rn):
        rows = slice(r * n1, (r + 1) * n1)
        zz = jnp.concatenate([re[rows], im[rows]], axis=0)
        o_ref[0, :, r, :] = _words(_dot(tab_ref[r], zz))


def _fft_b_kernel(zt_ref, tab_ref, o_ref):
    for i in range(zt_ref.shape[1]):
        o_ref[0, :, i, :] = _words(_dot(tab_ref[...], _rows(zt_ref[0, i])))


def _ret_fwd_fft_b_kernel(decf_ref, decb_ref, q_ref, k_ref, v_ref, sb_ref,
                          zt_ref, tab_ref, r_ref, f_ref,
                          state_ref, dmat_ref, xif_ref, xib_ref, zeta_ref, gc_ref):
    _ret_fwd_kernel(decf_ref, decb_ref, q_ref, k_ref, v_ref, sb_ref, r_ref,
                    state_ref, dmat_ref, xif_ref, xib_ref, zeta_ref, gc_ref)
    _fft_b_kernel(zt_ref, tab_ref, f_ref)


def _mixers(q, k, v, sb, z, decay_fwd, decay_bwd, dft, tab_a, tab_b):
    b, s, d = q.shape
    dk = d // RET_HEADS
    c = RET_CHUNK
    n2 = tab_b.shape[0]
    n1 = s // n2
    steps_b = n1 // SUBLANES
    t_b = s // steps_b
    assert t_b % c == 0
    smem = pl.BlockSpec(memory_space=pltpu.SMEM)
    fwd = pl.BlockSpec((1, t_b, d), lambda i, t: (i, t, 0))
    xi = pltpu.VMEM((RET_HEADS, c, dk), BF16)
    table_bytes = RET_HEADS * (3 * c * dk + c * c + dk * dk + dk) * 4
    stage_b_bytes = 2 * (SUBLANES * n2 * d * 4 + (n2 // 2) * SUBLANES * d * 4) + 4 * n2 * d * 4
    zt = pl.pallas_call(
        _fft_a_kernel,
        out_shape=jax.ShapeDtypeStruct((b, n1, n2, d), jnp.int32),
        grid=(b, n2 // SUBLANES),
        in_specs=[pl.BlockSpec((1, n1 // 2, SUBLANES, d), lambda i, t: (i, 0, t, 0)),
                  _resident(dft.shape),
                  pl.BlockSpec((SUBLANES, 2 * n1, 2 * n1), lambda i, t: (t, 0, 0))],
        out_specs=pl.BlockSpec((1, n1, SUBLANES, d), lambda i, t: (i, 0, t, 0)),
        compiler_params=pltpu.CompilerParams(
            dimension_semantics=("parallel", "parallel"),
            vmem_limit_bytes=_vmem_limit(2 * (3 * n1 * SUBLANES * d * 2
                                              + SUBLANES * 4 * n1 * n1 * 2)
                                         + 5 * n1 * SUBLANES * d * 4)),
        name="fft_a",
    )(z.reshape(b, n1 // 2, n2, d), dft, tab_a)
    return pl.pallas_call(
        _ret_fwd_fft_b_kernel,
        out_shape=(jax.ShapeDtypeStruct((b, s, d), BF16),
                   jax.ShapeDtypeStruct((b, n2 // 2, n1, d), jnp.int32)),
        grid=(b, steps_b),
        in_specs=[smem, smem, fwd, fwd, fwd,
                  pl.BlockSpec((1, t_b // c, RET_HEADS, dk, dk), lambda i, t: (i, t, 0, 0, 0)),
                  pl.BlockSpec((1, SUBLANES, n2, d), lambda i, t: (i, t, 0, 0)),
                  _resident((n2, 2 * n2))],
        out_specs=(fwd, pl.BlockSpec((1, n2 // 2, SUBLANES, d), lambda i, t: (i, 0, t, 0))),
        scratch_shapes=[pltpu.VMEM((RET_HEADS, dk, dk), F32), pltpu.VMEM((RET_HEADS, c, c), BF16),
                        xi, xi, pltpu.VMEM((RET_HEADS, c, dk), BF16),
                        pltpu.VMEM((RET_HEADS, 1, dk), F32)],
        compiler_params=pltpu.CompilerParams(
            dimension_semantics=("arbitrary", "arbitrary"),
            vmem_limit_bytes=_vmem_limit(2 * 5 * t_b * d * 2 + table_bytes + stage_b_bytes)),
        name="ret_fwd_fft_b",
    )(decay_fwd, decay_bwd, q, k, v, sb, zt, tab_b)


def _gain_rows_kernel(w_ref, g_ref, o_ref):
    o_ref[...] = (w_ref[...] * g_ref[...]).astype(BF16)


def _gain_rows(w, gain):
    d_in, d_out = w.shape
    rows = min(d_in, V7X_MXU_DIM)
    return pl.pallas_call(
        _gain_rows_kernel,
        out_shape=jax.ShapeDtypeStruct((d_in, d_out), BF16),
        grid=(d_in // rows,),
        in_specs=[pl.BlockSpec((rows, d_out), lambda i: (i, 0)),
                  pl.BlockSpec((rows, 1), lambda i: (i, 0))],
        out_specs=pl.BlockSpec((rows, d_out), lambda i: (i, 0)),
        compiler_params=pltpu.CompilerParams(dimension_semantics=("parallel",)),
        name="gain_rows",
    )(w, gain.reshape(d_in, 1))


def _tail_kernel(x_ref, r_ref, g_ref, f_ref, gr_ref, gf_ref, wqk_ref, wvo_ref,
                 w_ro_ref, w_fo_ref, w_mo_ref, n_ca_ref,
                 n_mlp_ref, w_up_ref, w_dn_ref, n_fin_ref, o_ref, att_ref, perm_ref):
    d = x_ref.shape[-1]
    m = wqk_ref.shape[2] // CA_HEADS
    ff = w_up_ref.shape[1]
    gq, n1 = f_ref.shape[1], f_ref.shape[2]
    hb = x_ref.shape[1] // ROW_BLOCKS
    lanes = range(d // V7X_LANES)
    blocks = [slice(i * hb, (i + 1) * hb) for i in range(ROW_BLOCKS)]
    def gated(rows):
        g = g_ref[0, rows, :]
        return g * jax.nn.sigmoid(g) * r_ref[0, rows, :]

    ret = [_dot(gated(rows), w_ro_ref[...]) for rows in blocks]
    for i, rows in enumerate(blocks):
        g0, g1 = i * gq // ROW_BLOCKS, (i + 1) * gq // ROW_BLOCKS
        fperm = _dot(_rows(f_ref[0, g0:g1].reshape(hb // 2, d)), w_fo_ref[...])
        for lc in lanes:
            perm_ref[lc, rows, :] = fperm[:, lc * V7X_LANES:(lc + 1) * V7X_LANES]
    xs = []
    for i, rows in enumerate(blocks):
        fou = jnp.concatenate(
            [jnp.concatenate([perm_ref[lc, pl.ds(i * hb + blk // 2 * 2 * n1 + blk % 2, n1,
                                                 stride=2), :] for lc in lanes], axis=1)
             for blk in range(2 * gq // ROW_BLOCKS)], axis=0)
        merged = (jax.nn.sigmoid(gr_ref[0, rows, :].astype(F32)) * ret[i]
                  + jax.nn.sigmoid(gf_ref[0, rows, :].astype(F32)) * fou)
        xs.append(x_ref[0, rows, :] + _dot(merged.astype(BF16), w_mo_ref[...]))

    logits = [_dot(_rms(x, n_ca_ref[...]).astype(BF16), wqk_ref[0]) for x in xs]
    for lg, rows in zip(logits, blocks):
        for h in range(CA_HEADS):
            cols = slice(h * m, (h + 1) * m)
            p = jnp.exp(lg[:, cols] - jnp.max(lg[:, cols], axis=-1, keepdims=True))
            att_ref[rows, cols] = (p / jnp.sum(p, axis=-1, keepdims=True)).astype(BF16)
    xs = [x + _dot(att_ref[rows, :], wvo_ref[0]) for x, rows in zip(xs, blocks)]

    xb = [_rms(x, n_mlp_ref[...]).astype(BF16) for x in xs]
    for j in range(ff // d):
        cols = slice(j * d, (j + 1) * d)
        hid = [jnp.square(jnp.maximum(_dot(b, w_up_ref[:, cols]), 0.0)).astype(BF16) for b in xb]
        xs = [x + _dot(hd, w_dn_ref[cols, :]) for x, hd in zip(xs, hid)]
    for x, rows in zip(xs, blocks):
        o_ref[0, rows, :] = _rms(x, n_fin_ref[...])


def _tail(x, r, g, four, gate_r, gate_f, wqk, wvo, w_ro, w_fo, w_mo, n_ca,
          n_mlp, w_up, w_dn, n_fin, tm):
    b, s, d = x.shape
    hm = wqk.shape[2]
    ff = w_up.shape[1]
    row = pl.BlockSpec((1, tm, d), lambda i, j: (i, j, 0))
    n1 = four.shape[2]
    assert tm % (2 * n1 * ROW_BLOCKS) == 0
    half = pl.BlockSpec((1, tm // (2 * n1), n1, d), lambda i, j: (i, j, 0, 0))
    vec = _resident((1, d))
    sq = _resident((d, d))
    vmem = ((3 * d * d + 2 * d * ff) * 2 + 4 * d * hm * 2 + 2 * tm * d * (4 + 4 + 5 * 2)
            + tm * d * 4 * 8)
    return pl.pallas_call(
        _tail_kernel,
        out_shape=jax.ShapeDtypeStruct((b, s, d), F32),
        grid=(b, s // tm),
        in_specs=[row, row, row, half, row, row,
                  pl.BlockSpec((1, d, hm), lambda i, j: (i, 0, 0)),
                  pl.BlockSpec((1, hm, d), lambda i, j: (i, 0, 0)),
                  sq, sq, sq, vec, vec, _resident((d, ff)), _resident((ff, d)), vec],
        out_specs=row,
        scratch_shapes=[pltpu.VMEM((tm, hm), BF16),
                        pltpu.VMEM((d // V7X_LANES, tm, V7X_LANES), F32)],
        compiler_params=pltpu.CompilerParams(
            dimension_semantics=("parallel", "parallel"),
            vmem_limit_bytes=_vmem_limit(vmem)),
        name="tail",
    )(x, r, g, four, gate_r, gate_f, wqk, wvo, w_ro, w_fo, w_mo, n_ca, n_mlp, w_up, w_dn, n_fin)


def _tiles(s):
    return min(512, s), min(512, s)


def _trunk(x, mem, p):
    b, s, d = x.shape
    dk = d // RET_HEADS
    gd = d // FOUR_GROUPS
    n1 = s // FFT_N2
    assert s % FFT_N2 == 0 and s % RET_CHUNK == 0
    tm_in, tm_tail = _tiles(s)
    cos, sin = (jnp.asarray(t) for t in _rope_tables(s, dk))
    dft = jnp.asarray(_chan_dft(gd)).astype(BF16)
    tab_a = jnp.asarray(_fft_a_table(n1, FFT_N2)).astype(BF16)
    tab_b = jnp.asarray(_fft_b_table(FFT_N2, float(s * gd) ** -0.5)).astype(BF16)

    for l in range(p["w_in"].shape[0]):
        wqk, wvo = _mem_kv(mem, p["norm_mem_w"][l], p["w_ck"][l], p["w_cv"][l],
                           p["w_cq"][l], p["w_co"][l])
        q, k, v, g, z, gate_r, gate_f, sb = _in_proj(x, p["norm_mix_w"][l], p["w_in"][l], cos, sin,
                                                     p["ret_decay_bwd"][l], tm_in)
        r, four = _mixers(q, k, v, sb, z, p["ret_decay_fwd"][l], p["ret_decay_bwd"][l],
                          dft, tab_a, tab_b)
        w_ro = _gain_rows(p["w_ret_out"][l], p["ret_gn_w"][l])
        x = _tail(x, r, g, four, gate_r, gate_f, wqk, wvo, w_ro, p["w_four_out"][l],
                  p["w_mix_out"][l], p["norm_ca_w"][l], p["norm_mlp_w"][l], p["w_up"][l],
                  p["w_down"][l], p["norm_final_w"], tm_tail)
    return x


def kernel(x_prompt, x_sample, mem_prompt, mem_sample, norm_mix_w, w_in, ret_decay_fwd,
           ret_decay_bwd, ret_gn_w, w_ret_out, w_four_out, w_mix_out, norm_ca_w, norm_mem_w,
           w_cq, w_ck, w_cv, w_co, norm_mlp_w, w_up, w_down, norm_final_w):
    depth, d = norm_mix_w.shape
    assert depth == 1, "the tail fuses the final norm into the (single) layer"
    vec = lambda w: w.reshape(depth, 1, d)
    mat = lambda w: w.astype(BF16)
    p = dict(
        norm_mix_w=vec(norm_mix_w), w_in=mat(w_in), ret_decay_fwd=ret_decay_fwd,
        ret_decay_bwd=ret_decay_bwd, ret_gn_w=ret_gn_w, w_ret_out=w_ret_out,
        w_four_out=mat(w_four_out), w_mix_out=mat(w_mix_out), norm_ca_w=vec(norm_ca_w),
        norm_mem_w=vec(norm_mem_w), w_cq=w_cq, w_ck=w_ck, w_cv=w_cv,
        w_co=w_co, norm_mlp_w=vec(norm_mlp_w), w_up=mat(w_up), w_down=mat(w_down),
        norm_final_w=norm_final_w.reshape(1, d))
    return _trunk(x_prompt, mem_prompt, p), _trunk(x_sample, mem_sample, p)
```

```python
import functools

import numpy as np
import jax
import jax.numpy as jnp
from jax import lax
from jax.experimental import pallas as pl
from jax.experimental.pallas import tpu as pltpu

RET_HEADS = 4
FOUR_GROUPS = 4
CA_HEADS = 4
ROPE_THETA = 10000.0
EPS = 1e-6
GN_EPS = 1e-5

V7X_VMEM_BYTES = 64 * 1024 * 1024
V7X_MXU_DIM = 256
V7X_LANES = 128
SUBLANES = 8

RET_CHUNK = V7X_MXU_DIM
FFT_N2 = 128
ROW_BLOCKS = 2

BF16 = jnp.bfloat16
F32 = jnp.float32


COMPILER_SCRATCH_BYTES = 8 * 1024 * 1024
VMEM_RESERVE_BYTES = 6 * 1024 * 1024


def _vmem_limit(nbytes):
    return int(min(nbytes + COMPILER_SCRATCH_BYTES, V7X_VMEM_BYTES - VMEM_RESERVE_BYTES))


def _rms(x, w):
    return x * lax.rsqrt(jnp.mean(x * x, axis=-1, keepdims=True) + EPS) * w


def _dot(a, b):
    return jnp.dot(a, b, preferred_element_type=F32)


def _words(x):
    return pltpu.bitcast(x.astype(BF16), jnp.int32)


def _rows(w):
    return pltpu.bitcast(w, BF16)


def _resident(shape):
    return pl.BlockSpec(shape, lambda *_: (0,) * len(shape), pipeline_mode=pl.Buffered(1))


@functools.lru_cache(maxsize=None)
def _rope_tables(s, dk):
    inv = ROPE_THETA ** (-np.arange(0, dk, 2, dtype=np.float64) / dk)
    ang = np.arange(s, dtype=np.float64)[:, None] * inv[None, :]
    return np.cos(ang).astype(np.float32), np.sin(ang).astype(np.float32)


@functools.lru_cache(maxsize=None)
def _chan_dft(n):
    ang = 2.0 * np.pi * np.outer(np.arange(n), np.arange(n)) / n
    return np.concatenate([np.cos(ang), -np.sin(ang)], axis=1).astype(np.float32)


@functools.lru_cache(maxsize=None)
def _fft_a_table(n1, n2):
    n = n1 * n2
    k1 = np.arange(n1, dtype=np.float64)[None, :, None]
    tok = (n2 * np.arange(n1, dtype=np.float64)[None, None, :]
           + np.arange(n2, dtype=np.float64)[:, None, None])
    ang = 2.0 * np.pi * np.mod(k1 * tok, n) / n
    c, s = np.cos(ang), np.sin(ang)
    re_rows = np.concatenate([c, s], axis=2)
    im_rows = np.concatenate([-s, c], axis=2)
    return np.stack([re_rows, im_rows], axis=2).reshape(n2, 2 * n1, 2 * n1).astype(np.float32)


@functools.lru_cache(maxsize=None)
def _fft_b_table(n2, scale):
    ang = 2.0 * np.pi * np.outer(np.arange(n2), np.arange(n2)) / n2
    return (scale * np.stack([np.cos(ang), np.sin(ang)], axis=-1).reshape(n2, 2 * n2)).astype(np.float32)


def _mem_kv_kernel(mem_ref, nw_ref, wk_ref, wv_ref, wq_ref, wo_ref, wqk_ref, wvo_ref):
    m, d = mem_ref.shape[1], mem_ref.shape[2]
    dh = d // CA_HEADS
    mn = _rms(mem_ref[0], nw_ref[...]).astype(BF16)
    ck = _dot(mn, wk_ref[...]).astype(BF16)
    cv = _dot(mn, wv_ref[...]).astype(BF16)
    for h in range(CA_HEADS):
        cols = slice(h * dh, (h + 1) * dh)
        qk = lax.dot_general(wq_ref[:, cols], ck[:, cols], (((1,), (1,)), ((), ())),
                             preferred_element_type=F32) * (dh ** -0.5)
        wqk_ref[0, :, h * m:(h + 1) * m] = qk.astype(BF16)
        wvo_ref[0, h * m:(h + 1) * m, :] = _dot(cv[:, cols], wo_ref[cols, :]).astype(BF16)


def _mem_kv(mem, norm_w, w_ck, w_cv, w_cq, w_co):
    b, m, d = mem.shape
    hm = CA_HEADS * m
    sq = _resident((d, d))
    return pl.pallas_call(
        _mem_kv_kernel,
        out_shape=(jax.ShapeDtypeStruct((b, d, hm), BF16), jax.ShapeDtypeStruct((b, hm, d), BF16)),
        grid=(b,),
        in_specs=[pl.BlockSpec((1, m, d), lambda i: (i, 0, 0)), _resident((1, d)), sq, sq, sq, sq],
        out_specs=(pl.BlockSpec((1, d, hm), lambda i: (i, 0, 0)),
                   pl.BlockSpec((1, hm, d), lambda i: (i, 0, 0))),
        compiler_params=pltpu.CompilerParams(
            dimension_semantics=("arbitrary",),
            vmem_limit_bytes=_vmem_limit(4 * d * d * 2 + 4 * m * d * 4 + 4 * d * hm * 2
                                         + 2 * d * hm * 4)),
        name="mem_kv",
    )(mem, norm_w, w_ck, w_cv, w_cq, w_co)


def _log_sigmoid(x):
    return jnp.minimum(x, 0.0) - jnp.log1p(jnp.exp(-jnp.abs(x)))


def _tn_dot(a, b):
    return lax.dot_general(a, b, (((0,), (0,)), ((), ())), preferred_element_type=F32)


def _decay_rows(dec, c, dk, offset, sign):
    lg = _log_sigmoid(jnp.full((c, dk), dec, F32))
    row = lax.broadcasted_iota(jnp.int32, (c, dk), 0).astype(F32)
    return jnp.exp(lg * (offset + sign * row))


def _chunk_decay(dec, c, dk):
    return jnp.exp(_log_sigmoid(jnp.full((1, dk), dec, F32)) * c)


def _ret_bwd_init(dec_ref, state_ref, zeta_ref, gc_ref):
    c, dk = zeta_ref.shape[1], zeta_ref.shape[2]

    @pl.when(pl.program_id(1) == 0)
    def _init():
        state_ref[...] = jnp.zeros_like(state_ref)
        for h in range(RET_HEADS):
            zeta_ref[h] = _decay_rows(dec_ref[h], c, dk, 0.0, 1.0).astype(BF16)
            gc_ref[h] = _chunk_decay(dec_ref[h], c, dk)


def _ret_bwd_states(k_ref, v_ref, sb_ref, state_ref, zeta_ref, gc_ref):
    c = RET_CHUNK
    t_len, d = k_ref.shape[1], k_ref.shape[2]
    dk = d // RET_HEADS
    for h in range(RET_HEADS):
        cols = slice(h * dk, (h + 1) * dk)
        state = state_ref[h]
        for ci in reversed(range(t_len // c)):
            rows = slice(ci * c, (ci + 1) * c)
            sb_ref[0, ci, h] = state.astype(BF16)
            state = state * gc_ref[h] + _tn_dot(k_ref[0, rows, cols],
                                                v_ref[0, rows, cols] * zeta_ref[h])
        state_ref[h] = state


def _in_proj_kernel(dec_ref, x_ref, nw_ref, w_ref, cos_ref, sin_ref,
                    q_ref, k_ref, v_ref, g_ref, z_ref, gr_ref, gf_ref, sb_ref,
                    perm_ref, state_ref, zeta_ref, gc_ref):
    _ret_bwd_init(dec_ref, state_ref, zeta_ref, gc_ref)
    d = x_ref.shape[-1]
    dk = d // RET_HEADS
    half = dk // 2
    tm = x_ref.shape[1]
    hb = tm // ROW_BLOCKS
    blocks = [slice(i * hb, (i + 1) * hb) for i in range(ROW_BLOCKS)]
    xb = [_rms(x_ref[0, rows, :], nw_ref[...]).astype(BF16) for rows in blocks]

    def proj(j):
        return [_dot(b, w_ref[:, j * d:(j + 1) * d]) for b in xb]

    def rope(ps, out_ref, scale):
        for p, rows in zip(ps, blocks):
            cos = cos_ref[rows, :]
            sin = sin_ref[rows, :]
            for h in range(RET_HEADS):
                x1 = p[:, h * dk:h * dk + half]
                x2 = p[:, h * dk + half:(h + 1) * dk]
                o1 = x1 * cos - x2 * sin
                o2 = x2 * cos + x1 * sin
                if scale is not None:
                    o1 = o1 * scale
                    o2 = o2 * scale
                out_ref[0, rows, h * dk:h * dk + half] = o1.astype(BF16)
                out_ref[0, rows, h * dk + half:(h + 1) * dk] = o2.astype(BF16)

    def store(ps, out_ref):
        for p, rows in zip(ps, blocks):
            out_ref[0, rows, :] = p.astype(BF16)

    rope(proj(0), q_ref, dk ** -0.5)
    rope(proj(1), k_ref, None)
    store(proj(2), v_ref)
    _ret_bwd_states(k_ref, v_ref, sb_ref, state_ref, zeta_ref, gc_ref)
    store(proj(3), g_ref)
    us = proj(4)
    for lc in range(d // V7X_LANES):
        for blk in range(tm // FFT_N2):
            a, par = divmod(blk, 2)
            u = us[blk * FFT_N2 // hb]
            r0 = blk * FFT_N2 % hb
            perm_ref[lc, pl.ds(2 * FFT_N2 * a + par, FFT_N2, stride=2), :] = (
                u[r0:r0 + FFT_N2, lc * V7X_LANES:(lc + 1) * V7X_LANES])
    for lc in range(d // V7X_LANES):
        z_ref[0, :, lc * V7X_LANES:(lc + 1) * V7X_LANES] = _words(perm_ref[lc])
    store(proj(5), gr_ref)
    store(proj(6), gf_ref)


def _in_proj(x, norm_w, w_in, cos, sin, decay_bwd, tm):
    b, s, d = x.shape
    dk = d // RET_HEADS
    half = dk // 2
    c = RET_CHUNK
    nt = s // tm
    act = jax.ShapeDtypeStruct((b, s, d), BF16)
    assert tm % (2 * FFT_N2) == 0 and tm % c == 0 and (tm // ROW_BLOCKS) % FFT_N2 == 0
    zsh = jax.ShapeDtypeStruct((b, s // 2, d), jnp.int32)
    sbsh = jax.ShapeDtypeStruct((b, s // c, RET_HEADS, dk, dk), BF16)
    row = pl.BlockSpec((1, tm, d), lambda i, j: (i, nt - 1 - j, 0))
    zrow = pl.BlockSpec((1, tm // 2, d), lambda i, j: (i, nt - 1 - j, 0))
    sbrow = pl.BlockSpec((1, tm // c, RET_HEADS, dk, dk), lambda i, j: (i, nt - 1 - j, 0, 0, 0))
    tab = pl.BlockSpec((tm, half), lambda i, j: (nt - 1 - j, 0))
    vmem = (w_in.size * 2 + 2 * tm * d * 4 + 2 * 8 * tm * d * 2 + 4 * tm * half * 4
            + 5 * tm * d * 4 + 2 * tm * d * 2 + RET_HEADS * (dk * dk * 4 + c * dk * 2))
    return pl.pallas_call(
        _in_proj_kernel,
        out_shape=(act, act, act, act, zsh, act, act, sbsh),
        grid=(b, nt),
        in_specs=[pl.BlockSpec(memory_space=pltpu.SMEM), row, _resident((1, d)),
                  _resident(w_in.shape), tab, tab],
        out_specs=(row, row, row, row, zrow, row, row, sbrow),
        scratch_shapes=[pltpu.VMEM((d // V7X_LANES, tm, V7X_LANES), F32),
                        pltpu.VMEM((RET_HEADS, dk, dk), F32),
                        pltpu.VMEM((RET_HEADS, c, dk), BF16),
                        pltpu.VMEM((RET_HEADS, 1, dk), F32)],
        compiler_params=pltpu.CompilerParams(
            dimension_semantics=("arbitrary", "arbitrary"),
            vmem_limit_bytes=_vmem_limit(vmem)),
        name="in_proj",
    )(decay_bwd, x, norm_w, w_in, cos, sin)


def _ret_fwd_kernel(decf_ref, decb_ref, q_ref, k_ref, v_ref, sb_ref, r_ref,
                    state_ref, dmat_ref, xif_ref, xib_ref, zeta_ref, gc_ref):
    c = RET_CHUNK
    t_len, d = q_ref.shape[1], q_ref.shape[2]
    dk = d // RET_HEADS

    @pl.when(pl.program_id(1) == 0)
    def _init():
        state_ref[...] = jnp.zeros_like(state_ref)
        for h in range(RET_HEADS):
            xif_ref[h] = _decay_rows(decf_ref[h], c, dk, 1.0, 1.0).astype(BF16)
            xib_ref[h] = _decay_rows(decb_ref[h], c, dk, float(c), -1.0).astype(BF16)
            zeta_ref[h] = _decay_rows(decf_ref[h], c, dk, c - 1.0, -1.0).astype(BF16)
            gc_ref[h] = _chunk_decay(decf_ref[h], c, dk)
            i = lax.broadcasted_iota(jnp.int32, (c, c), 0)
            j = lax.broadcasted_iota(jnp.int32, (c, c), 1)
            diff = (i - j).astype(F32)
            lgf = _log_sigmoid(jnp.full((c, c), decf_ref[h], F32))
            lgb = _log_sigmoid(jnp.full((c, c), decb_ref[h], F32))
            dmat_ref[h] = jnp.where(i >= j, jnp.exp(lgf * jnp.maximum(diff, 0.0)),
                                    jnp.exp(lgb * jnp.maximum(-diff, 0.0))).astype(BF16)

    states = [state_ref[h] for h in range(RET_HEADS)]
    for ci in range(t_len // c):
        rows = slice(ci * c, (ci + 1) * c)
        scores = []
        for h in range(RET_HEADS):
            cols = slice(h * dk, (h + 1) * dk)
            scores.append(lax.dot_general(q_ref[0, rows, cols], k_ref[0, rows, cols],
                                          (((1,), (1,)), ((), ())), preferred_element_type=F32))
        for h in range(RET_HEADS):
            cols = slice(h * dk, (h + 1) * dk)
            q = q_ref[0, rows, cols]
            v = v_ref[0, rows, cols]
            lhs = jnp.concatenate([scores[h].astype(BF16) * dmat_ref[h],
                                   q * xif_ref[h], q * xib_ref[h]], axis=1)
            rhs = jnp.concatenate([v, states[h].astype(BF16), sb_ref[0, ci, h]], axis=0)
            y = _dot(lhs, rhs)
            states[h] = states[h] * gc_ref[h] + _tn_dot(k_ref[0, rows, cols], v * zeta_ref[h])
            yc = y - jnp.mean(y, axis=-1, keepdims=True)
            var = jnp.mean(yc * yc, axis=-1, keepdims=True)
            r_ref[0, rows, cols] = (yc * lax.rsqrt(var + GN_EPS)).astype(BF16)
    for h in range(RET_HEADS):
        state_ref[h] = states[h]


def _fft_a_kernel(z_ref, dft_ref, tab_ref, o_ref):
    rn, d = z_ref.shape[2], z_ref.shape[3]
    gd = dft_ref.shape[0]
    u = jnp.concatenate([_rows(z_ref[0, :, r, :]) for r in range(rn)], axis=0)
    n1 = u.shape[0] // rn
    parts = [_dot(u[:, gi * gd:(gi + 1) * gd], dft_ref[...]) for gi in range(d // gd)]
    re = jnp.concatenate([p[:, :gd] for p in parts], axis=1).astype(BF16)
    im = jnp.concatenate([p[:, gd:] for p in parts], axis=1).astype(BF16)
    for r in range(rn):
        rows = slice(r * n1, (r + 1) * n1)
        zz = jnp.concatenate([re[rows], im[rows]], axis=0)
        o_ref[0, :, r, :] = _words(_dot(tab_ref[r], zz))


def _fft_b_kernel(zt_ref, tab_ref, o_ref):
    for i in range(zt_ref.shape[1]):
        o_ref[0, :, i, :] = _words(_dot(tab_ref[...], _rows(zt_ref[0, i])))


def _ret_fwd_fft_b_kernel(decf_ref, decb_ref, q_ref, k_ref, v_ref, sb_ref,
                          zt_ref, tab_ref, r_ref, f_ref,
                          state_ref, dmat_ref, xif_ref, xib_ref, zeta_ref, gc_ref):
    _ret_fwd_kernel(decf_ref, decb_ref, q_ref, k_ref, v_ref, sb_ref, r_ref,
                    state_ref, dmat_ref, xif_ref, xib_ref, zeta_ref, gc_ref)
    _fft_b_kernel(zt_ref, tab_ref, f_ref)


def _mixers(q, k, v, sb, z, decay_fwd, decay_bwd, dft, tab_a, tab_b):
    b, s, d = q.shape
    dk = d // RET_HEADS
    c = RET_CHUNK
    n2 = tab_b.shape[0]
    n1 = s // n2
    steps_b = n1 // SUBLANES
    t_b = s // steps_b
    assert t_b % c == 0
    smem = pl.BlockSpec(memory_space=pltpu.SMEM)
    fwd = pl.BlockSpec((1, t_b, d), lambda i, t: (i, t, 0))
    xi = pltpu.VMEM((RET_HEADS, c, dk), BF16)
    table_bytes = RET_HEADS * (3 * c * dk + c * c + dk * dk + dk) * 4
    stage_b_bytes = 2 * (SUBLANES * n2 * d * 4 + (n2 // 2) * SUBLANES * d * 4) + 4 * n2 * d * 4
    zt = pl.pallas_call(
        _fft_a_kernel,
        out_shape=jax.ShapeDtypeStruct((b, n1, n2, d), jnp.int32),
        grid=(b, n2 // SUBLANES),
        in_specs=[pl.BlockSpec((1, n1 // 2, SUBLANES, d), lambda i, t: (i, 0, t, 0)),
                  _resident(dft.shape),
                  pl.BlockSpec((SUBLANES, 2 * n1, 2 * n1), lambda i, t: (t, 0, 0))],
        out_specs=pl.BlockSpec((1, n1, SUBLANES, d), lambda i, t: (i, 0, t, 0)),
        compiler_params=pltpu.CompilerParams(
            dimension_semantics=("parallel", "parallel"),
            vmem_limit_bytes=_vmem_limit(2 * (3 * n1 * SUBLANES * d * 2
                                              + SUBLANES * 4 * n1 * n1 * 2)
                                         + 5 * n1 * SUBLANES * d * 4)),
        name="fft_a",
    )(z.reshape(b, n1 // 2, n2, d), dft, tab_a)
    return pl.pallas_call(
        _ret_fwd_fft_b_kernel,
        out_shape=(jax.ShapeDtypeStruct((b, s, d), BF16),
                   jax.ShapeDtypeStruct((b, n2 // 2, n1, d), jnp.int32)),
        grid=(b, steps_b),
        in_specs=[smem, smem, fwd, fwd, fwd,
                  pl.BlockSpec((1, t_b // c, RET_HEADS, dk, dk), lambda i, t: (i, t, 0, 0, 0)),
                  pl.BlockSpec((1, SUBLANES, n2, d), lambda i, t: (i, t, 0, 0)),
                  _resident((n2, 2 * n2))],
        out_specs=(fwd, pl.BlockSpec((1, n2 // 2, SUBLANES, d), lambda i, t: (i, 0, t, 0))),
        scratch_shapes=[pltpu.VMEM((RET_HEADS, dk, dk), F32), pltpu.VMEM((RET_HEADS, c, c), BF16),
                        xi, xi, pltpu.VMEM((RET_HEADS, c, dk), BF16),
                        pltpu.VMEM((RET_HEADS, 1, dk), F32)],
        compiler_params=pltpu.CompilerParams(
            dimension_semantics=("arbitrary", "arbitrary"),
            vmem_limit_bytes=_vmem_limit(2 * 5 * t_b * d * 2 + table_bytes + stage_b_bytes)),
        name="ret_fwd_fft_b",
    )(decay_fwd, decay_bwd, q, k, v, sb, zt, tab_b)


def _gain_rows_kernel(w_ref, g_ref, o_ref):
    o_ref[...] = (w_ref[...] * g_ref[...]).astype(BF16)


def _gain_rows(w, gain):
    d_in, d_out = w.shape
    rows = min(d_in, V7X_MXU_DIM)
    return pl.pallas_call(
        _gain_rows_kernel,
        out_shape=jax.ShapeDtypeStruct((d_in, d_out), BF16),
        grid=(d_in // rows,),
        in_specs=[pl.BlockSpec((rows, d_out), lambda i: (i, 0)),
                  pl.BlockSpec((rows, 1), lambda i: (i, 0))],
        out_specs=pl.BlockSpec((rows, d_out), lambda i: (i, 0)),
        compiler_params=pltpu.CompilerParams(dimension_semantics=("parallel",)),
        name="gain_rows",
    )(w, gain.reshape(d_in, 1))


def _tail_kernel(x_ref, r_ref, g_ref, f_ref, gr_ref, gf_ref, wqk_ref, wvo_ref,
                 w_ro_ref, w_fo_ref, w_mo_ref, n_ca_ref,
                 n_mlp_ref, w_up_ref, w_dn_ref, n_fin_ref, o_ref, att_ref, perm_ref):
    d = x_ref.shape[-1]
    m = wqk_ref.shape[2] // CA_HEADS
    ff = w_up_ref.shape[1]
    gq, n1 = f_ref.shape[1], f_ref.shape[2]
    hb = x_ref.shape[1] // ROW_BLOCKS
    lanes = range(d // V7X_LANES)
    blocks = [slice(i * hb, (i + 1) * hb) for i in range(ROW_BLOCKS)]
    def gated(rows):
        g = g_ref[0, rows, :]
        return g * jax.nn.sigmoid(g) * r_ref[0, rows, :]

    ret = [_dot(gated(rows), w_ro_ref[...]) for rows in blocks]
    for i, rows in enumerate(blocks):
        g0, g1 = i * gq // ROW_BLOCKS, (i + 1) * gq // ROW_BLOCKS
        fperm = _dot(_rows(f_ref[0, g0:g1].reshape(hb // 2, d)), w_fo_ref[...])
        for lc in lanes:
            perm_ref[lc, rows, :] = fperm[:, lc * V7X_LANES:(lc + 1) * V7X_LANES]
    xs = []
    for i, rows in enumerate(blocks):
        fou = jnp.concatenate(
            [jnp.concatenate([perm_ref[lc, pl.ds(i * hb + blk // 2 * 2 * n1 + blk % 2, n1,
                                                 stride=2), :] for lc in lanes], axis=1)
             for blk in range(2 * gq // ROW_BLOCKS)], axis=0)
        merged = (jax.nn.sigmoid(gr_ref[0, rows, :].astype(F32)) * ret[i]
                  + jax.nn.sigmoid(gf_ref[0, rows, :].astype(F32)) * fou)
        xs.append(x_ref[0, rows, :] + _dot(merged.astype(BF16), w_mo_ref[...]))

    logits = [_dot(_rms(x, n_ca_ref[...]).astype(BF16), wqk_ref[0]) for x in xs]
    for lg, rows in zip(logits, blocks):
        for h in range(CA_HEADS):
            cols = slice(h * m, (h + 1) * m)
            p = jnp.exp(lg[:, cols] - jnp.max(lg[:, cols], axis=-1, keepdims=True))
            att_ref[rows, cols] = (p / jnp.sum(p, axis=-1, keepdims=True)).astype(BF16)
    xs = [x + _dot(att_ref[rows, :], wvo_ref[0]) for x, rows in zip(xs, blocks)]

    xb = [_rms(x, n_mlp_ref[...]).astype(BF16) for x in xs]
    for j in range(ff // d):
        cols = slice(j * d, (j + 1) * d)
        hid = [jnp.square(jnp.maximum(_dot(b, w_up_ref[:, cols]), 0.0)).astype(BF16) for b in xb]
        xs = [x + _dot(hd, w_dn_ref[cols, :]) for x, hd in zip(xs, hid)]
    for x, rows in zip(xs, blocks):
        o_ref[0, rows, :] = _rms(x, n_fin_ref[...])


def _tail(x, r, g, four, gate_r, gate_f, wqk, wvo, w_ro, w_fo, w_mo, n_ca,
          n_mlp, w_up, w_dn, n_fin, tm):
    b, s, d = x.shape
    hm = wqk.shape[2]
    ff = w_up.shape[1]
    row = pl.BlockSpec((1, tm, d), lambda i, j: (i, j, 0))
    n1 = four.shape[2]
    assert tm % (2 * n1 * ROW_BLOCKS) == 0
    half = pl.BlockSpec((1, tm // (2 * n1), n1, d), lambda i, j: (i, j, 0, 0))
    vec = _resident((1, d))
    sq = _resident((d, d))
    vmem = ((3 * d * d + 2 * d * ff) * 2 + 4 * d * hm * 2 + 2 * tm * d * (4 + 4 + 5 * 2)
            + tm * d * 4 * 8)
    return pl.pallas_call(
        _tail_kernel,
        out_shape=jax.ShapeDtypeStruct((b, s, d), F32),
        grid=(b, s // tm),
        in_specs=[row, row, row, half, row, row,
                  pl.BlockSpec((1, d, hm), lambda i, j: (i, 0, 0)),
                  pl.BlockSpec((1, hm, d), lambda i, j: (i, 0, 0)),
                  sq, sq, sq, vec, vec, _resident((d, ff)), _resident((ff, d)), vec],
        out_specs=row,
        scratch_shapes=[pltpu.VMEM((tm, hm), BF16),
                        pltpu.VMEM((d // V7X_LANES, tm, V7X_LANES), F32)],
        compiler_params=pltpu.CompilerParams(
            dimension_semantics=("parallel", "parallel"),
            vmem_limit_bytes=_vmem_limit(vmem)),
        name="tail",
    )(x, r, g, four, gate_r, gate_f, wqk, wvo, w_ro, w_fo, w_mo, n_ca, n_mlp, w_up, w_dn, n_fin)


def _tiles(s):
    return min(512, s), min(512, s)


def _trunk(x, mem, p):
    b, s, d = x.shape
    dk = d // RET_HEADS
    gd = d // FOUR_GROUPS
    n1 = s // FFT_N2
    assert s % FFT_N2 == 0 and s % RET_CHUNK == 0
    tm_in, tm_tail = _tiles(s)
    cos, sin = (jnp.asarray(t) for t in _rope_tables(s, dk))
    dft = jnp.asarray(_chan_dft(gd)).astype(BF16)
    tab_a = jnp.asarray(_fft_a_table(n1, FFT_N2)).astype(BF16)
    tab_b = jnp.asarray(_fft_b_table(FFT_N2, float(s * gd) ** -0.5)).astype(BF16)

    for l in range(p["w_in"].shape[0]):
        wqk, wvo = _mem_kv(mem, p["norm_mem_w"][l], p["w_ck"][l], p["w_cv"][l],
                           p["w_cq"][l], p["w_co"][l])
        q, k, v, g, z, gate_r, gate_f, sb = _in_proj(x, p["norm_mix_w"][l], p["w_in"][l], cos, sin,
                                                     p["ret_decay_bwd"][l], tm_in)
        r, four = _mixers(q, k, v, sb, z, p["ret_decay_fwd"][l], p["ret_decay_bwd"][l],
                          dft, tab_a, tab_b)
        w_ro = _gain_rows(p["w_ret_out"][l], p["ret_gn_w"][l])
        x = _tail(x, r, g, four, gate_r, gate_f, wqk, wvo, w_ro, p["w_four_out"][l],
                  p["w_mix_out"][l], p["norm_ca_w"][l], p["norm_mlp_w"][l], p["w_up"][l],
                  p["w_down"][l], p["norm_final_w"], tm_tail)
    return x


def kernel(x_prompt, x_sample, mem_prompt, mem_sample, norm_mix_w, w_in, ret_decay_fwd,
           ret_decay_bwd, ret_gn_w, w_ret_out, w_four_out, w_mix_out, norm_ca_w, norm_mem_w,
           w_cq, w_ck, w_cv, w_co, norm_mlp_w, w_up, w_down, norm_final_w):
    depth, d = norm_mix_w.shape
    assert depth == 1, "the tail fuses the final norm into the (single) layer"
    vec = lambda w: w.reshape(depth, 1, d)
    mat = lambda w: w.astype(BF16)
    p = dict(
        norm_mix_w=vec(norm_mix_w), w_in=mat(w_in), ret_decay_fwd=ret_decay_fwd,
        ret_decay_bwd=ret_decay_bwd, ret_gn_w=ret_gn_w, w_ret_out=w_ret_out,
        w_four_out=mat(w_four_out), w_mix_out=mat(w_mix_out), norm_ca_w=vec(norm_ca_w),
        norm_mem_w=vec(norm_mem_w), w_cq=mat(w_cq), w_ck=mat(w_ck), w_cv=mat(w_cv),
        w_co=mat(w_co), norm_mlp_w=vec(norm_mlp_w), w_up=mat(w_up), w_down=mat(w_down),
        norm_final_w=norm_final_w.reshape(1, d))
    return _trunk(x_prompt, mem_prompt, p), _trunk(x_sample, mem_sample, p)
```

```python
import functools

import numpy as np
import jax
import jax.numpy as jnp
from jax import lax
from jax.experimental import pallas as pl
from jax.experimental.pallas import tpu as pltpu

RET_HEADS = 4
FOUR_GROUPS = 4
CA_HEADS = 4
ROPE_THETA = 10000.0
EPS = 1e-6
GN_EPS = 1e-5

V7X_VMEM_BYTES = 64 * 1024 * 1024
V7X_MXU_DIM = 256
V7X_LANES = 128
SUBLANES = 8
STAGE_PITCH = 12

RET_CHUNK = V7X_MXU_DIM
FFT_N2 = 128
ROW_BLOCKS = 2

BF16 = jnp.bfloat16
F32 = jnp.float32


COMPILER_SCRATCH_BYTES = 8 * 1024 * 1024
VMEM_RESERVE_BYTES = 6 * 1024 * 1024


def _vmem_limit(nbytes):
    return int(min(nbytes + COMPILER_SCRATCH_BYTES, V7X_VMEM_BYTES - VMEM_RESERVE_BYTES))


def _rms(x, w):
    return x * lax.rsqrt(jnp.mean(x * x, axis=-1, keepdims=True) + EPS) * w


def _dot(a, b):
    return jnp.dot(a, b, preferred_element_type=F32)


def _words(x):
    return pltpu.bitcast(x.astype(BF16), jnp.int32)


def _rows(w):
    return pltpu.bitcast(w, BF16)


def _resident(shape):
    return pl.BlockSpec(shape, lambda *_: (0,) * len(shape), pipeline_mode=pl.Buffered(1))


@functools.lru_cache(maxsize=None)
def _rope_tables(s, dk):
    inv = ROPE_THETA ** (-np.arange(0, dk, 2, dtype=np.float64) / dk)
    ang = np.arange(s, dtype=np.float64)[:, None] * inv[None, :]
    return np.cos(ang).astype(np.float32), np.sin(ang).astype(np.float32)


@functools.lru_cache(maxsize=None)
def _chan_dft(n):
    ang = 2.0 * np.pi * np.outer(np.arange(n), np.arange(n)) / n
    return np.concatenate([np.cos(ang), -np.sin(ang)], axis=1).astype(np.float32)


@functools.lru_cache(maxsize=None)
def _fft_a_table(n1, n2):
    n = n1 * n2
    k1 = np.arange(n1, dtype=np.float64)[None, :, None]
    tok = (n2 * np.arange(n1, dtype=np.float64)[None, None, :]
           + np.arange(n2, dtype=np.float64)[:, None, None])
    ang = 2.0 * np.pi * np.mod(k1 * tok, n) / n
    c, s = np.cos(ang), np.sin(ang)
    re_rows = np.concatenate([c, s], axis=2)
    im_rows = np.concatenate([-s, c], axis=2)
    return np.stack([re_rows, im_rows], axis=2).reshape(n2, 2 * n1, 2 * n1).astype(np.float32)


@functools.lru_cache(maxsize=None)
def _fft_b_table(n2, scale):
    ang = 2.0 * np.pi * np.outer(np.arange(n2), np.arange(n2)) / n2
    return (scale * np.stack([np.cos(ang), np.sin(ang)], axis=-1).reshape(n2, 2 * n2)).astype(np.float32)


def _mem_kv_kernel(mem_ref, nw_ref, wk_ref, wv_ref, wq_ref, wo_ref, wqk_ref, wvo_ref):
    m, d = mem_ref.shape[1], mem_ref.shape[2]
    dh = d // CA_HEADS
    mn = _rms(mem_ref[0], nw_ref[...]).astype(BF16)
    ck = _dot(mn, wk_ref[...]).astype(BF16)
    cv = _dot(mn, wv_ref[...]).astype(BF16)
    for h in range(CA_HEADS):
        cols = slice(h * dh, (h + 1) * dh)
        qk = lax.dot_general(wq_ref[:, cols], ck[:, cols], (((1,), (1,)), ((), ())),
                             preferred_element_type=F32) * (dh ** -0.5)
        wqk_ref[0, :, h * m:(h + 1) * m] = qk.astype(BF16)
        wvo_ref[0, h * m:(h + 1) * m, :] = _dot(cv[:, cols], wo_ref[cols, :]).astype(BF16)


def _mem_kv(mem, norm_w, w_ck, w_cv, w_cq, w_co):
    b, m, d = mem.shape
    hm = CA_HEADS * m
    sq = _resident((d, d))
    return pl.pallas_call(
        _mem_kv_kernel,
        out_shape=(jax.ShapeDtypeStruct((b, d, hm), BF16), jax.ShapeDtypeStruct((b, hm, d), BF16)),
        grid=(b,),
        in_specs=[pl.BlockSpec((1, m, d), lambda i: (i, 0, 0)), _resident((1, d)), sq, sq, sq, sq],
        out_specs=(pl.BlockSpec((1, d, hm), lambda i: (i, 0, 0)),
                   pl.BlockSpec((1, hm, d), lambda i: (i, 0, 0))),
        compiler_params=pltpu.CompilerParams(
            dimension_semantics=("arbitrary",),
            vmem_limit_bytes=_vmem_limit(4 * d * d * 2 + 4 * m * d * 4 + 4 * d * hm * 2
                                         + 2 * d * hm * 4)),
        name="mem_kv",
    )(mem, norm_w, w_ck, w_cv, w_cq, w_co)


def _log_sigmoid(x):
    return jnp.minimum(x, 0.0) - jnp.log1p(jnp.exp(-jnp.abs(x)))


def _tn_dot(a, b):
    return lax.dot_general(a, b, (((0,), (0,)), ((), ())), preferred_element_type=F32)


def _decay_rows(dec, c, dk, offset, sign):
    lg = _log_sigmoid(jnp.full((c, dk), dec, F32))
    row = lax.broadcasted_iota(jnp.int32, (c, dk), 0).astype(F32)
    return jnp.exp(lg * (offset + sign * row))


def _chunk_decay(dec, c, dk):
    return jnp.exp(_log_sigmoid(jnp.full((1, dk), dec, F32)) * c)


def _ret_bwd_init(dec_ref, state_ref, zeta_ref, gc_ref):
    c, dk = zeta_ref.shape[1], zeta_ref.shape[2]

    @pl.when(pl.program_id(1) == 0)
    def _init():
        state_ref[...] = jnp.zeros_like(state_ref)
        for h in range(RET_HEADS):
            zeta_ref[h] = _decay_rows(dec_ref[h], c, dk, 0.0, 1.0).astype(BF16)
            gc_ref[h] = _chunk_decay(dec_ref[h], c, dk)


def _ret_bwd_states(k_ref, v_ref, sb_ref, state_ref, zeta_ref, gc_ref):
    c = RET_CHUNK
    t_len, d = k_ref.shape[1], k_ref.shape[2]
    dk = d // RET_HEADS
    for h in range(RET_HEADS):
        cols = slice(h * dk, (h + 1) * dk)
        state = state_ref[h]
        for ci in reversed(range(t_len // c)):
            rows = slice(ci * c, (ci + 1) * c)
            sb_ref[0, ci, h] = state.astype(BF16)
            state = state * gc_ref[h] + _tn_dot(k_ref[0, rows, cols],
                                                v_ref[0, rows, cols] * zeta_ref[h])
        state_ref[h] = state


def _in_proj_kernel(dec_ref, x_ref, nw_ref, w_ref, cos_ref, sin_ref,
                    q_ref, k_ref, v_ref, g_ref, z_ref, gr_ref, gf_ref, sb_ref,
                    perm_ref, state_ref, zeta_ref, gc_ref):
    _ret_bwd_init(dec_ref, state_ref, zeta_ref, gc_ref)
    d = x_ref.shape[-1]
    dk = d // RET_HEADS
    half = dk // 2
    tm = x_ref.shape[1]
    hb = tm // ROW_BLOCKS
    blocks = [slice(i * hb, (i + 1) * hb) for i in range(ROW_BLOCKS)]
    xb = [_rms(x_ref[0, rows, :], nw_ref[...]).astype(BF16) for rows in blocks]

    def proj(j):
        return [_dot(b, w_ref[:, j * d:(j + 1) * d]) for b in xb]

    def rope(ps, out_ref, scale):
        for p, rows in zip(ps, blocks):
            cos = cos_ref[rows, :]
            sin = sin_ref[rows, :]
            for h in range(RET_HEADS):
                x1 = p[:, h * dk:h * dk + half]
                x2 = p[:, h * dk + half:(h + 1) * dk]
                o1 = x1 * cos - x2 * sin
                o2 = x2 * cos + x1 * sin
                if scale is not None:
                    o1 = o1 * scale
                    o2 = o2 * scale
                out_ref[0, rows, h * dk:h * dk + half] = o1.astype(BF16)
                out_ref[0, rows, h * dk + half:(h + 1) * dk] = o2.astype(BF16)

    def store(ps, out_ref):
        for p, rows in zip(ps, blocks):
            out_ref[0, rows, :] = p.astype(BF16)

    rope(proj(0), q_ref, dk ** -0.5)
    rope(proj(1), k_ref, None)
    store(proj(2), v_ref)
    _ret_bwd_states(k_ref, v_ref, sb_ref, state_ref, zeta_ref, gc_ref)
    store(proj(3), g_ref)
    us = proj(4)
    for lc in range(d // V7X_LANES):
        for blk in range(tm // FFT_N2):
            a, par = divmod(blk, 2)
            u = us[blk * FFT_N2 // hb]
            r0 = blk * FFT_N2 % hb
            perm_ref[lc, pl.ds(2 * FFT_N2 * a + par, FFT_N2, stride=2), :] = (
                u[r0:r0 + FFT_N2, lc * V7X_LANES:(lc + 1) * V7X_LANES])
    for lc in range(d // V7X_LANES):
        z_ref[0, :, lc * V7X_LANES:(lc + 1) * V7X_LANES] = _words(perm_ref[lc])
    store(proj(5), gr_ref)
    store(proj(6), gf_ref)


def _in_proj(x, norm_w, w_in, cos, sin, decay_bwd, tm):
    b, s, d = x.shape
    dk = d // RET_HEADS
    half = dk // 2
    c = RET_CHUNK
    nt = s // tm
    act = jax.ShapeDtypeStruct((b, s, d), BF16)
    assert tm % (2 * FFT_N2) == 0 and tm % c == 0 and (tm // ROW_BLOCKS) % FFT_N2 == 0
    zsh = jax.ShapeDtypeStruct((b, s // 2, d), jnp.int32)
    sbsh = jax.ShapeDtypeStruct((b, s // c, RET_HEADS, dk, dk), BF16)
    row = pl.BlockSpec((1, tm, d), lambda i, j: (i, nt - 1 - j, 0))
    zrow = pl.BlockSpec((1, tm // 2, d), lambda i, j: (i, nt - 1 - j, 0))
    sbrow = pl.BlockSpec((1, tm // c, RET_HEADS, dk, dk), lambda i, j: (i, nt - 1 - j, 0, 0, 0))
    tab = pl.BlockSpec((tm, half), lambda i, j: (nt - 1 - j, 0))
    vmem = (w_in.size * 2 + 2 * tm * d * 4 + 2 * 8 * tm * d * 2 + 4 * tm * half * 4
            + 5 * tm * d * 4 + 2 * tm * d * 2 + RET_HEADS * (dk * dk * 4 + c * dk * 2))
    return pl.pallas_call(
        _in_proj_kernel,
        out_shape=(act, act, act, act, zsh, act, act, sbsh),
        grid=(b, nt),
        in_specs=[pl.BlockSpec(memory_space=pltpu.SMEM), row, _resident((1, d)),
                  _resident(w_in.shape), tab, tab],
        out_specs=(row, row, row, row, zrow, row, row, sbrow),
        scratch_shapes=[pltpu.VMEM((d // V7X_LANES, tm, V7X_LANES), F32),
                        pltpu.VMEM((RET_HEADS, dk, dk), F32),
                        pltpu.VMEM((RET_HEADS, c, dk), BF16),
                        pltpu.VMEM((RET_HEADS, 1, dk), F32)],
        compiler_params=pltpu.CompilerParams(
            dimension_semantics=("arbitrary", "arbitrary"),
            vmem_limit_bytes=_vmem_limit(vmem)),
        name="in_proj",
    )(decay_bwd, x, norm_w, w_in, cos, sin)


def _ret_fwd_kernel(decf_ref, decb_ref, q_ref, k_ref, v_ref, sb_ref, r_ref,
                    state_ref, dmat_ref, xif_ref, xib_ref, zeta_ref, gc_ref):
    c = RET_CHUNK
    t_len, d = q_ref.shape[1], q_ref.shape[2]
    dk = d // RET_HEADS

    @pl.when(pl.program_id(1) == 0)
    def _init():
        state_ref[...] = jnp.zeros_like(state_ref)
        for h in range(RET_HEADS):
            xif_ref[h] = _decay_rows(decf_ref[h], c, dk, 1.0, 1.0).astype(BF16)
            xib_ref[h] = _decay_rows(decb_ref[h], c, dk, float(c), -1.0).astype(BF16)
            zeta_ref[h] = _decay_rows(decf_ref[h], c, dk, c - 1.0, -1.0).astype(BF16)
            gc_ref[h] = _chunk_decay(decf_ref[h], c, dk)
            i = lax.broadcasted_iota(jnp.int32, (c, c), 0)
            j = lax.broadcasted_iota(jnp.int32, (c, c), 1)
            diff = (i - j).astype(F32)
            lgf = _log_sigmoid(jnp.full((c, c), decf_ref[h], F32))
            lgb = _log_sigmoid(jnp.full((c, c), decb_ref[h], F32))
            dmat_ref[h] = jnp.where(i >= j, jnp.exp(lgf * jnp.maximum(diff, 0.0)),
                                    jnp.exp(lgb * jnp.maximum(-diff, 0.0))).astype(BF16)

    states = [state_ref[h] for h in range(RET_HEADS)]
    for ci in range(t_len // c):
        rows = slice(ci * c, (ci + 1) * c)
        scores = []
        for h in range(RET_HEADS):
            cols = slice(h * dk, (h + 1) * dk)
            scores.append(lax.dot_general(q_ref[0, rows, cols], k_ref[0, rows, cols],
                                          (((1,), (1,)), ((), ())), preferred_element_type=F32))
        for h in range(RET_HEADS):
            cols = slice(h * dk, (h + 1) * dk)
            q = q_ref[0, rows, cols]
            v = v_ref[0, rows, cols]
            lhs = jnp.concatenate([scores[h].astype(BF16) * dmat_ref[h],
                                   q * xif_ref[h], q * xib_ref[h]], axis=1)
            rhs = jnp.concatenate([v, states[h].astype(BF16), sb_ref[0, ci, h]], axis=0)
            y = _dot(lhs, rhs)
            states[h] = states[h] * gc_ref[h] + _tn_dot(k_ref[0, rows, cols], v * zeta_ref[h])
            yc = y - jnp.mean(y, axis=-1, keepdims=True)
            var = jnp.mean(yc * yc, axis=-1, keepdims=True)
            r_ref[0, rows, cols] = (yc * lax.rsqrt(var + GN_EPS)).astype(BF16)
    for h in range(RET_HEADS):
        state_ref[h] = states[h]


def _fft_a_kernel(z_ref, dft_ref, tab_ref, o_ref, zin_ref, zout_ref):
    na, rn, d = z_ref.shape[1], z_ref.shape[2], z_ref.shape[3]
    gd = dft_ref.shape[0]
    lanes = range(d // V7X_LANES)
    for a in range(na):
        for lc in lanes:
            zin_ref[lc, a * STAGE_PITCH:a * STAGE_PITCH + rn, :] = (
                z_ref[0, a, :, lc * V7X_LANES:(lc + 1) * V7X_LANES])
    u = jnp.concatenate(
        [_rows(jnp.concatenate([zin_ref[lc, pl.ds(r, na, stride=STAGE_PITCH), :] for lc in lanes],
                               axis=1)) for r in range(rn)], axis=0)
    n1 = u.shape[0] // rn
    parts = [_dot(u[:, gi * gd:(gi + 1) * gd], dft_ref[...]) for gi in range(d // gd)]
    re = jnp.concatenate([p[:, :gd] for p in parts], axis=1).astype(BF16)
    im = jnp.concatenate([p[:, gd:] for p in parts], axis=1).astype(BF16)
    for r in range(rn):
        rows = slice(r * n1, (r + 1) * n1)
        zz = jnp.concatenate([re[rows], im[rows]], axis=0)
        w = _words(_dot(tab_ref[r], zz))
        for lc in lanes:
            zout_ref[lc, pl.ds(r, n1, stride=STAGE_PITCH), :] = w[:, lc * V7X_LANES:(lc + 1) * V7X_LANES]
    for k in range(n1):
        for lc in lanes:
            o_ref[0, k, :, lc * V7X_LANES:(lc + 1) * V7X_LANES] = (
                zout_ref[lc, k * STAGE_PITCH:k * STAGE_PITCH + rn, :])


def _fft_b_kernel(zt_ref, tab_ref, o_ref):
    for i in range(zt_ref.shape[1]):
        o_ref[0, :, i, :] = _words(_dot(tab_ref[...], _rows(zt_ref[0, i])))


def _ret_fwd_fft_b_kernel(decf_ref, decb_ref, q_ref, k_ref, v_ref, sb_ref,
                          zt_ref, tab_ref, r_ref, f_ref,
                          state_ref, dmat_ref, xif_ref, xib_ref, zeta_ref, gc_ref):
    _ret_fwd_kernel(decf_ref, decb_ref, q_ref, k_ref, v_ref, sb_ref, r_ref,
                    state_ref, dmat_ref, xif_ref, xib_ref, zeta_ref, gc_ref)
    _fft_b_kernel(zt_ref, tab_ref, f_ref)


def _mixers(q, k, v, sb, z, decay_fwd, decay_bwd, dft, tab_a, tab_b):
    b, s, d = q.shape
    dk = d // RET_HEADS
    c = RET_CHUNK
    n2 = tab_b.shape[0]
    n1 = s // n2
    steps_b = n1 // SUBLANES
    t_b = s // steps_b
    assert t_b % c == 0
    smem = pl.BlockSpec(memory_space=pltpu.SMEM)
    fwd = pl.BlockSpec((1, t_b, d), lambda i, t: (i, t, 0))
    xi = pltpu.VMEM((RET_HEADS, c, dk), BF16)
    table_bytes = RET_HEADS * (3 * c * dk + c * c + dk * dk + dk) * 4
    stage_b_bytes = 2 * (SUBLANES * n2 * d * 4 + (n2 // 2) * SUBLANES * d * 4) + 4 * n2 * d * 4
    zt = pl.pallas_call(
        _fft_a_kernel,
        out_shape=jax.ShapeDtypeStruct((b, n1, n2, d), jnp.int32),
        grid=(b, n2 // SUBLANES),
        in_specs=[pl.BlockSpec((1, n1 // 2, SUBLANES, d), lambda i, t: (i, 0, t, 0)),
                  _resident(dft.shape),
                  pl.BlockSpec((SUBLANES, 2 * n1, 2 * n1), lambda i, t: (t, 0, 0))],
        out_specs=pl.BlockSpec((1, n1, SUBLANES, d), lambda i, t: (i, 0, t, 0)),
        scratch_shapes=[pltpu.VMEM((d // V7X_LANES, n1 // 2 * STAGE_PITCH, V7X_LANES), jnp.int32),
                        pltpu.VMEM((d // V7X_LANES, n1 * STAGE_PITCH, V7X_LANES), jnp.int32)],
        compiler_params=pltpu.CompilerParams(
            dimension_semantics=("parallel", "parallel"),
            vmem_limit_bytes=_vmem_limit(3 * n1 // 2 * STAGE_PITCH * d * 4
                                         + 2 * (3 * n1 * SUBLANES * d * 2
                                              + SUBLANES * 4 * n1 * n1 * 2)
                                         + 5 * n1 * SUBLANES * d * 4)),
        name="fft_a",
    )(z.reshape(b, n1 // 2, n2, d), dft, tab_a)
    return pl.pallas_call(
        _ret_fwd_fft_b_kernel,
        out_shape=(jax.ShapeDtypeStruct((b, s, d), BF16),
                   jax.ShapeDtypeStruct((b, n2 // 2, n1, d), jnp.int32)),
        grid=(b, steps_b),
        in_specs=[smem, smem, fwd, fwd, fwd,
                  pl.BlockSpec((1, t_b // c, RET_HEADS, dk, dk), lambda i, t: (i, t, 0, 0, 0)),
                  pl.BlockSpec((1, SUBLANES, n2, d), lambda i, t: (i, t, 0, 0)),
                  _resident((n2, 2 * n2))],
        out_specs=(fwd, pl.BlockSpec((1, n2 // 2, SUBLANES, d), lambda i, t: (i, 0, t, 0))),
        scratch_shapes=[pltpu.VMEM((RET_HEADS, dk, dk), F32), pltpu.VMEM((RET_HEADS, c, c), BF16),
                        xi, xi, pltpu.VMEM((RET_HEADS, c, dk), BF16),
                        pltpu.VMEM((RET_HEADS, 1, dk), F32)],
        compiler_params=pltpu.CompilerParams(
            dimension_semantics=("arbitrary", "arbitrary"),
            vmem_limit_bytes=_vmem_limit(2 * 5 * t_b * d * 2 + table_bytes + stage_b_bytes)),
        name="ret_fwd_fft_b",
    )(decay_fwd, decay_bwd, q, k, v, sb, zt, tab_b)


def _gain_rows_kernel(w_ref, g_ref, o_ref):
    o_ref[...] = (w_ref[...] * g_ref[...]).astype(BF16)


def _gain_rows(w, gain):
    d_in, d_out = w.shape
    rows = min(d_in, V7X_MXU_DIM)
    return pl.pallas_call(
        _gain_rows_kernel,
        out_shape=jax.ShapeDtypeStruct((d_in, d_out), BF16),
        grid=(d_in // rows,),
        in_specs=[pl.BlockSpec((rows, d_out), lambda i: (i, 0)),
                  pl.BlockSpec((rows, 1), lambda i: (i, 0))],
        out_specs=pl.BlockSpec((rows, d_out), lambda i: (i, 0)),
        compiler_params=pltpu.CompilerParams(dimension_semantics=("parallel",)),
        name="gain_rows",
    )(w, gain.reshape(d_in, 1))


def _tail_kernel(x_ref, r_ref, g_ref, f_ref, gr_ref, gf_ref, wqk_ref, wvo_ref,
                 w_ro_ref, w_fo_ref, w_mo_ref, n_ca_ref,
                 n_mlp_ref, w_up_ref, w_dn_ref, n_fin_ref, o_ref, att_ref, perm_ref):
    d = x_ref.shape[-1]
    m = wqk_ref.shape[2] // CA_HEADS
    ff = w_up_ref.shape[1]
    gq, n1 = f_ref.shape[1], f_ref.shape[2]
    hb = x_ref.shape[1] // ROW_BLOCKS
    lanes = range(d // V7X_LANES)
    blocks = [slice(i * hb, (i + 1) * hb) for i in range(ROW_BLOCKS)]
    def gated(rows):
        g = g_ref[0, rows, :]
        return g * jax.nn.sigmoid(g) * r_ref[0, rows, :]

    ret = [_dot(gated(rows), w_ro_ref[...]) for rows in blocks]
    for i, rows in enumerate(blocks):
        g0, g1 = i * gq // ROW_BLOCKS, (i + 1) * gq // ROW_BLOCKS
        fperm = _dot(_rows(f_ref[0, g0:g1].reshape(hb // 2, d)), w_fo_ref[...])
        for lc in lanes:
            perm_ref[lc, rows, :] = fperm[:, lc * V7X_LANES:(lc + 1) * V7X_LANES]
    xs = []
    for i, rows in enumerate(blocks):
        fou = jnp.concatenate(
            [jnp.concatenate([perm_ref[lc, pl.ds(i * hb + blk // 2 * 2 * n1 + blk % 2, n1,
                                                 stride=2), :] for lc in lanes], axis=1)
             for blk in range(2 * gq // ROW_BLOCKS)], axis=0)
        merged = (jax.nn.sigmoid(gr_ref[0, rows, :].astype(F32)) * ret[i]
                  + jax.nn.sigmoid(gf_ref[0, rows, :].astype(F32)) * fou)
        xs.append(x_ref[0, rows, :] + _dot(merged.astype(BF16), w_mo_ref[...]))

    logits = [_dot(_rms(x, n_ca_ref[...]).astype(BF16), wqk_ref[0]) for x in xs]
    for lg, rows in zip(logits, blocks):
        for h in range(CA_HEADS):
            cols = slice(h * m, (h + 1) * m)
            p = jnp.exp(lg[:, cols] - jnp.max(lg[:, cols], axis=-1, keepdims=True))
            att_ref[rows, cols] = (p / jnp.sum(p, axis=-1, keepdims=True)).astype(BF16)
    xs = [x + _dot(att_ref[rows, :], wvo_ref[0]) for x, rows in zip(xs, blocks)]

    xb = [_rms(x, n_mlp_ref[...]).astype(BF16) for x in xs]
    for j in range(ff // d):
        cols = slice(j * d, (j + 1) * d)
        hid = [jnp.square(jnp.maximum(_dot(b, w_up_ref[:, cols]), 0.0)).astype(BF16) for b in xb]
        xs = [x + _dot(hd, w_dn_ref[cols, :]) for x, hd in zip(xs, hid)]
    for x, rows in zip(xs, blocks):
        o_ref[0, rows, :] = _rms(x, n_fin_ref[...])


def _tail(x, r, g, four, gate_r, gate_f, wqk, wvo, w_ro, w_fo, w_mo, n_ca,
          n_mlp, w_up, w_dn, n_fin, tm):
    b, s, d = x.shape
    hm = wqk.shape[2]
    ff = w_up.shape[1]
    row = pl.BlockSpec((1, tm, d), lambda i, j: (i, j, 0))
    n1 = four.shape[2]
    assert tm % (2 * n1 * ROW_BLOCKS) == 0
    half = pl.BlockSpec((1, tm // (2 * n1), n1, d), lambda i, j: (i, j, 0, 0))
    vec = _resident((1, d))
    sq = _resident((d, d))
    vmem = ((3 * d * d + 2 * d * ff) * 2 + 4 * d * hm * 2 + 2 * tm * d * (4 + 4 + 5 * 2)
            + tm * d * 4 * 8)
    return pl.pallas_call(
        _tail_kernel,
        out_shape=jax.ShapeDtypeStruct((b, s, d), F32),
        grid=(b, s // tm),
        in_specs=[row, row, row, half, row, row,
                  pl.BlockSpec((1, d, hm), lambda i, j: (i, 0, 0)),
                  pl.BlockSpec((1, hm, d), lambda i, j: (i, 0, 0)),
                  sq, sq, sq, vec, vec, _resident((d, ff)), _resident((ff, d)), vec],
        out_specs=row,
        scratch_shapes=[pltpu.VMEM((tm, hm), BF16),
                        pltpu.VMEM((d // V7X_LANES, tm, V7X_LANES), F32)],
        compiler_params=pltpu.CompilerParams(
            dimension_semantics=("parallel", "parallel"),
            vmem_limit_bytes=_vmem_limit(vmem)),
        name="tail",
    )(x, r, g, four, gate_r, gate_f, wqk, wvo, w_ro, w_fo, w_mo, n_ca, n_mlp, w_up, w_dn, n_fin)


def _tiles(s):
    return min(512, s), min(512, s)


def _trunk(x, mem, p):
    b, s, d = x.shape
    dk = d // RET_HEADS
    gd = d // FOUR_GROUPS
    n1 = s // FFT_N2
    assert s % FFT_N2 == 0 and s % RET_CHUNK == 0
    tm_in, tm_tail = _tiles(s)
    cos, sin = (jnp.asarray(t) for t in _rope_tables(s, dk))
    dft = jnp.asarray(_chan_dft(gd)).astype(BF16)
    tab_a = jnp.asarray(_fft_a_table(n1, FFT_N2)).astype(BF16)
    tab_b = jnp.asarray(_fft_b_table(FFT_N2, float(s * gd) ** -0.5)).astype(BF16)

    for l in range(p["w_in"].shape[0]):
        wqk, wvo = _mem_kv(mem, p["norm_mem_w"][l], p["w_ck"][l], p["w_cv"][l],
                           p["w_cq"][l], p["w_co"][l])
        q, k, v, g, z, gate_r, gate_f, sb = _in_proj(x, p["norm_mix_w"][l], p["w_in"][l], cos, sin,
                                                     p["ret_decay_bwd"][l], tm_in)
        r, four = _mixers(q, k, v, sb, z, p["ret_decay_fwd"][l], p["ret_decay_bwd"][l],
                          dft, tab_a, tab_b)
        w_ro = _gain_rows(p["w_ret_out"][l], p["ret_gn_w"][l])
        x = _tail(x, r, g, four, gate_r, gate_f, wqk, wvo, w_ro, p["w_four_out"][l],
                  p["w_mix_out"][l], p["norm_ca_w"][l], p["norm_mlp_w"][l], p["w_up"][l],
                  p["w_down"][l], p["norm_final_w"], tm_tail)
    return x


def kernel(x_prompt, x_sample, mem_prompt, mem_sample, norm_mix_w, w_in, ret_decay_fwd,
           ret_decay_bwd, ret_gn_w, w_ret_out, w_four_out, w_mix_out, norm_ca_w, norm_mem_w,
           w_cq, w_ck, w_cv, w_co, norm_mlp_w, w_up, w_down, norm_final_w):
    depth, d = norm_mix_w.shape
    assert depth == 1, "the tail fuses the final norm into the (single) layer"
    vec = lambda w: w.reshape(depth, 1, d)
    mat = lambda w: w.astype(BF16)
    p = dict(
        norm_mix_w=vec(norm_mix_w), w_in=mat(w_in), ret_decay_fwd=ret_decay_fwd,
        ret_decay_bwd=ret_decay_bwd, ret_gn_w=ret_gn_w, w_ret_out=w_ret_out,
        w_four_out=mat(w_four_out), w_mix_out=mat(w_mix_out), norm_ca_w=vec(norm_ca_w),
        norm_mem_w=vec(norm_mem_w), w_cq=mat(w_cq), w_ck=mat(w_ck), w_cv=mat(w_cv),
        w_co=mat(w_co), norm_mlp_w=vec(norm_mlp_w), w_up=mat(w_up), w_down=mat(w_down),
        norm_final_w=norm_final_w.reshape(1, d))
    return _trunk(x_prompt, mem_prompt, p), _trunk(x_sample, mem_sample, p)
```

```python
import functools

import numpy as np
import jax
import jax.numpy as jnp
from jax import lax
from jax.experimental import pallas as pl
from jax.experimental.pallas import tpu as pltpu

RET_HEADS = 4
FOUR_GROUPS = 4
CA_HEADS = 4
ROPE_THETA = 10000.0
EPS = 1e-6
GN_EPS = 1e-5

V7X_VMEM_BYTES = 64 * 1024 * 1024
V7X_MXU_DIM = 256
V7X_LANES = 128
SUBLANES = 8
STAGE_PITCH = 12

RET_CHUNK = V7X_MXU_DIM
FFT_N2 = 128
ROW_BLOCKS = 2
PIPELINE_DEPTH = 3

BF16 = jnp.bfloat16
F32 = jnp.float32


COMPILER_SCRATCH_BYTES = 8 * 1024 * 1024
VMEM_RESERVE_BYTES = 6 * 1024 * 1024


def _vmem_limit(nbytes):
    return int(min(nbytes + COMPILER_SCRATCH_BYTES, V7X_VMEM_BYTES - VMEM_RESERVE_BYTES))


def _rms(x, w):
    return x * lax.rsqrt(jnp.mean(x * x, axis=-1, keepdims=True) + EPS) * w


def _dot(a, b):
    return jnp.dot(a, b, preferred_element_type=F32)


def _words(x):
    return pltpu.bitcast(x.astype(BF16), jnp.int32)


def _rows(w):
    return pltpu.bitcast(w, BF16)


def _resident(shape):
    return pl.BlockSpec(shape, lambda *_: (0,) * len(shape), pipeline_mode=pl.Buffered(1))


@functools.lru_cache(maxsize=None)
def _rope_tables(s, dk):
    inv = ROPE_THETA ** (-np.arange(0, dk, 2, dtype=np.float64) / dk)
    ang = np.arange(s, dtype=np.float64)[:, None] * inv[None, :]
    return np.cos(ang).astype(np.float32), np.sin(ang).astype(np.float32)


@functools.lru_cache(maxsize=None)
def _chan_dft(n):
    ang = 2.0 * np.pi * np.outer(np.arange(n), np.arange(n)) / n
    return np.concatenate([np.cos(ang), -np.sin(ang)], axis=1).astype(np.float32)


@functools.lru_cache(maxsize=None)
def _fft_a_table(n1, n2):
    n = n1 * n2
    k1 = np.arange(n1, dtype=np.float64)[None, :, None]
    tok = (n2 * np.arange(n1, dtype=np.float64)[None, None, :]
           + np.arange(n2, dtype=np.float64)[:, None, None])
    ang = 2.0 * np.pi * np.mod(k1 * tok, n) / n
    c, s = np.cos(ang), np.sin(ang)
    re_rows = np.concatenate([c, s], axis=2)
    im_rows = np.concatenate([-s, c], axis=2)
    return np.stack([re_rows, im_rows], axis=2).reshape(n2, 2 * n1, 2 * n1).astype(np.float32)


@functools.lru_cache(maxsize=None)
def _fft_b_table(n2, scale):
    ang = 2.0 * np.pi * np.outer(np.arange(n2), np.arange(n2)) / n2
    return (scale * np.stack([np.cos(ang), np.sin(ang)], axis=-1).reshape(n2, 2 * n2)).astype(np.float32)


def _mem_kv_kernel(mem_ref, nw_ref, wk_ref, wv_ref, wq_ref, wo_ref, wqk_ref, wvo_ref):
    m, d = mem_ref.shape[1], mem_ref.shape[2]
    dh = d // CA_HEADS
    mn = _rms(mem_ref[0], nw_ref[...]).astype(BF16)
    ck = _dot(mn, wk_ref[...]).astype(BF16)
    cv = _dot(mn, wv_ref[...]).astype(BF16)
    for h in range(CA_HEADS):
        cols = slice(h * dh, (h + 1) * dh)
        qk = lax.dot_general(wq_ref[:, cols], ck[:, cols], (((1,), (1,)), ((), ())),
                             preferred_element_type=F32) * (dh ** -0.5)
        wqk_ref[0, :, h * m:(h + 1) * m] = qk.astype(BF16)
        wvo_ref[0, h * m:(h + 1) * m, :] = _dot(cv[:, cols], wo_ref[cols, :]).astype(BF16)


def _mem_kv(mem, norm_w, w_ck, w_cv, w_cq, w_co):
    b, m, d = mem.shape
    hm = CA_HEADS * m
    sq = _resident((d, d))
    return pl.pallas_call(
        _mem_kv_kernel,
        out_shape=(jax.ShapeDtypeStruct((b, d, hm), BF16), jax.ShapeDtypeStruct((b, hm, d), BF16)),
        grid=(b,),
        in_specs=[pl.BlockSpec((1, m, d), lambda i: (i, 0, 0)), _resident((1, d)), sq, sq, sq, sq],
        out_specs=(pl.BlockSpec((1, d, hm), lambda i: (i, 0, 0)),
                   pl.BlockSpec((1, hm, d), lambda i: (i, 0, 0))),
        compiler_params=pltpu.CompilerParams(
            dimension_semantics=("arbitrary",),
            vmem_limit_bytes=_vmem_limit(4 * d * d * 2 + 4 * m * d * 4 + 4 * d * hm * 2
                                         + 2 * d * hm * 4)),
        name="mem_kv",
    )(mem, norm_w, w_ck, w_cv, w_cq, w_co)


def _log_sigmoid(x):
    return jnp.minimum(x, 0.0) - jnp.log1p(jnp.exp(-jnp.abs(x)))


def _tn_dot(a, b):
    return lax.dot_general(a, b, (((0,), (0,)), ((), ())), preferred_element_type=F32)


def _decay_rows(dec, c, dk, offset, sign):
    lg = _log_sigmoid(jnp.full((c, dk), dec, F32))
    row = lax.broadcasted_iota(jnp.int32, (c, dk), 0).astype(F32)
    return jnp.exp(lg * (offset + sign * row))


def _chunk_decay(dec, c, dk):
    return jnp.exp(_log_sigmoid(jnp.full((1, dk), dec, F32)) * c)


def _ret_bwd_init(dec_ref, state_ref, zeta_ref, gc_ref):
    c, dk = zeta_ref.shape[1], zeta_ref.shape[2]

    @pl.when(pl.program_id(1) == 0)
    def _init():
        state_ref[...] = jnp.zeros_like(state_ref)
        for h in range(RET_HEADS):
            zeta_ref[h] = _decay_rows(dec_ref[h], c, dk, 0.0, 1.0).astype(BF16)
            gc_ref[h] = _chunk_decay(dec_ref[h], c, dk)


def _ret_bwd_states(k_ref, v_ref, sb_ref, state_ref, zeta_ref, gc_ref):
    c = RET_CHUNK
    t_len, d = k_ref.shape[1], k_ref.shape[2]
    dk = d // RET_HEADS
    for h in range(RET_HEADS):
        cols = slice(h * dk, (h + 1) * dk)
        state = state_ref[h]
        for ci in reversed(range(t_len // c)):
            rows = slice(ci * c, (ci + 1) * c)
            sb_ref[0, ci, h] = state.astype(BF16)
            state = state * gc_ref[h] + _tn_dot(k_ref[0, rows, cols],
                                                v_ref[0, rows, cols] * zeta_ref[h])
        state_ref[h] = state


def _in_proj_kernel(dec_ref, x_ref, nw_ref, w_ref, cos_ref, sin_ref,
                    q_ref, k_ref, v_ref, g_ref, z_ref, gr_ref, gf_ref, sb_ref,
                    perm_ref, state_ref, zeta_ref, gc_ref):
    _ret_bwd_init(dec_ref, state_ref, zeta_ref, gc_ref)
    d = x_ref.shape[-1]
    dk = d // RET_HEADS
    half = dk // 2
    tm = x_ref.shape[1]
    hb = tm // ROW_BLOCKS
    blocks = [slice(i * hb, (i + 1) * hb) for i in range(ROW_BLOCKS)]
    xb = [_rms(x_ref[0, rows, :], nw_ref[...]).astype(BF16) for rows in blocks]

    def proj(j):
        return [_dot(b, w_ref[:, j * d:(j + 1) * d]) for b in xb]

    def rope(ps, out_ref, scale):
        for p, rows in zip(ps, blocks):
            cos = cos_ref[rows, :]
            sin = sin_ref[rows, :]
            for h in range(RET_HEADS):
                x1 = p[:, h * dk:h * dk + half]
                x2 = p[:, h * dk + half:(h + 1) * dk]
                o1 = x1 * cos - x2 * sin
                o2 = x2 * cos + x1 * sin
                if scale is not None:
                    o1 = o1 * scale
                    o2 = o2 * scale
                out_ref[0, rows, h * dk:h * dk + half] = o1.astype(BF16)
                out_ref[0, rows, h * dk + half:(h + 1) * dk] = o2.astype(BF16)

    def store(ps, out_ref):
        for p, rows in zip(ps, blocks):
            out_ref[0, rows, :] = p.astype(BF16)

    rope(proj(0), q_ref, dk ** -0.5)
    rope(proj(1), k_ref, None)
    store(proj(2), v_ref)
    _ret_bwd_states(k_ref, v_ref, sb_ref, state_ref, zeta_ref, gc_ref)
    store(proj(3), g_ref)
    us = proj(4)
    for lc in range(d // V7X_LANES):
        for blk in range(tm // FFT_N2):
            a, par = divmod(blk, 2)
            u = us[blk * FFT_N2 // hb]
            r0 = blk * FFT_N2 % hb
            perm_ref[lc, pl.ds(2 * FFT_N2 * a + par, FFT_N2, stride=2), :] = (
                u[r0:r0 + FFT_N2, lc * V7X_LANES:(lc + 1) * V7X_LANES])
    for lc in range(d // V7X_LANES):
        z_ref[0, :, lc * V7X_LANES:(lc + 1) * V7X_LANES] = _words(perm_ref[lc])
    store(proj(5), gr_ref)
    store(proj(6), gf_ref)


def _in_proj(x, norm_w, w_in, cos, sin, decay_bwd, tm):
    b, s, d = x.shape
    dk = d // RET_HEADS
    half = dk // 2
    c = RET_CHUNK
    nt = s // tm
    act = jax.ShapeDtypeStruct((b, s, d), BF16)
    assert tm % (2 * FFT_N2) == 0 and tm % c == 0 and (tm // ROW_BLOCKS) % FFT_N2 == 0
    zsh = jax.ShapeDtypeStruct((b, s // 2, d), jnp.int32)
    sbsh = jax.ShapeDtypeStruct((b, s // c, RET_HEADS, dk, dk), BF16)
    row = pl.BlockSpec((1, tm, d), lambda i, j: (i, nt - 1 - j, 0))
    zrow = pl.BlockSpec((1, tm // 2, d), lambda i, j: (i, nt - 1 - j, 0))
    sbrow = pl.BlockSpec((1, tm // c, RET_HEADS, dk, dk), lambda i, j: (i, nt - 1 - j, 0, 0, 0))
    tab = pl.BlockSpec((tm, half), lambda i, j: (nt - 1 - j, 0))
    vmem = (w_in.size * 2 + 2 * tm * d * 4 + 2 * 8 * tm * d * 2 + 4 * tm * half * 4
            + 5 * tm * d * 4 + 2 * tm * d * 2 + RET_HEADS * (dk * dk * 4 + c * dk * 2))
    return pl.pallas_call(
        _in_proj_kernel,
        out_shape=(act, act, act, act, zsh, act, act, sbsh),
        grid=(b, nt),
        in_specs=[pl.BlockSpec(memory_space=pltpu.SMEM), row, _resident((1, d)),
                  _resident(w_in.shape), tab, tab],
        out_specs=(row, row, row, row, zrow, row, row, sbrow),
        scratch_shapes=[pltpu.VMEM((d // V7X_LANES, tm, V7X_LANES), F32),
                        pltpu.VMEM((RET_HEADS, dk, dk), F32),
                        pltpu.VMEM((RET_HEADS, c, dk), BF16),
                        pltpu.VMEM((RET_HEADS, 1, dk), F32)],
        compiler_params=pltpu.CompilerParams(
            dimension_semantics=("arbitrary", "arbitrary"),
            vmem_limit_bytes=_vmem_limit(vmem)),
        name="in_proj",
    )(decay_bwd, x, norm_w, w_in, cos, sin)


def _ret_fwd_init(decf_ref, decb_ref, state_ref, dmat_ref, xif_ref, xib_ref, zeta_ref, gc_ref):
    c, dk = xif_ref.shape[1], xif_ref.shape[2]
    state_ref[...] = jnp.zeros_like(state_ref)
    if True:
        for h in range(RET_HEADS):
            xif_ref[h] = _decay_rows(decf_ref[h], c, dk, 1.0, 1.0).astype(BF16)
            xib_ref[h] = _decay_rows(decb_ref[h], c, dk, float(c), -1.0).astype(BF16)
            zeta_ref[h] = _decay_rows(decf_ref[h], c, dk, c - 1.0, -1.0).astype(BF16)
            gc_ref[h] = _chunk_decay(decf_ref[h], c, dk)
            i = lax.broadcasted_iota(jnp.int32, (c, c), 0)
            j = lax.broadcasted_iota(jnp.int32, (c, c), 1)
            diff = (i - j).astype(F32)
            lgf = _log_sigmoid(jnp.full((c, c), decf_ref[h], F32))
            lgb = _log_sigmoid(jnp.full((c, c), decb_ref[h], F32))
            dmat_ref[h] = jnp.where(i >= j, jnp.exp(lgf * jnp.maximum(diff, 0.0)),
                                    jnp.exp(lgb * jnp.maximum(-diff, 0.0))).astype(BF16)


def _ret_fwd_kernel(q_ref, k_ref, v_ref, sb_ref, r_ref,
                    state_ref, dmat_ref, xif_ref, xib_ref, zeta_ref, gc_ref):
    c = RET_CHUNK
    t_len, d = q_ref.shape[1], q_ref.shape[2]
    dk = d // RET_HEADS
    states = [state_ref[h] for h in range(RET_HEADS)]
    for ci in range(t_len // c):
        rows = slice(ci * c, (ci + 1) * c)
        scores = []
        for h in range(RET_HEADS):
            cols = slice(h * dk, (h + 1) * dk)
            scores.append(lax.dot_general(q_ref[0, rows, cols], k_ref[0, rows, cols],
                                          (((1,), (1,)), ((), ())), preferred_element_type=F32))
        for h in range(RET_HEADS):
            cols = slice(h * dk, (h + 1) * dk)
            q = q_ref[0, rows, cols]
            v = v_ref[0, rows, cols]
            lhs = jnp.concatenate([scores[h].astype(BF16) * dmat_ref[h],
                                   q * xif_ref[h], q * xib_ref[h]], axis=1)
            rhs = jnp.concatenate([v, states[h].astype(BF16), sb_ref[0, ci, h]], axis=0)
            y = _dot(lhs, rhs)
            states[h] = states[h] * gc_ref[h] + _tn_dot(k_ref[0, rows, cols], v * zeta_ref[h])
            yc = y - jnp.mean(y, axis=-1, keepdims=True)
            var = jnp.mean(yc * yc, axis=-1, keepdims=True)
            r_ref[0, rows, cols] = (yc * lax.rsqrt(var + GN_EPS)).astype(BF16)
    for h in range(RET_HEADS):
        state_ref[h] = states[h]


def _fft_a_kernel(z_ref, dft_ref, tab_ref, o_ref, zin_ref, zout_ref):
    na, rn, d = z_ref.shape[1], z_ref.shape[2], z_ref.shape[3]
    gd = dft_ref.shape[0]
    lanes = range(d // V7X_LANES)
    for a in range(na):
        for lc in lanes:
            zin_ref[lc, a * STAGE_PITCH:a * STAGE_PITCH + rn, :] = (
                z_ref[0, a, :, lc * V7X_LANES:(lc + 1) * V7X_LANES])
    u = jnp.concatenate(
        [_rows(jnp.concatenate([zin_ref[lc, pl.ds(r, na, stride=STAGE_PITCH), :] for lc in lanes],
                               axis=1)) for r in range(rn)], axis=0)
    n1 = u.shape[0] // rn
    parts = [_dot(u[:, gi * gd:(gi + 1) * gd], dft_ref[...]) for gi in range(d // gd)]
    re = jnp.concatenate([p[:, :gd] for p in parts], axis=1).astype(BF16)
    im = jnp.concatenate([p[:, gd:] for p in parts], axis=1).astype(BF16)
    for r in range(rn):
        rows = slice(r * n1, (r + 1) * n1)
        zz = jnp.concatenate([re[rows], im[rows]], axis=0)
        w = _words(_dot(tab_ref[r], zz))
        for lc in lanes:
            zout_ref[lc, pl.ds(r, n1, stride=STAGE_PITCH), :] = w[:, lc * V7X_LANES:(lc + 1) * V7X_LANES]
    for k in range(n1):
        for lc in lanes:
            o_ref[0, k, :, lc * V7X_LANES:(lc + 1) * V7X_LANES] = (
                zout_ref[lc, k * STAGE_PITCH:k * STAGE_PITCH + rn, :])


def _fft_b_kernel(zt_ref, tab_ref, o_ref):
    for i in range(zt_ref.shape[1]):
        o_ref[0, :, i, :] = _words(_dot(tab_ref[...], _rows(zt_ref[0, i])))


def _ret_fwd_fft_b_kernel(decf_ref, decb_ref, q_hbm, k_hbm, v_hbm, sb_hbm, zt_hbm, tab_ref,
                          r_hbm, f_hbm, state_ref, dmat_ref, xif_ref, xib_ref, zeta_ref, gc_ref):
    bi = pl.program_id(0)
    _ret_fwd_init(decf_ref, decb_ref, state_ref, dmat_ref, xif_ref, xib_ref, zeta_ref, gc_ref)
    c = RET_CHUNK
    s, d = q_hbm.shape[1], q_hbm.shape[2]
    dk = d // RET_HEADS
    n2, n1 = zt_hbm.shape[2], zt_hbm.shape[1]
    steps = n1 // SUBLANES
    t_b = s // steps

    def body(q_ref, k_ref, v_ref, sb_ref, zt_ref, r_ref, f_ref):
        _ret_fwd_kernel(q_ref, k_ref, v_ref, sb_ref, r_ref,
                        state_ref, dmat_ref, xif_ref, xib_ref, zeta_ref, gc_ref)
        _fft_b_kernel(zt_ref, tab_ref, f_ref)

    deep = pl.Buffered(PIPELINE_DEPTH)
    row = lambda t: (0, t, 0)
    pltpu.emit_pipeline(
        body,
        grid=(steps,),
        in_specs=[pl.BlockSpec((1, t_b, d), row, pipeline_mode=deep),
                  pl.BlockSpec((1, t_b, d), row, pipeline_mode=deep),
                  pl.BlockSpec((1, t_b, d), row, pipeline_mode=deep),
                  pl.BlockSpec((1, t_b // c, RET_HEADS, dk, dk), lambda t: (0, t, 0, 0, 0),
                               pipeline_mode=deep),
                  pl.BlockSpec((1, SUBLANES, n2, d), lambda t: (0, t, 0, 0), pipeline_mode=deep)],
        out_specs=[pl.BlockSpec((1, t_b, d), row),
                   pl.BlockSpec((1, n2 // 2, SUBLANES, d), lambda t: (0, 0, t, 0))],
    )(q_hbm.at[pl.ds(bi, 1)], k_hbm.at[pl.ds(bi, 1)], v_hbm.at[pl.ds(bi, 1)],
      sb_hbm.at[pl.ds(bi, 1)], zt_hbm.at[pl.ds(bi, 1)],
      r_hbm.at[pl.ds(bi, 1)], f_hbm.at[pl.ds(bi, 1)])


def _mixers(q, k, v, sb, z, decay_fwd, decay_bwd, dft, tab_a, tab_b):
    b, s, d = q.shape
    dk = d // RET_HEADS
    c = RET_CHUNK
    n2 = tab_b.shape[0]
    n1 = s // n2
    steps_b = n1 // SUBLANES
    t_b = s // steps_b
    assert t_b % c == 0
    smem = pl.BlockSpec(memory_space=pltpu.SMEM)
    hbm = pl.BlockSpec(memory_space=pl.ANY)
    xi = pltpu.VMEM((RET_HEADS, c, dk), BF16)
    table_bytes = RET_HEADS * (3 * c * dk + c * c + dk * dk + dk) * 4
    stage_b_bytes = 2 * (SUBLANES * n2 * d * 4 + (n2 // 2) * SUBLANES * d * 4) + 4 * n2 * d * 4
    zt = pl.pallas_call(
        _fft_a_kernel,
        out_shape=jax.ShapeDtypeStruct((b, n1, n2, d), jnp.int32),
        grid=(b, n2 // SUBLANES),
        in_specs=[pl.BlockSpec((1, n1 // 2, SUBLANES, d), lambda i, t: (i, 0, t, 0)),
                  _resident(dft.shape),
                  pl.BlockSpec((SUBLANES, 2 * n1, 2 * n1), lambda i, t: (t, 0, 0))],
        out_specs=pl.BlockSpec((1, n1, SUBLANES, d), lambda i, t: (i, 0, t, 0)),
        scratch_shapes=[pltpu.VMEM((d // V7X_LANES, n1 // 2 * STAGE_PITCH, V7X_LANES), jnp.int32),
                        pltpu.VMEM((d // V7X_LANES, n1 * STAGE_PITCH, V7X_LANES), jnp.int32)],
        compiler_params=pltpu.CompilerParams(
            dimension_semantics=("parallel", "parallel"),
            vmem_limit_bytes=_vmem_limit(3 * n1 // 2 * STAGE_PITCH * d * 4
                                         + 2 * (3 * n1 * SUBLANES * d * 2
                                              + SUBLANES * 4 * n1 * n1 * 2)
                                         + 5 * n1 * SUBLANES * d * 4)),
        name="fft_a",
    )(z.reshape(b, n1 // 2, n2, d), dft, tab_a)
    return pl.pallas_call(
        _ret_fwd_fft_b_kernel,
        out_shape=(jax.ShapeDtypeStruct((b, s, d), BF16),
                   jax.ShapeDtypeStruct((b, n2 // 2, n1, d), jnp.int32)),
        grid=(b,),
        in_specs=[smem, smem, hbm, hbm, hbm, hbm, hbm, _resident((n2, 2 * n2))],
        out_specs=(hbm, hbm),
        scratch_shapes=[pltpu.VMEM((RET_HEADS, dk, dk), F32), pltpu.VMEM((RET_HEADS, c, c), BF16),
                        xi, xi, pltpu.VMEM((RET_HEADS, c, dk), BF16),
                        pltpu.VMEM((RET_HEADS, 1, dk), F32)],
        compiler_params=pltpu.CompilerParams(
            dimension_semantics=("arbitrary",),
            vmem_limit_bytes=_vmem_limit((PIPELINE_DEPTH * 6 + 2 * 2) * t_b * d * 2 + table_bytes
                                         + stage_b_bytes)),
        name="ret_fwd_fft_b",
    )(decay_fwd, decay_bwd, q, k, v, sb, zt, tab_b)


def _gain_rows_kernel(w_ref, g_ref, o_ref):
    o_ref[...] = (w_ref[...] * g_ref[...]).astype(BF16)


def _gain_rows(w, gain):
    d_in, d_out = w.shape
    rows = min(d_in, V7X_MXU_DIM)
    return pl.pallas_call(
        _gain_rows_kernel,
        out_shape=jax.ShapeDtypeStruct((d_in, d_out), BF16),
        grid=(d_in // rows,),
        in_specs=[pl.BlockSpec((rows, d_out), lambda i: (i, 0)),
                  pl.BlockSpec((rows, 1), lambda i: (i, 0))],
        out_specs=pl.BlockSpec((rows, d_out), lambda i: (i, 0)),
        compiler_params=pltpu.CompilerParams(dimension_semantics=("parallel",)),
        name="gain_rows",
    )(w, gain.reshape(d_in, 1))


def _tail_kernel(x_ref, r_ref, g_ref, f_ref, gr_ref, gf_ref, wqk_ref, wvo_ref,
                 w_ro_ref, w_fo_ref, w_mo_ref, n_ca_ref,
                 n_mlp_ref, w_up_ref, w_dn_ref, n_fin_ref, o_ref, att_ref, perm_ref):
    d = x_ref.shape[-1]
    m = wqk_ref.shape[2] // CA_HEADS
    ff = w_up_ref.shape[1]
    gq, n1 = f_ref.shape[1], f_ref.shape[2]
    hb = x_ref.shape[1] // ROW_BLOCKS
    lanes = range(d // V7X_LANES)
    blocks = [slice(i * hb, (i + 1) * hb) for i in range(ROW_BLOCKS)]
    def gated(rows):
        g = g_ref[0, rows, :]
        return g * jax.nn.sigmoid(g) * r_ref[0, rows, :]

    ret = [_dot(gated(rows), w_ro_ref[...]) for rows in blocks]
    for i, rows in enumerate(blocks):
        g0, g1 = i * gq // ROW_BLOCKS, (i + 1) * gq // ROW_BLOCKS
        fperm = _dot(_rows(f_ref[0, g0:g1].reshape(hb // 2, d)), w_fo_ref[...])
        for lc in lanes:
            perm_ref[lc, rows, :] = fperm[:, lc * V7X_LANES:(lc + 1) * V7X_LANES]
    xs = []
    for i, rows in enumerate(blocks):
        fou = jnp.concatenate(
            [jnp.concatenate([perm_ref[lc, pl.ds(i * hb + blk // 2 * 2 * n1 + blk % 2, n1,
                                                 stride=2), :] for lc in lanes], axis=1)
             for blk in range(2 * gq // ROW_BLOCKS)], axis=0)
        merged = (jax.nn.sigmoid(gr_ref[0, rows, :].astype(F32)) * ret[i]
                  + jax.nn.sigmoid(gf_ref[0, rows, :].astype(F32)) * fou)
        xs.append(x_ref[0, rows, :] + _dot(merged.astype(BF16), w_mo_ref[...]))

    logits = [_dot(_rms(x, n_ca_ref[...]).astype(BF16), wqk_ref[0]) for x in xs]
    for lg, rows in zip(logits, blocks):
        for h in range(CA_HEADS):
            cols = slice(h * m, (h + 1) * m)
            p = jnp.exp(lg[:, cols] - jnp.max(lg[:, cols], axis=-1, keepdims=True))
            att_ref[rows, cols] = (p / jnp.sum(p, axis=-1, keepdims=True)).astype(BF16)
    xs = [x + _dot(att_ref[rows, :], wvo_ref[0]) for x, rows in zip(xs, blocks)]

    xb = [_rms(x, n_mlp_ref[...]).astype(BF16) for x in xs]
    for j in range(ff // d):
        cols = slice(j * d, (j + 1) * d)
        hid = [jnp.square(jnp.maximum(_dot(b, w_up_ref[:, cols]), 0.0)).astype(BF16) for b in xb]
        xs = [x + _dot(hd, w_dn_ref[cols, :]) for x, hd in zip(xs, hid)]
    for x, rows in zip(xs, blocks):
        o_ref[0, rows, :] = _rms(x, n_fin_ref[...])


def _tail(x, r, g, four, gate_r, gate_f, wqk, wvo, w_ro, w_fo, w_mo, n_ca,
          n_mlp, w_up, w_dn, n_fin, tm):
    b, s, d = x.shape
    hm = wqk.shape[2]
    ff = w_up.shape[1]
    row = pl.BlockSpec((1, tm, d), lambda i, j: (i, j, 0))
    n1 = four.shape[2]
    assert tm % (2 * n1 * ROW_BLOCKS) == 0
    half = pl.BlockSpec((1, tm // (2 * n1), n1, d), lambda i, j: (i, j, 0, 0))
    vec = _resident((1, d))
    sq = _resident((d, d))
    vmem = ((3 * d * d + 2 * d * ff) * 2 + 4 * d * hm * 2 + 2 * tm * d * (4 + 4 + 5 * 2)
            + tm * d * 4 * 8)
    return pl.pallas_call(
        _tail_kernel,
        out_shape=jax.ShapeDtypeStruct((b, s, d), F32),
        grid=(b, s // tm),
        in_specs=[row, row, row, half, row, row,
                  pl.BlockSpec((1, d, hm), lambda i, j: (i, 0, 0)),
                  pl.BlockSpec((1, hm, d), lambda i, j: (i, 0, 0)),
                  sq, sq, sq, vec, vec, _resident((d, ff)), _resident((ff, d)), vec],
        out_specs=row,
        scratch_shapes=[pltpu.VMEM((tm, hm), BF16),
                        pltpu.VMEM((d // V7X_LANES, tm, V7X_LANES), F32)],
        compiler_params=pltpu.CompilerParams(
            dimension_semantics=("parallel", "parallel"),
            vmem_limit_bytes=_vmem_limit(vmem)),
        name="tail",
    )(x, r, g, four, gate_r, gate_f, wqk, wvo, w_ro, w_fo, w_mo, n_ca, n_mlp, w_up, w_dn, n_fin)


def _tiles(s):
    return min(512, s), min(512, s)


def _trunk(x, mem, p):
    b, s, d = x.shape
    dk = d // RET_HEADS
    gd = d // FOUR_GROUPS
    n1 = s // FFT_N2
    assert s % FFT_N2 == 0 and s % RET_CHUNK == 0
    tm_in, tm_tail = _tiles(s)
    cos, sin = (jnp.asarray(t) for t in _rope_tables(s, dk))
    dft = jnp.asarray(_chan_dft(gd)).astype(BF16)
    tab_a = jnp.asarray(_fft_a_table(n1, FFT_N2)).astype(BF16)
    tab_b = jnp.asarray(_fft_b_table(FFT_N2, float(s * gd) ** -0.5)).astype(BF16)

    for l in range(p["w_in"].shape[0]):
        wqk, wvo = _mem_kv(mem, p["norm_mem_w"][l], p["w_ck"][l], p["w_cv"][l],
                           p["w_cq"][l], p["w_co"][l])
        q, k, v, g, z, gate_r, gate_f, sb = _in_proj(x, p["norm_mix_w"][l], p["w_in"][l], cos, sin,
                                                     p["ret_decay_bwd"][l], tm_in)
        r, four = _mixers(q, k, v, sb, z, p["ret_decay_fwd"][l], p["ret_decay_bwd"][l],
                          dft, tab_a, tab_b)
        w_ro = _gain_rows(p["w_ret_out"][l], p["ret_gn_w"][l])
        x = _tail(x, r, g, four, gate_r, gate_f, wqk, wvo, w_ro, p["w_four_out"][l],
                  p["w_mix_out"][l], p["norm_ca_w"][l], p["norm_mlp_w"][l], p["w_up"][l],
                  p["w_down"][l], p["norm_final_w"], tm_tail)
    return x


def kernel(x_prompt, x_sample, mem_prompt, mem_sample, norm_mix_w, w_in, ret_decay_fwd,
           ret_decay_bwd, ret_gn_w, w_ret_out, w_four_out, w_mix_out, norm_ca_w, norm_mem_w,
           w_cq, w_ck, w_cv, w_co, norm_mlp_w, w_up, w_down, norm_final_w):
    depth, d = norm_mix_w.shape
    assert depth == 1, "the tail fuses the final norm into the (single) layer"
    vec = lambda w: w.reshape(depth, 1, d)
    mat = lambda w: w.astype(BF16)
    p = dict(
        norm_mix_w=vec(norm_mix_w), w_in=mat(w_in), ret_decay_fwd=ret_decay_fwd,
        ret_decay_bwd=ret_decay_bwd, ret_gn_w=ret_gn_w, w_ret_out=w_ret_out,
        w_four_out=mat(w_four_out), w_mix_out=mat(w_mix_out), norm_ca_w=vec(norm_ca_w),
        norm_mem_w=vec(norm_mem_w), w_cq=mat(w_cq), w_ck=mat(w_ck), w_cv=mat(w_cv),
        w_co=mat(w_co), norm_mlp_w=vec(norm_mlp_w), w_up=mat(w_up), w_down=mat(w_down),
        norm_final_w=norm_final_w.reshape(1, d))
    return _trunk(x_prompt, mem_prompt, p), _trunk(x_sample, mem_sample, p)
```

```python
import functools

import numpy as np
import jax
import jax.numpy as jnp
from jax import lax
from jax.experimental import pallas as pl
from jax.experimental.pallas import tpu as pltpu

RET_HEADS = 4
FOUR_GROUPS = 4
CA_HEADS = 4
ROPE_THETA = 10000.0
EPS = 1e-6
GN_EPS = 1e-5

V7X_VMEM_BYTES = 64 * 1024 * 1024
V7X_MXU_DIM = 256
V7X_LANES = 128
SUBLANES = 8
STAGE_PITCH = 12

RET_CHUNK = V7X_MXU_DIM
FFT_N2 = 128
ROW_BLOCKS = 2
FFT_A_TILES = 2

BF16 = jnp.bfloat16
F32 = jnp.float32


COMPILER_SCRATCH_BYTES = 8 * 1024 * 1024
VMEM_RESERVE_BYTES = 6 * 1024 * 1024


def _vmem_limit(nbytes):
    return int(min(nbytes + COMPILER_SCRATCH_BYTES, V7X_VMEM_BYTES - VMEM_RESERVE_BYTES))


def _rms(x, w):
    return x * lax.rsqrt(jnp.mean(x * x, axis=-1, keepdims=True) + EPS) * w


def _dot(a, b):
    return jnp.dot(a, b, preferred_element_type=F32)


def _words(x):
    return pltpu.bitcast(x.astype(BF16), jnp.int32)


def _rows(w):
    return pltpu.bitcast(w, BF16)


def _resident(shape):
    return pl.BlockSpec(shape, lambda *_: (0,) * len(shape), pipeline_mode=pl.Buffered(1))


@functools.lru_cache(maxsize=None)
def _rope_tables(s, dk):
    inv = ROPE_THETA ** (-np.arange(0, dk, 2, dtype=np.float64) / dk)
    ang = np.arange(s, dtype=np.float64)[:, None] * inv[None, :]
    return np.cos(ang).astype(np.float32), np.sin(ang).astype(np.float32)


@functools.lru_cache(maxsize=None)
def _chan_dft(n):
    ang = 2.0 * np.pi * np.outer(np.arange(n), np.arange(n)) / n
    return np.concatenate([np.cos(ang), -np.sin(ang)], axis=1).astype(np.float32)


@functools.lru_cache(maxsize=None)
def _fft_a_table(n1, n2):
    n = n1 * n2
    k1 = np.arange(n1, dtype=np.float64)[None, :, None]
    tok = (n2 * np.arange(n1, dtype=np.float64)[None, None, :]
           + np.arange(n2, dtype=np.float64)[:, None, None])
    ang = 2.0 * np.pi * np.mod(k1 * tok, n) / n
    c, s = np.cos(ang), np.sin(ang)
    re_rows = np.concatenate([c, s], axis=2)
    im_rows = np.concatenate([-s, c], axis=2)
    return np.stack([re_rows, im_rows], axis=2).reshape(n2, 2 * n1, 2 * n1).astype(np.float32)


@functools.lru_cache(maxsize=None)
def _fft_b_table(n2, scale):
    ang = 2.0 * np.pi * np.outer(np.arange(n2), np.arange(n2)) / n2
    return (scale * np.stack([np.cos(ang), np.sin(ang)], axis=-1).reshape(n2, 2 * n2)).astype(np.float32)


def _mem_kv_kernel(mem_ref, nw_ref, wk_ref, wv_ref, wq_ref, wo_ref, wqk_ref, wvo_ref):
    m, d = mem_ref.shape[1], mem_ref.shape[2]
    dh = d // CA_HEADS
    mn = _rms(mem_ref[0], nw_ref[...]).astype(BF16)
    ck = _dot(mn, wk_ref[...]).astype(BF16)
    cv = _dot(mn, wv_ref[...]).astype(BF16)
    for h in range(CA_HEADS):
        cols = slice(h * dh, (h + 1) * dh)
        qk = lax.dot_general(wq_ref[:, cols], ck[:, cols], (((1,), (1,)), ((), ())),
                             preferred_element_type=F32) * (dh ** -0.5)
        wqk_ref[0, :, h * m:(h + 1) * m] = qk.astype(BF16)
        wvo_ref[0, h * m:(h + 1) * m, :] = _dot(cv[:, cols], wo_ref[cols, :]).astype(BF16)


def _mem_kv(mem, norm_w, w_ck, w_cv, w_cq, w_co):
    b, m, d = mem.shape
    hm = CA_HEADS * m
    sq = _resident((d, d))
    return pl.pallas_call(
        _mem_kv_kernel,
        out_shape=(jax.ShapeDtypeStruct((b, d, hm), BF16), jax.ShapeDtypeStruct((b, hm, d), BF16)),
        grid=(b,),
        in_specs=[pl.BlockSpec((1, m, d), lambda i: (i, 0, 0)), _resident((1, d)), sq, sq, sq, sq],
        out_specs=(pl.BlockSpec((1, d, hm), lambda i: (i, 0, 0)),
                   pl.BlockSpec((1, hm, d), lambda i: (i, 0, 0))),
        compiler_params=pltpu.CompilerParams(
            dimension_semantics=("arbitrary",),
            vmem_limit_bytes=_vmem_limit(4 * d * d * 2 + 4 * m * d * 4 + 4 * d * hm * 2
                                         + 2 * d * hm * 4)),
        name="mem_kv",
    )(mem, norm_w, w_ck, w_cv, w_cq, w_co)


def _log_sigmoid(x):
    return jnp.minimum(x, 0.0) - jnp.log1p(jnp.exp(-jnp.abs(x)))


def _tn_dot(a, b):
    return lax.dot_general(a, b, (((0,), (0,)), ((), ())), preferred_element_type=F32)


def _decay_rows(dec, c, dk, offset, sign):
    lg = _log_sigmoid(jnp.full((c, dk), dec, F32))
    row = lax.broadcasted_iota(jnp.int32, (c, dk), 0).astype(F32)
    return jnp.exp(lg * (offset + sign * row))


def _chunk_decay(dec, c, dk):
    return jnp.exp(_log_sigmoid(jnp.full((1, dk), dec, F32)) * c)


def _ret_bwd_init(dec_ref, state_ref, zeta_ref, gc_ref):
    c, dk = zeta_ref.shape[1], zeta_ref.shape[2]

    @pl.when(pl.program_id(1) == 0)
    def _init():
        state_ref[...] = jnp.zeros_like(state_ref)
        for h in range(RET_HEADS):
            zeta_ref[h] = _decay_rows(dec_ref[h], c, dk, 0.0, 1.0).astype(BF16)
            gc_ref[h] = _chunk_decay(dec_ref[h], c, dk)


def _ret_bwd_states(k_ref, v_ref, sb_ref, state_ref, zeta_ref, gc_ref):
    c = RET_CHUNK
    t_len, d = k_ref.shape[1], k_ref.shape[2]
    dk = d // RET_HEADS
    for h in range(RET_HEADS):
        cols = slice(h * dk, (h + 1) * dk)
        state = state_ref[h]
        for ci in reversed(range(t_len // c)):
            rows = slice(ci * c, (ci + 1) * c)
            sb_ref[0, ci, h] = state.astype(BF16)
            state = state * gc_ref[h] + _tn_dot(k_ref[0, rows, cols],
                                                v_ref[0, rows, cols] * zeta_ref[h])
        state_ref[h] = state


def _in_proj_kernel(dec_ref, x_ref, nw_ref, w_ref, cos_ref, sin_ref,
                    q_ref, k_ref, v_ref, g_ref, z_ref, gr_ref, gf_ref, sb_ref,
                    perm_ref, state_ref, zeta_ref, gc_ref):
    _ret_bwd_init(dec_ref, state_ref, zeta_ref, gc_ref)
    d = x_ref.shape[-1]
    dk = d // RET_HEADS
    half = dk // 2
    tm = x_ref.shape[1]
    hb = tm // ROW_BLOCKS
    blocks = [slice(i * hb, (i + 1) * hb) for i in range(ROW_BLOCKS)]
    xb = [_rms(x_ref[0, rows, :], nw_ref[...]).astype(BF16) for rows in blocks]

    def proj(j):
        return [_dot(b, w_ref[:, j * d:(j + 1) * d]) for b in xb]

    def rope(ps, out_ref, scale):
        for p, rows in zip(ps, blocks):
            cos = cos_ref[rows, :]
            sin = sin_ref[rows, :]
            for h in range(RET_HEADS):
                x1 = p[:, h * dk:h * dk + half]
                x2 = p[:, h * dk + half:(h + 1) * dk]
                o1 = x1 * cos - x2 * sin
                o2 = x2 * cos + x1 * sin
                if scale is not None:
                    o1 = o1 * scale
                    o2 = o2 * scale
                out_ref[0, rows, h * dk:h * dk + half] = o1.astype(BF16)
                out_ref[0, rows, h * dk + half:(h + 1) * dk] = o2.astype(BF16)

    def store(ps, out_ref):
        for p, rows in zip(ps, blocks):
            out_ref[0, rows, :] = p.astype(BF16)

    rope(proj(0), q_ref, dk ** -0.5)
    rope(proj(1), k_ref, None)
    store(proj(2), v_ref)
    _ret_bwd_states(k_ref, v_ref, sb_ref, state_ref, zeta_ref, gc_ref)
    store(proj(3), g_ref)
    us = proj(4)
    for lc in range(d // V7X_LANES):
        for blk in range(tm // FFT_N2):
            a, par = divmod(blk, 2)
            u = us[blk * FFT_N2 // hb]
            r0 = blk * FFT_N2 % hb
            perm_ref[lc, pl.ds(2 * FFT_N2 * a + par, FFT_N2, stride=2), :] = (
                u[r0:r0 + FFT_N2, lc * V7X_LANES:(lc + 1) * V7X_LANES])
    for lc in range(d // V7X_LANES):
        z_ref[0, :, lc * V7X_LANES:(lc + 1) * V7X_LANES] = _words(perm_ref[lc])
    store(proj(5), gr_ref)
    store(proj(6), gf_ref)


def _in_proj(x, norm_w, w_in, cos, sin, decay_bwd, tm):
    b, s, d = x.shape
    dk = d // RET_HEADS
    half = dk // 2
    c = RET_CHUNK
    nt = s // tm
    act = jax.ShapeDtypeStruct((b, s, d), BF16)
    assert tm % (2 * FFT_N2) == 0 and tm % c == 0 and (tm // ROW_BLOCKS) % FFT_N2 == 0
    zsh = jax.ShapeDtypeStruct((b, s // 2, d), jnp.int32)
    sbsh = jax.ShapeDtypeStruct((b, s // c, RET_HEADS, dk, dk), BF16)
    row = pl.BlockSpec((1, tm, d), lambda i, j: (i, nt - 1 - j, 0))
    zrow = pl.BlockSpec((1, tm // 2, d), lambda i, j: (i, nt - 1 - j, 0))
    sbrow = pl.BlockSpec((1, tm // c, RET_HEADS, dk, dk), lambda i, j: (i, nt - 1 - j, 0, 0, 0))
    tab = pl.BlockSpec((tm, half), lambda i, j: (nt - 1 - j, 0))
    vmem = (w_in.size * 2 + 2 * tm * d * 4 + 2 * 8 * tm * d * 2 + 4 * tm * half * 4
            + 5 * tm * d * 4 + 2 * tm * d * 2 + RET_HEADS * (dk * dk * 4 + c * dk * 2))
    return pl.pallas_call(
        _in_proj_kernel,
        out_shape=(act, act, act, act, zsh, act, act, sbsh),
        grid=(b, nt),
        in_specs=[pl.BlockSpec(memory_space=pltpu.SMEM), row, _resident((1, d)),
                  _resident(w_in.shape), tab, tab],
        out_specs=(row, row, row, row, zrow, row, row, sbrow),
        scratch_shapes=[pltpu.VMEM((d // V7X_LANES, tm, V7X_LANES), F32),
                        pltpu.VMEM((RET_HEADS, dk, dk), F32),
                        pltpu.VMEM((RET_HEADS, c, dk), BF16),
                        pltpu.VMEM((RET_HEADS, 1, dk), F32)],
        compiler_params=pltpu.CompilerParams(
            dimension_semantics=("arbitrary", "arbitrary"),
            vmem_limit_bytes=_vmem_limit(vmem)),
        name="in_proj",
    )(decay_bwd, x, norm_w, w_in, cos, sin)


def _ret_fwd_kernel(decf_ref, decb_ref, q_ref, k_ref, v_ref, sb_ref, r_ref,
                    state_ref, dmat_ref, xif_ref, xib_ref, zeta_ref, gc_ref):
    c = RET_CHUNK
    t_len, d = q_ref.shape[1], q_ref.shape[2]
    dk = d // RET_HEADS

    @pl.when(pl.program_id(1) == 0)
    def _init():
        state_ref[...] = jnp.zeros_like(state_ref)
        for h in range(RET_HEADS):
            xif_ref[h] = _decay_rows(decf_ref[h], c, dk, 1.0, 1.0).astype(BF16)
            xib_ref[h] = _decay_rows(decb_ref[h], c, dk, float(c), -1.0).astype(BF16)
            zeta_ref[h] = _decay_rows(decf_ref[h], c, dk, c - 1.0, -1.0).astype(BF16)
            gc_ref[h] = _chunk_decay(decf_ref[h], c, dk)
            i = lax.broadcasted_iota(jnp.int32, (c, c), 0)
            j = lax.broadcasted_iota(jnp.int32, (c, c), 1)
            diff = (i - j).astype(F32)
            lgf = _log_sigmoid(jnp.full((c, c), decf_ref[h], F32))
            lgb = _log_sigmoid(jnp.full((c, c), decb_ref[h], F32))
            dmat_ref[h] = jnp.where(i >= j, jnp.exp(lgf * jnp.maximum(diff, 0.0)),
                                    jnp.exp(lgb * jnp.maximum(-diff, 0.0))).astype(BF16)

    states = [state_ref[h] for h in range(RET_HEADS)]
    for ci in range(t_len // c):
        rows = slice(ci * c, (ci + 1) * c)
        scores = []
        for h in range(RET_HEADS):
            cols = slice(h * dk, (h + 1) * dk)
            scores.append(lax.dot_general(q_ref[0, rows, cols], k_ref[0, rows, cols],
                                          (((1,), (1,)), ((), ())), preferred_element_type=F32))
        for h in range(RET_HEADS):
            cols = slice(h * dk, (h + 1) * dk)
            q = q_ref[0, rows, cols]
            v = v_ref[0, rows, cols]
            lhs = jnp.concatenate([scores[h].astype(BF16) * dmat_ref[h],
                                   q * xif_ref[h], q * xib_ref[h]], axis=1)
            rhs = jnp.concatenate([v, states[h].astype(BF16), sb_ref[0, ci, h]], axis=0)
            y = _dot(lhs, rhs)
            states[h] = states[h] * gc_ref[h] + _tn_dot(k_ref[0, rows, cols], v * zeta_ref[h])
            yc = y - jnp.mean(y, axis=-1, keepdims=True)
            var = jnp.mean(yc * yc, axis=-1, keepdims=True)
            r_ref[0, rows, cols] = (yc * lax.rsqrt(var + GN_EPS)).astype(BF16)
    for h in range(RET_HEADS):
        state_ref[h] = states[h]


def _fft_a_kernel(z_ref, dft_ref, tab_ref, o_ref, zin_ref, zout_ref):
    for r0 in range(0, z_ref.shape[2], SUBLANES):
        _fft_a_columns(z_ref, dft_ref, tab_ref, o_ref, zin_ref, zout_ref, r0)


def _fft_a_columns(z_ref, dft_ref, tab_ref, o_ref, zin_ref, zout_ref, r0):
    na, rn, d = z_ref.shape[1], SUBLANES, z_ref.shape[3]
    gd = dft_ref.shape[0]
    lanes = range(d // V7X_LANES)
    for a in range(na):
        for lc in lanes:
            zin_ref[lc, a * STAGE_PITCH:a * STAGE_PITCH + rn, :] = (
                z_ref[0, a, r0:r0 + rn, lc * V7X_LANES:(lc + 1) * V7X_LANES])
    u = jnp.concatenate(
        [_rows(jnp.concatenate([zin_ref[lc, pl.ds(r, na, stride=STAGE_PITCH), :] for lc in lanes],
                               axis=1)) for r in range(rn)], axis=0)
    n1 = u.shape[0] // rn
    parts = [_dot(u[:, gi * gd:(gi + 1) * gd], dft_ref[...]) for gi in range(d // gd)]
    re = jnp.concatenate([p[:, :gd] for p in parts], axis=1).astype(BF16)
    im = jnp.concatenate([p[:, gd:] for p in parts], axis=1).astype(BF16)
    for r in range(rn):
        rows = slice(r * n1, (r + 1) * n1)
        zz = jnp.concatenate([re[rows], im[rows]], axis=0)
        w = _words(_dot(tab_ref[r0 + r], zz))
        for lc in lanes:
            zout_ref[lc, pl.ds(r, n1, stride=STAGE_PITCH), :] = w[:, lc * V7X_LANES:(lc + 1) * V7X_LANES]
    for k in range(n1):
        for lc in lanes:
            o_ref[0, k, r0:r0 + rn, lc * V7X_LANES:(lc + 1) * V7X_LANES] = (
                zout_ref[lc, k * STAGE_PITCH:k * STAGE_PITCH + rn, :])


def _fft_b_kernel(zt_ref, tab_ref, o_ref):
    for i in range(zt_ref.shape[1]):
        o_ref[0, :, i, :] = _words(_dot(tab_ref[...], _rows(zt_ref[0, i])))


def _ret_fwd_fft_b_kernel(decf_ref, decb_ref, q_ref, k_ref, v_ref, sb_ref,
                          zt_ref, tab_ref, r_ref, f_ref,
                          state_ref, dmat_ref, xif_ref, xib_ref, zeta_ref, gc_ref):
    _ret_fwd_kernel(decf_ref, decb_ref, q_ref, k_ref, v_ref, sb_ref, r_ref,
                    state_ref, dmat_ref, xif_ref, xib_ref, zeta_ref, gc_ref)
    _fft_b_kernel(zt_ref, tab_ref, f_ref)


def _mixers(q, k, v, sb, z, decay_fwd, decay_bwd, dft, tab_a, tab_b):
    b, s, d = q.shape
    dk = d // RET_HEADS
    c = RET_CHUNK
    n2 = tab_b.shape[0]
    n1 = s // n2
    steps_b = n1 // SUBLANES
    t_b = s // steps_b
    assert t_b % c == 0
    smem = pl.BlockSpec(memory_space=pltpu.SMEM)
    fwd = pl.BlockSpec((1, t_b, d), lambda i, t: (i, t, 0))
    cols_a = FFT_A_TILES * SUBLANES
    assert n2 % cols_a == 0
    xi = pltpu.VMEM((RET_HEADS, c, dk), BF16)
    table_bytes = RET_HEADS * (3 * c * dk + c * c + dk * dk + dk) * 4
    stage_b_bytes = 2 * (SUBLANES * n2 * d * 4 + (n2 // 2) * SUBLANES * d * 4) + 4 * n2 * d * 4
    zt = pl.pallas_call(
        _fft_a_kernel,
        out_shape=jax.ShapeDtypeStruct((b, n1, n2, d), jnp.int32),
        grid=(b, n2 // cols_a),
        in_specs=[pl.BlockSpec((1, n1 // 2, cols_a, d), lambda i, t: (i, 0, t, 0)),
                  _resident(dft.shape),
                  pl.BlockSpec((cols_a, 2 * n1, 2 * n1), lambda i, t: (t, 0, 0))],
        out_specs=pl.BlockSpec((1, n1, cols_a, d), lambda i, t: (i, 0, t, 0)),
        scratch_shapes=[pltpu.VMEM((d // V7X_LANES, n1 // 2 * STAGE_PITCH, V7X_LANES), jnp.int32),
                        pltpu.VMEM((d // V7X_LANES, n1 * STAGE_PITCH, V7X_LANES), jnp.int32)],
        compiler_params=pltpu.CompilerParams(
            dimension_semantics=("parallel", "parallel"),
            vmem_limit_bytes=_vmem_limit(3 * n1 // 2 * STAGE_PITCH * d * 4
                                         + 2 * (3 * n1 * cols_a * d * 2
                                              + cols_a * 4 * n1 * n1 * 2)
                                         + 5 * n1 * SUBLANES * d * 4)),
        name="fft_a",
    )(z.reshape(b, n1 // 2, n2, d), dft, tab_a)
    return pl.pallas_call(
        _ret_fwd_fft_b_kernel,
        out_shape=(jax.ShapeDtypeStruct((b, s, d), BF16),
                   jax.ShapeDtypeStruct((b, n2 // 2, n1, d), jnp.int32)),
        grid=(b, steps_b),
        in_specs=[smem, smem, fwd, fwd, fwd,
                  pl.BlockSpec((1, t_b // c, RET_HEADS, dk, dk), lambda i, t: (i, t, 0, 0, 0)),
                  pl.BlockSpec((1, SUBLANES, n2, d), lambda i, t: (i, t, 0, 0)),
                  _resident((n2, 2 * n2))],
        out_specs=(fwd, pl.BlockSpec((1, n2 // 2, SUBLANES, d), lambda i, t: (i, 0, t, 0))),
        scratch_shapes=[pltpu.VMEM((RET_HEADS, dk, dk), F32), pltpu.VMEM((RET_HEADS, c, c), BF16),
                        xi, xi, pltpu.VMEM((RET_HEADS, c, dk), BF16),
                        pltpu.VMEM((RET_HEADS, 1, dk), F32)],
        compiler_params=pltpu.CompilerParams(
            dimension_semantics=("arbitrary", "arbitrary"),
            vmem_limit_bytes=_vmem_limit(2 * 5 * t_b * d * 2 + table_bytes + stage_b_bytes)),
        name="ret_fwd_fft_b",
    )(decay_fwd, decay_bwd, q, k, v, sb, zt, tab_b)


def _gain_rows_kernel(w_ref, g_ref, o_ref):
    o_ref[...] = (w_ref[...] * g_ref[...]).astype(BF16)


def _gain_rows(w, gain):
    d_in, d_out = w.shape
    rows = min(d_in, V7X_MXU_DIM)
    return pl.pallas_call(
        _gain_rows_kernel,
        out_shape=jax.ShapeDtypeStruct((d_in, d_out), BF16),
        grid=(d_in // rows,),
        in_specs=[pl.BlockSpec((rows, d_out), lambda i: (i, 0)),
                  pl.BlockSpec((rows, 1), lambda i: (i, 0))],
        out_specs=pl.BlockSpec((rows, d_out), lambda i: (i, 0)),
        compiler_params=pltpu.CompilerParams(dimension_semantics=("parallel",)),
        name="gain_rows",
    )(w, gain.reshape(d_in, 1))


def _tail_kernel(x_ref, r_ref, g_ref, f_ref, gr_ref, gf_ref, wqk_ref, wvo_ref,
                 w_ro_ref, w_fo_ref, w_mo_ref, n_ca_ref,
                 n_mlp_ref, w_up_ref, w_dn_ref, n_fin_ref, o_ref, att_ref, perm_ref):
    d = x_ref.shape[-1]
    m = wqk_ref.shape[2] // CA_HEADS
    ff = w_up_ref.shape[1]
    gq, n1 = f_ref.shape[1], f_ref.shape[2]
    hb = x_ref.shape[1] // ROW_BLOCKS
    lanes = range(d // V7X_LANES)
    blocks = [slice(i * hb, (i + 1) * hb) for i in range(ROW_BLOCKS)]
    def gated(rows):
        g = g_ref[0, rows, :]
        return g * jax.nn.sigmoid(g) * r_ref[0, rows, :]

    ret = [_dot(gated(rows), w_ro_ref[...]) for rows in blocks]
    for i, rows in enumerate(blocks):
        g0, g1 = i * gq // ROW_BLOCKS, (i + 1) * gq // ROW_BLOCKS
        fperm = _dot(_rows(f_ref[0, g0:g1].reshape(hb // 2, d)), w_fo_ref[...])
        for lc in lanes:
            perm_ref[lc, rows, :] = fperm[:, lc * V7X_LANES:(lc + 1) * V7X_LANES]
    xs = []
    for i, rows in enumerate(blocks):
        fou = jnp.concatenate(
            [jnp.concatenate([perm_ref[lc, pl.ds(i * hb + blk // 2 * 2 * n1 + blk % 2, n1,
                                                 stride=2), :] for lc in lanes], axis=1)
             for blk in range(2 * gq // ROW_BLOCKS)], axis=0)
        merged = (jax.nn.sigmoid(gr_ref[0, rows, :].astype(F32)) * ret[i]
                  + jax.nn.sigmoid(gf_ref[0, rows, :].astype(F32)) * fou)
        xs.append(x_ref[0, rows, :] + _dot(merged.astype(BF16), w_mo_ref[...]))

    logits = [_dot(_rms(x, n_ca_ref[...]).astype(BF16), wqk_ref[0]) for x in xs]
    for lg, rows in zip(logits, blocks):
        for h in range(CA_HEADS):
            cols = slice(h * m, (h + 1) * m)
            p = jnp.exp(lg[:, cols] - jnp.max(lg[:, cols], axis=-1, keepdims=True))
            att_ref[rows, cols] = (p / jnp.sum(p, axis=-1, keepdims=True)).astype(BF16)
    xs = [x + _dot(att_ref[rows, :], wvo_ref[0]) for x, rows in zip(xs, blocks)]

    xb = [_rms(x, n_mlp_ref[...]).astype(BF16) for x in xs]
    for j in range(ff // d):
        cols = slice(j * d, (j + 1) * d)
        hid = [jnp.square(jnp.maximum(_dot(b, w_up_ref[:, cols]), 0.0)).astype(BF16) for b in xb]
        xs = [x + _dot(hd, w_dn_ref[cols, :]) for x, hd in zip(xs, hid)]
    for x, rows in zip(xs, blocks):
        o_ref[0, rows, :] = _rms(x, n_fin_ref[...])


def _tail(x, r, g, four, gate_r, gate_f, wqk, wvo, w_ro, w_fo, w_mo, n_ca,
          n_mlp, w_up, w_dn, n_fin, tm):
    b, s, d = x.shape
    hm = wqk.shape[2]
    ff = w_up.shape[1]
    row = pl.BlockSpec((1, tm, d), lambda i, j: (i, j, 0))
    n1 = four.shape[2]
    assert tm % (2 * n1 * ROW_BLOCKS) == 0
    half = pl.BlockSpec((1, tm // (2 * n1), n1, d), lambda i, j: (i, j, 0, 0))
    vec = _resident((1, d))
    sq = _resident((d, d))
    vmem = ((3 * d * d + 2 * d * ff) * 2 + 4 * d * hm * 2 + 2 * tm * d * (4 + 4 + 5 * 2)
            + tm * d * 4 * 8)
    return pl.pallas_call(
        _tail_kernel,
        out_shape=jax.ShapeDtypeStruct((b, s, d), F32),
        grid=(b, s // tm),
        in_specs=[row, row, row, half, row, row,
                  pl.BlockSpec((1, d, hm), lambda i, j: (i, 0, 0)),
                  pl.BlockSpec((1, hm, d), lambda i, j: (i, 0, 0)),
                  sq, sq, sq, vec, vec, _resident((d, ff)), _resident((ff, d)), vec],
        out_specs=row,
        scratch_shapes=[pltpu.VMEM((tm, hm), BF16),
                        pltpu.VMEM((d // V7X_LANES, tm, V7X_LANES), F32)],
        compiler_params=pltpu.CompilerParams(
            dimension_semantics=("parallel", "parallel"),
            vmem_limit_bytes=_vmem_limit(vmem)),
        name="tail",
    )(x, r, g, four, gate_r, gate_f, wqk, wvo, w_ro, w_fo, w_mo, n_ca, n_mlp, w_up, w_dn, n_fin)


def _tiles(s):
    return min(512, s), min(512, s)


def _trunk(x, mem, p):
    b, s, d = x.shape
    dk = d // RET_HEADS
    gd = d // FOUR_GROUPS
    n1 = s // FFT_N2
    assert s % FFT_N2 == 0 and s % RET_CHUNK == 0
    tm_in, tm_tail = _tiles(s)
    cos, sin = (jnp.asarray(t) for t in _rope_tables(s, dk))
    dft = jnp.asarray(_chan_dft(gd)).astype(BF16)
    tab_a = jnp.asarray(_fft_a_table(n1, FFT_N2)).astype(BF16)
    tab_b = jnp.asarray(_fft_b_table(FFT_N2, float(s * gd) ** -0.5)).astype(BF16)

    for l in range(p["w_in"].shape[0]):
        wqk, wvo = _mem_kv(mem, p["norm_mem_w"][l], p["w_ck"][l], p["w_cv"][l],
                           p["w_cq"][l], p["w_co"][l])
        q, k, v, g, z, gate_r, gate_f, sb = _in_proj(x, p["norm_mix_w"][l], p["w_in"][l], cos, sin,
                                                     p["ret_decay_bwd"][l], tm_in)
        r, four = _mixers(q, k, v, sb, z, p["ret_decay_fwd"][l], p["ret_decay_bwd"][l],
                          dft, tab_a, tab_b)
        w_ro = _gain_rows(p["w_ret_out"][l], p["ret_gn_w"][l])
        x = _tail(x, r, g, four, gate_r, gate_f, wqk, wvo, w_ro, p["w_four_out"][l],
                  p["w_mix_out"][l], p["norm_ca_w"][l], p["norm_mlp_w"][l], p["w_up"][l],
                  p["w_down"][l], p["norm_final_w"], tm_tail)
    return x


def kernel(x_prompt, x_sample, mem_prompt, mem_sample, norm_mix_w, w_in, ret_decay_fwd,
           ret_decay_bwd, ret_gn_w, w_ret_out, w_four_out, w_mix_out, norm_ca_w, norm_mem_w,
           w_cq, w_ck, w_cv, w_co, norm_mlp_w, w_up, w_down, norm_final_w):
    depth, d = norm_mix_w.shape
    assert depth == 1, "the tail fuses the final norm into the (single) layer"
    vec = lambda w: w.reshape(depth, 1, d)
    mat = lambda w: w.astype(BF16)
    p = dict(
        norm_mix_w=vec(norm_mix_w), w_in=mat(w_in), ret_decay_fwd=ret_decay_fwd,
        ret_decay_bwd=ret_decay_bwd, ret_gn_w=ret_gn_w, w_ret_out=w_ret_out,
        w_four_out=mat(w_four_out), w_mix_out=mat(w_mix_out), norm_ca_w=vec(norm_ca_w),
        norm_mem_w=vec(norm_mem_w), w_cq=mat(w_cq), w_ck=mat(w_ck), w_cv=mat(w_cv),
        w_co=mat(w_co), norm_mlp_w=vec(norm_mlp_w), w_up=mat(w_up), w_down=mat(w_down),
        norm_final_w=norm_final_w.reshape(1, d))
    return _trunk(x_prompt, mem_prompt, p), _trunk(x_sample, mem_sample, p)
```

```python
import functools

import numpy as np
import jax
import jax.numpy as jnp
from jax import lax
from jax.experimental import pallas as pl
from jax.experimental.pallas import tpu as pltpu

RET_HEADS = 4
FOUR_GROUPS = 4
CA_HEADS = 4
ROPE_THETA = 10000.0
EPS = 1e-6
GN_EPS = 1e-5

V7X_VMEM_BYTES = 64 * 1024 * 1024
V7X_MXU_DIM = 256
V7X_LANES = 128
SUBLANES = 8
STAGE_PITCH = 12

RET_CHUNK = V7X_MXU_DIM
FFT_N2 = 128
ROW_BLOCKS = 2
FFT_A_TILES = 2

BF16 = jnp.bfloat16
F32 = jnp.float32


COMPILER_SCRATCH_BYTES = 8 * 1024 * 1024
VMEM_RESERVE_BYTES = 6 * 1024 * 1024


def _vmem_limit(nbytes):
    return int(min(nbytes + COMPILER_SCRATCH_BYTES, V7X_VMEM_BYTES - VMEM_RESERVE_BYTES))


def _rms(x, w):
    return x * lax.rsqrt(jnp.mean(x * x, axis=-1, keepdims=True) + EPS) * w


def _dot(a, b):
    return jnp.dot(a, b, preferred_element_type=F32)


def _words(x):
    return pltpu.bitcast(x.astype(BF16), jnp.int32)


def _rows(w):
    return pltpu.bitcast(w, BF16)


def _resident(shape):
    return pl.BlockSpec(shape, lambda *_: (0,) * len(shape), pipeline_mode=pl.Buffered(1))


@functools.lru_cache(maxsize=None)
def _rope_tables(s, dk):
    inv = ROPE_THETA ** (-np.arange(0, dk, 2, dtype=np.float64) / dk)
    ang = np.arange(s, dtype=np.float64)[:, None] * inv[None, :]
    return np.cos(ang).astype(np.float32), np.sin(ang).astype(np.float32)


@functools.lru_cache(maxsize=None)
def _chan_dft(n):
    ang = 2.0 * np.pi * np.outer(np.arange(n), np.arange(n)) / n
    return np.concatenate([np.cos(ang), -np.sin(ang)], axis=1).astype(np.float32)


@functools.lru_cache(maxsize=None)
def _fft_a_table(n1, n2):
    n = n1 * n2
    k1 = np.arange(n1, dtype=np.float64)[None, :, None]
    tok = (n2 * np.arange(n1, dtype=np.float64)[None, None, :]
           + np.arange(n2, dtype=np.float64)[:, None, None])
    ang = 2.0 * np.pi * np.mod(k1 * tok, n) / n
    c, s = np.cos(ang), np.sin(ang)
    re_rows = np.concatenate([c, s], axis=2)
    im_rows = np.concatenate([-s, c], axis=2)
    return np.stack([re_rows, im_rows], axis=2).reshape(n2, 2 * n1, 2 * n1).astype(np.float32)


@functools.lru_cache(maxsize=None)
def _fft_b_table(n2, scale):
    ang = 2.0 * np.pi * np.outer(np.arange(n2), np.arange(n2)) / n2
    return (scale * np.stack([np.cos(ang), np.sin(ang)], axis=-1).reshape(n2, 2 * n2)).astype(np.float32)


def _mem_kv_kernel(mem_ref, nw_ref, wk_ref, wv_ref, wq_ref, wo_ref, wqk_ref, wvo_ref):
    m, d = mem_ref.shape[1], mem_ref.shape[2]
    dh = d // CA_HEADS
    mn = _rms(mem_ref[0], nw_ref[...]).astype(BF16)
    ck = _dot(mn, wk_ref[...]).astype(BF16)
    cv = _dot(mn, wv_ref[...]).astype(BF16)
    for h in range(CA_HEADS):
        cols = slice(h * dh, (h + 1) * dh)
        qk = lax.dot_general(wq_ref[:, cols], ck[:, cols], (((1,), (1,)), ((), ())),
                             preferred_element_type=F32) * (dh ** -0.5)
        wqk_ref[0, :, h * m:(h + 1) * m] = qk.astype(BF16)
        wvo_ref[0, h * m:(h + 1) * m, :] = _dot(cv[:, cols], wo_ref[cols, :]).astype(BF16)


def _mem_kv(mem, norm_w, w_ck, w_cv, w_cq, w_co):
    b, m, d = mem.shape
    hm = CA_HEADS * m
    sq = _resident((d, d))
    return pl.pallas_call(
        _mem_kv_kernel,
        out_shape=(jax.ShapeDtypeStruct((b, d, hm), BF16), jax.ShapeDtypeStruct((b, hm, d), BF16)),
        grid=(b,),
        in_specs=[pl.BlockSpec((1, m, d), lambda i: (i, 0, 0)), _resident((1, d)), sq, sq, sq, sq],
        out_specs=(pl.BlockSpec((1, d, hm), lambda i: (i, 0, 0)),
                   pl.BlockSpec((1, hm, d), lambda i: (i, 0, 0))),
        compiler_params=pltpu.CompilerParams(
            dimension_semantics=("arbitrary",),
            vmem_limit_bytes=_vmem_limit(4 * d * d * 2 + 4 * m * d * 4 + 4 * d * hm * 2
                                         + 2 * d * hm * 4)),
        name="mem_kv",
    )(mem, norm_w, w_ck, w_cv, w_cq, w_co)


def _log_sigmoid(x):
    return jnp.minimum(x, 0.0) - jnp.log1p(jnp.exp(-jnp.abs(x)))


def _tn_dot(a, b):
    return lax.dot_general(a, b, (((0,), (0,)), ((), ())), preferred_element_type=F32)


def _decay_rows(dec, c, dk, offset, sign):
    lg = _log_sigmoid(jnp.full((c, dk), dec, F32))
    row = lax.broadcasted_iota(jnp.int32, (c, dk), 0).astype(F32)
    return jnp.exp(lg * (offset + sign * row))


def _chunk_decay(dec, c, dk):
    return jnp.exp(_log_sigmoid(jnp.full((1, dk), dec, F32)) * c)


def _ret_bwd_init(dec_ref, state_ref, zeta_ref, gc_ref):
    c, dk = zeta_ref.shape[1], zeta_ref.shape[2]

    @pl.when(pl.program_id(1) == 0)
    def _init():
        state_ref[...] = jnp.zeros_like(state_ref)
        for h in range(RET_HEADS):
            zeta_ref[h] = _decay_rows(dec_ref[h], c, dk, 0.0, 1.0).astype(BF16)
            gc_ref[h] = _chunk_decay(dec_ref[h], c, dk)


def _ret_bwd_states(k_ref, v_ref, sb_ref, state_ref, zeta_ref, gc_ref):
    c = RET_CHUNK
    t_len, d = k_ref.shape[1], k_ref.shape[2]
    dk = d // RET_HEADS
    for h in range(RET_HEADS):
        cols = slice(h * dk, (h + 1) * dk)
        state = state_ref[h]
        for ci in reversed(range(t_len // c)):
            rows = slice(ci * c, (ci + 1) * c)
            sb_ref[0, ci, h] = state.astype(BF16)
            state = state * gc_ref[h] + _tn_dot(k_ref[0, rows, cols],
                                                v_ref[0, rows, cols] * zeta_ref[h])
        state_ref[h] = state


def _in_proj_kernel(dec_ref, x_ref, nw_ref, w_ref, cos_ref, sin_ref,
                    q_ref, k_ref, v_ref, g_ref, z_ref, gr_ref, gf_ref, sb_ref,
                    perm_ref, state_ref, zeta_ref, gc_ref):
    _ret_bwd_init(dec_ref, state_ref, zeta_ref, gc_ref)
    d = x_ref.shape[-1]
    dk = d // RET_HEADS
    half = dk // 2
    tm = x_ref.shape[1]
    hb = tm // ROW_BLOCKS
    blocks = [slice(i * hb, (i + 1) * hb) for i in range(ROW_BLOCKS)]
    xb = [_rms(x_ref[0, rows, :], nw_ref[...]).astype(BF16) for rows in blocks]

    def proj(j):
        return [_dot(b, w_ref[:, j * d:(j + 1) * d]) for b in xb]

    def rope(ps, out_ref, scale):
        for p, rows in zip(ps, blocks):
            cos = cos_ref[rows, :]
            sin = sin_ref[rows, :]
            for h in range(RET_HEADS):
                x1 = p[:, h * dk:h * dk + half]
                x2 = p[:, h * dk + half:(h + 1) * dk]
                o1 = x1 * cos - x2 * sin
                o2 = x2 * cos + x1 * sin
                if scale is not None:
                    o1 = o1 * scale
                    o2 = o2 * scale
                out_ref[0, rows, h * dk:h * dk + half] = o1.astype(BF16)
                out_ref[0, rows, h * dk + half:(h + 1) * dk] = o2.astype(BF16)

    def store(ps, out_ref):
        for p, rows in zip(ps, blocks):
            out_ref[0, rows, :] = p.astype(BF16)

    rope(proj(0), q_ref, dk ** -0.5)
    rope(proj(1), k_ref, None)
    store(proj(2), v_ref)
    _ret_bwd_states(k_ref, v_ref, sb_ref, state_ref, zeta_ref, gc_ref)
    store(proj(3), g_ref)
    us = proj(4)
    for lc in range(d // V7X_LANES):
        for blk in range(tm // FFT_N2):
            a, par = divmod(blk, 2)
            u = us[blk * FFT_N2 // hb]
            r0 = blk * FFT_N2 % hb
            perm_ref[lc, pl.ds(2 * FFT_N2 * a + par, FFT_N2, stride=2), :] = (
                u[r0:r0 + FFT_N2, lc * V7X_LANES:(lc + 1) * V7X_LANES])
    for lc in range(d // V7X_LANES):
        z_ref[0, :, lc * V7X_LANES:(lc + 1) * V7X_LANES] = _words(perm_ref[lc])
    store(proj(5), gr_ref)
    store(proj(6), gf_ref)


def _in_proj(x, norm_w, w_in, cos, sin, decay_bwd, tm):
    b, s, d = x.shape
    dk = d // RET_HEADS
    half = dk // 2
    c = RET_CHUNK
    nt = s // tm
    act = jax.ShapeDtypeStruct((b, s, d), BF16)
    assert tm % (2 * FFT_N2) == 0 and tm % c == 0 and (tm // ROW_BLOCKS) % FFT_N2 == 0
    zsh = jax.ShapeDtypeStruct((b, s // 2, d), jnp.int32)
    sbsh = jax.ShapeDtypeStruct((b, s // c, RET_HEADS, dk, dk), BF16)
    row = pl.BlockSpec((1, tm, d), lambda i, j: (i, nt - 1 - j, 0))
    zrow = pl.BlockSpec((1, tm // 2, d), lambda i, j: (i, nt - 1 - j, 0))
    sbrow = pl.BlockSpec((1, tm // c, RET_HEADS, dk, dk), lambda i, j: (i, nt - 1 - j, 0, 0, 0))
    tab = pl.BlockSpec((tm, half), lambda i, j: (nt - 1 - j, 0))
    vmem = (w_in.size * 2 + 2 * tm * d * 4 + 2 * 8 * tm * d * 2 + 4 * tm * half * 4
            + 5 * tm * d * 4 + 2 * tm * d * 2 + RET_HEADS * (dk * dk * 4 + c * dk * 2))
    return pl.pallas_call(
        _in_proj_kernel,
        out_shape=(act, act, act, act, zsh, act, act, sbsh),
        grid=(b, nt),
        in_specs=[pl.BlockSpec(memory_space=pltpu.SMEM), row, _resident((1, d)),
                  _resident(w_in.shape), tab, tab],
        out_specs=(row, row, row, row, zrow, row, row, sbrow),
        scratch_shapes=[pltpu.VMEM((d // V7X_LANES, tm, V7X_LANES), F32),
                        pltpu.VMEM((RET_HEADS, dk, dk), F32),
                        pltpu.VMEM((RET_HEADS, c, dk), BF16),
                        pltpu.VMEM((RET_HEADS, 1, dk), F32)],
        compiler_params=pltpu.CompilerParams(
            dimension_semantics=("arbitrary", "arbitrary"),
            vmem_limit_bytes=_vmem_limit(vmem)),
        name="in_proj",
    )(decay_bwd, x, norm_w, w_in, cos, sin)


def _ret_fwd_kernel(decf_ref, decb_ref, q_ref, k_ref, v_ref, sb_ref, r_ref,
                    state_ref, dmat_ref, xif_ref, xib_ref, zeta_ref, gc_ref):
    c = RET_CHUNK
    t_len, d = q_ref.shape[1], q_ref.shape[2]
    dk = d // RET_HEADS

    @pl.when(pl.program_id(1) == 0)
    def _init():
        state_ref[...] = jnp.zeros_like(state_ref)
        for h in range(RET_HEADS):
            xif_ref[h] = _decay_rows(decf_ref[h], c, dk, 1.0, 1.0).astype(BF16)
            xib_ref[h] = _decay_rows(decb_ref[h], c, dk, float(c), -1.0).astype(BF16)
            zeta_ref[h] = _decay_rows(decf_ref[h], c, dk, c - 1.0, -1.0).astype(BF16)
            gc_ref[h] = _chunk_decay(decf_ref[h], c, dk)
            i = lax.broadcasted_iota(jnp.int32, (c, c), 0)
            j = lax.broadcasted_iota(jnp.int32, (c, c), 1)
            diff = (i - j).astype(F32)
            lgf = _log_sigmoid(jnp.full((c, c), decf_ref[h], F32))
            lgb = _log_sigmoid(jnp.full((c, c), decb_ref[h], F32))
            dmat_ref[h] = jnp.where(i >= j, jnp.exp(lgf * jnp.maximum(diff, 0.0)),
                                    jnp.exp(lgb * jnp.maximum(-diff, 0.0))).astype(BF16)

    states = [state_ref[h] for h in range(RET_HEADS)]
    for ci in range(t_len // c):
        rows = slice(ci * c, (ci + 1) * c)
        scores = []
        for h in range(RET_HEADS):
            cols = slice(h * dk, (h + 1) * dk)
            scores.append(lax.dot_general(q_ref[0, rows, cols], k_ref[0, rows, cols],
                                          (((1,), (1,)), ((), ())), preferred_element_type=F32))
        for h in range(RET_HEADS):
            cols = slice(h * dk, (h + 1) * dk)
            q = q_ref[0, rows, cols]
            v = v_ref[0, rows, cols]
            lhs = jnp.concatenate([scores[h].astype(BF16) * dmat_ref[h],
                                   q * xif_ref[h], q * xib_ref[h]], axis=1)
            rhs = jnp.concatenate([v, states[h].astype(BF16), sb_ref[0, ci, h]], axis=0)
            y = _dot(lhs, rhs)
            states[h] = states[h] * gc_ref[h] + _tn_dot(k_ref[0, rows, cols], v * zeta_ref[h])
            yc = y - jnp.mean(y, axis=-1, keepdims=True)
            var = jnp.mean(yc * yc, axis=-1, keepdims=True)
            r_ref[0, rows, cols] = (yc * lax.rsqrt(var + GN_EPS)).astype(BF16)
    for h in range(RET_HEADS):
        state_ref[h] = states[h]


def _fft_a_kernel(z_ref, dft_ref, tab_ref, o_ref, zin_ref, zout_ref):
    for r0 in range(0, z_ref.shape[2], SUBLANES):
        _fft_a_columns(z_ref, dft_ref, tab_ref, o_ref, zin_ref, zout_ref, r0)


def _fft_a_columns(z_ref, dft_ref, tab_ref, o_ref, zin_ref, zout_ref, r0):
    na, rn, d = z_ref.shape[1], SUBLANES, z_ref.shape[3]
    gd = dft_ref.shape[0]
    lanes = range(d // V7X_LANES)
    for a in range(na):
        for lc in lanes:
            zin_ref[lc, a * STAGE_PITCH:a * STAGE_PITCH + rn, :] = (
                z_ref[0, a, r0:r0 + rn, lc * V7X_LANES:(lc + 1) * V7X_LANES])
    u = jnp.concatenate(
        [_rows(jnp.concatenate([zin_ref[lc, pl.ds(r, na, stride=STAGE_PITCH), :] for lc in lanes],
                               axis=1)) for r in range(rn)], axis=0)
    n1 = u.shape[0] // rn
    parts = [_dot(u[:, gi * gd:(gi + 1) * gd], dft_ref[...]) for gi in range(d // gd)]
    re = jnp.concatenate([p[:, :gd] for p in parts], axis=1).astype(BF16)
    im = jnp.concatenate([p[:, gd:] for p in parts], axis=1).astype(BF16)
    for r in range(rn):
        rows = slice(r * n1, (r + 1) * n1)
        zz = jnp.concatenate([re[rows], im[rows]], axis=0)
        w = _words(_dot(tab_ref[r0 + r], zz))
        for lc in lanes:
            zout_ref[lc, pl.ds(r, n1, stride=STAGE_PITCH), :] = w[:, lc * V7X_LANES:(lc + 1) * V7X_LANES]
    for k in range(n1):
        for lc in lanes:
            o_ref[0, k, r0:r0 + rn, lc * V7X_LANES:(lc + 1) * V7X_LANES] = (
                zout_ref[lc, k * STAGE_PITCH:k * STAGE_PITCH + rn, :])


def _fft_b_kernel(zt_ref, tab_ref, o_ref):
    for i in range(zt_ref.shape[1]):
        o_ref[0, :, i, :] = _words(_dot(tab_ref[...], _rows(zt_ref[0, i])))


def _ret_fwd_fft_b_kernel(decf_ref, decb_ref, q_ref, k_ref, v_ref, sb_ref,
                          zt_ref, tab_ref, r_ref, f_ref,
                          state_ref, dmat_ref, xif_ref, xib_ref, zeta_ref, gc_ref):
    _ret_fwd_kernel(decf_ref, decb_ref, q_ref, k_ref, v_ref, sb_ref, r_ref,
                    state_ref, dmat_ref, xif_ref, xib_ref, zeta_ref, gc_ref)
    _fft_b_kernel(zt_ref, tab_ref, f_ref)


def _mixers(q, k, v, sb, z, decay_fwd, decay_bwd, dft, tab_a, tab_b):
    b, s, d = q.shape
    dk = d // RET_HEADS
    c = RET_CHUNK
    n2 = tab_b.shape[0]
    n1 = s // n2
    steps_b = n1 // SUBLANES
    t_b = s // steps_b
    assert t_b % c == 0
    smem = pl.BlockSpec(memory_space=pltpu.SMEM)
    fwd = pl.BlockSpec((1, t_b, d), lambda i, t: (i, t, 0))
    cols_a = FFT_A_TILES * SUBLANES
    assert n2 % cols_a == 0
    xi = pltpu.VMEM((RET_HEADS, c, dk), BF16)
    table_bytes = RET_HEADS * (3 * c * dk + c * c + dk * dk + dk) * 4
    stage_b_bytes = 2 * (SUBLANES * n2 * d * 4 + (n2 // 2) * SUBLANES * d * 4) + 4 * n2 * d * 4
    zt = pl.pallas_call(
        _fft_a_kernel,
        out_shape=jax.ShapeDtypeStruct((b, n1, n2, d), jnp.int32),
        grid=(n2 // cols_a, b),
        in_specs=[pl.BlockSpec((1, n1 // 2, cols_a, d), lambda t, i: (i, 0, t, 0)),
                  _resident(dft.shape),
                  pl.BlockSpec((cols_a, 2 * n1, 2 * n1), lambda t, i: (t, 0, 0))],
        out_specs=pl.BlockSpec((1, n1, cols_a, d), lambda t, i: (i, 0, t, 0)),
        scratch_shapes=[pltpu.VMEM((d // V7X_LANES, n1 // 2 * STAGE_PITCH, V7X_LANES), jnp.int32),
                        pltpu.VMEM((d // V7X_LANES, n1 * STAGE_PITCH, V7X_LANES), jnp.int32)],
        compiler_params=pltpu.CompilerParams(
            dimension_semantics=("parallel", "parallel"),
            vmem_limit_bytes=_vmem_limit(3 * n1 // 2 * STAGE_PITCH * d * 4
                                         + 2 * (3 * n1 * cols_a * d * 2
                                              + cols_a * 4 * n1 * n1 * 2)
                                         + 5 * n1 * SUBLANES * d * 4)),
        name="fft_a",
    )(z.reshape(b, n1 // 2, n2, d), dft, tab_a)
    return pl.pallas_call(
        _ret_fwd_fft_b_kernel,
        out_shape=(jax.ShapeDtypeStruct((b, s, d), BF16),
                   jax.ShapeDtypeStruct((b, n2 // 2, n1, d), jnp.int32)),
        grid=(b, steps_b),
        in_specs=[smem, smem, fwd, fwd, fwd,
                  pl.BlockSpec((1, t_b // c, RET_HEADS, dk, dk), lambda i, t: (i, t, 0, 0, 0)),
                  pl.BlockSpec((1, SUBLANES, n2, d), lambda i, t: (i, t, 0, 0)),
                  _resident((n2, 2 * n2))],
        out_specs=(fwd, pl.BlockSpec((1, n2 // 2, SUBLANES, d), lambda i, t: (i, 0, t, 0))),
        scratch_shapes=[pltpu.VMEM((RET_HEADS, dk, dk), F32), pltpu.VMEM((RET_HEADS, c, c), BF16),
                        xi, xi, pltpu.VMEM((RET_HEADS, c, dk), BF16),
                        pltpu.VMEM((RET_HEADS, 1, dk), F32)],
        compiler_params=pltpu.CompilerParams(
            dimension_semantics=("arbitrary", "arbitrary"),
            vmem_limit_bytes=_vmem_limit(2 * 5 * t_b * d * 2 + table_bytes + stage_b_bytes)),
        name="ret_fwd_fft_b",
    )(decay_fwd, decay_bwd, q, k, v, sb, zt, tab_b)


def _gain_rows_kernel(w_ref, g_ref, o_ref):
    o_ref[...] = (w_ref[...] * g_ref[...]).astype(BF16)


def _gain_rows(w, gain):
    d_in, d_out = w.shape
    rows = min(d_in, V7X_MXU_DIM)
    return pl.pallas_call(
        _gain_rows_kernel,
        out_shape=jax.ShapeDtypeStruct((d_in, d_out), BF16),
        grid=(d_in // rows,),
        in_specs=[pl.BlockSpec((rows, d_out), lambda i: (i, 0)),
                  pl.BlockSpec((rows, 1), lambda i: (i, 0))],
        out_specs=pl.BlockSpec((rows, d_out), lambda i: (i, 0)),
        compiler_params=pltpu.CompilerParams(dimension_semantics=("parallel",)),
        name="gain_rows",
    )(w, gain.reshape(d_in, 1))


def _tail_kernel(x_ref, r_ref, g_ref, f_ref, gr_ref, gf_ref, wqk_ref, wvo_ref,
                 w_ro_ref, w_fo_ref, w_mo_ref, n_ca_ref,
                 n_mlp_ref, w_up_ref, w_dn_ref, n_fin_ref, o_ref, att_ref, perm_ref):
    d = x_ref.shape[-1]
    m = wqk_ref.shape[2] // CA_HEADS
    ff = w_up_ref.shape[1]
    gq, n1 = f_ref.shape[1], f_ref.shape[2]
    hb = x_ref.shape[1] // ROW_BLOCKS
    lanes = range(d // V7X_LANES)
    blocks = [slice(i * hb, (i + 1) * hb) for i in range(ROW_BLOCKS)]
    def gated(rows):
        g = g_ref[0, rows, :]
        return g * jax.nn.sigmoid(g) * r_ref[0, rows, :]

    ret = [_dot(gated(rows), w_ro_ref[...]) for rows in blocks]
    for i, rows in enumerate(blocks):
        g0, g1 = i * gq // ROW_BLOCKS, (i + 1) * gq // ROW_BLOCKS
        fperm = _dot(_rows(f_ref[0, g0:g1].reshape(hb // 2, d)), w_fo_ref[...])
        for lc in lanes:
            perm_ref[lc, rows, :] = fperm[:, lc * V7X_LANES:(lc + 1) * V7X_LANES]
    xs = []
    for i, rows in enumerate(blocks):
        fou = jnp.concatenate(
            [jnp.concatenate([perm_ref[lc, pl.ds(i * hb + blk // 2 * 2 * n1 + blk % 2, n1,
                                                 stride=2), :] for lc in lanes], axis=1)
             for blk in range(2 * gq // ROW_BLOCKS)], axis=0)
        merged = (jax.nn.sigmoid(gr_ref[0, rows, :].astype(F32)) * ret[i]
                  + jax.nn.sigmoid(gf_ref[0, rows, :].astype(F32)) * fou)
        xs.append(x_ref[0, rows, :] + _dot(merged.astype(BF16), w_mo_ref[...]))

    logits = [_dot(_rms(x, n_ca_ref[...]).astype(BF16), wqk_ref[0]) for x in xs]
    for lg, rows in zip(logits, blocks):
        for h in range(CA_HEADS):
            cols = slice(h * m, (h + 1) * m)
            p = jnp.exp(lg[:, cols] - jnp.max(lg[:, cols], axis=-1, keepdims=True))
            att_ref[rows, cols] = (p / jnp.sum(p, axis=-1, keepdims=True)).astype(BF16)
    xs = [x + _dot(att_ref[rows, :], wvo_ref[0]) for x, rows in zip(xs, blocks)]

    xb = [_rms(x, n_mlp_ref[...]).astype(BF16) for x in xs]
    for j in range(ff // d):
        cols = slice(j * d, (j + 1) * d)
        hid = [jnp.square(jnp.maximum(_dot(b, w_up_ref[:, cols]), 0.0)).astype(BF16) for b in xb]
        xs = [x + _dot(hd, w_dn_ref[cols, :]) for x, hd in zip(xs, hid)]
    for x, rows in zip(xs, blocks):
        o_ref[0, rows, :] = _rms(x, n_fin_ref[...])


def _tail(x, r, g, four, gate_r, gate_f, wqk, wvo, w_ro, w_fo, w_mo, n_ca,
          n_mlp, w_up, w_dn, n_fin, tm):
    b, s, d = x.shape
    hm = wqk.shape[2]
    ff = w_up.shape[1]
    row = pl.BlockSpec((1, tm, d), lambda i, j: (i, j, 0))
    n1 = four.shape[2]
    assert tm % (2 * n1 * ROW_BLOCKS) == 0
    half = pl.BlockSpec((1, tm // (2 * n1), n1, d), lambda i, j: (i, j, 0, 0))
    vec = _resident((1, d))
    sq = _resident((d, d))
    vmem = ((3 * d * d + 2 * d * ff) * 2 + 4 * d * hm * 2 + 2 * tm * d * (4 + 4 + 5 * 2)
            + tm * d * 4 * 8)
    return pl.pallas_call(
        _tail_kernel,
        out_shape=jax.ShapeDtypeStruct((b, s, d), F32),
        grid=(b, s // tm),
        in_specs=[row, row, row, half, row, row,
                  pl.BlockSpec((1, d, hm), lambda i, j: (i, 0, 0)),
                  pl.BlockSpec((1, hm, d), lambda i, j: (i, 0, 0)),
                  sq, sq, sq, vec, vec, _resident((d, ff)), _resident((ff, d)), vec],
        out_specs=row,
        scratch_shapes=[pltpu.VMEM((tm, hm), BF16),
                        pltpu.VMEM((d // V7X_LANES, tm, V7X_LANES), F32)],
        compiler_params=pltpu.CompilerParams(
            dimension_semantics=("parallel", "parallel"),
            vmem_limit_bytes=_vmem_limit(vmem)),
        name="tail",
    )(x, r, g, four, gate_r, gate_f, wqk, wvo, w_ro, w_fo, w_mo, n_ca, n_mlp, w_up, w_dn, n_fin)


def _tiles(s):
    return min(512, s), min(512, s)


def _trunk(x, mem, p):
    b, s, d = x.shape
    dk = d // RET_HEADS
    gd = d // FOUR_GROUPS
    n1 = s // FFT_N2
    assert s % FFT_N2 == 0 and s % RET_CHUNK == 0
    tm_in, tm_tail = _tiles(s)
    cos, sin = (jnp.asarray(t) for t in _rope_tables(s, dk))
    dft = jnp.asarray(_chan_dft(gd)).astype(BF16)
    tab_a = jnp.asarray(_fft_a_table(n1, FFT_N2)).astype(BF16)
    tab_b = jnp.asarray(_fft_b_table(FFT_N2, float(s * gd) ** -0.5)).astype(BF16)

    for l in range(p["w_in"].shape[0]):
        wqk, wvo = _mem_kv(mem, p["norm_mem_w"][l], p["w_ck"][l], p["w_cv"][l],
                           p["w_cq"][l], p["w_co"][l])
        q, k, v, g, z, gate_r, gate_f, sb = _in_proj(x, p["norm_mix_w"][l], p["w_in"][l], cos, sin,
                                                     p["ret_decay_bwd"][l], tm_in)
        r, four = _mixers(q, k, v, sb, z, p["ret_decay_fwd"][l], p["ret_decay_bwd"][l],
                          dft, tab_a, tab_b)
        w_ro = _gain_rows(p["w_ret_out"][l], p["ret_gn_w"][l])
        x = _tail(x, r, g, four, gate_r, gate_f, wqk, wvo, w_ro, p["w_four_out"][l],
                  p["w_mix_out"][l], p["norm_ca_w"][l], p["norm_mlp_w"][l], p["w_up"][l],
                  p["w_down"][l], p["norm_final_w"], tm_tail)
    return x


def kernel(x_prompt, x_sample, mem_prompt, mem_sample, norm_mix_w, w_in, ret_decay_fwd,
           ret_decay_bwd, ret_gn_w, w_ret_out, w_four_out, w_mix_out, norm_ca_w, norm_mem_w,
           w_cq, w_ck, w_cv, w_co, norm_mlp_w, w_up, w_down, norm_final_w):
    depth, d = norm_mix_w.shape
    assert depth == 1, "the tail fuses the final norm into the (single) layer"
    vec = lambda w: w.reshape(depth, 1, d)
    mat = lambda w: w.astype(BF16)
    p = dict(
        norm_mix_w=vec(norm_mix_w), w_in=mat(w_in), ret_decay_fwd=ret_decay_fwd,
        ret_decay_bwd=ret_decay_bwd, ret_gn_w=ret_gn_w, w_ret_out=w_ret_out,
        w_four_out=mat(w_four_out), w_mix_out=mat(w_mix_out), norm_ca_w=vec(norm_ca_w),
        norm_mem_w=vec(norm_mem_w), w_cq=mat(w_cq), w_ck=mat(w_ck), w_cv=mat(w_cv),
        w_co=mat(w_co), norm_mlp_w=vec(norm_mlp_w), w_up=mat(w_up), w_down=mat(w_down),
        norm_final_w=norm_final_w.reshape(1, d))
    return _trunk(x_prompt, mem_prompt, p), _trunk(x_sample, mem_sample, p)
```
